```python
import math
import jax, jax.numpy as jnp
from jax import lax
import numpy as np

D_MODEL = 1024
BATCH = 32
SEQ = 2048
DEPTH = 4

HEAD_DIM = 64
ROPE_THETA = 10000.0
NORM_EPS = 1e-6
Q_BLOCK = 128
NEG = -1e30
D_MIX = D_MODEL
A_HEADS = 4
A_DV = 2 * HEAD_DIM
B_HEADS = 8
C_HEADS = 8
C_GROUPS = 2
C_HPG = C_HEADS // C_GROUPS
L_CMP = 32
CMP_STRIDE = 16
CMP_HIDDEN = 256
L_SEL = 64
N_SEL = 8
WINDOW = 512
N_BRANCH = 3
FORCE_BONUS = 1e4
D_HEADS = 4
D_DK = 128
D_DV = 128
CONV_WIDTH = 4
DN_CHUNK = 64

EVEN_SPLITS = [A_HEADS * 2 * HEAD_DIM, A_HEADS * 2 * HEAD_DIM, A_HEADS * A_DV,
               B_HEADS * HEAD_DIM, B_HEADS * HEAD_DIM, B_HEADS * HEAD_DIM, B_HEADS, D_MIX]
EVEN_IN = sum(EVEN_SPLITS)
ODD_SPLITS = [C_HEADS * HEAD_DIM] + [C_GROUPS * HEAD_DIM] * 6 + [C_HEADS * N_BRANCH,
              D_HEADS * D_DK, D_HEADS * D_DK, D_HEADS * D_DV, D_HEADS, D_HEADS, D_MIX]
ODD_IN = sum(ODD_SPLITS)
CONV_CH = D_HEADS * (2 * D_DK + D_DV)

kernel_name = 'hybrid_diff_fox_nsa_deltanet_trunk'


def _split(a, sizes):
    return jnp.split(a, [int(s) for s in np.cumsum(sizes)[:-1]], axis=-1)


def _rmsnorm(x, g):
    xf = x.astype(jnp.float32)
    y = xf * lax.rsqrt(jnp.mean(xf * xf, axis=-1, keepdims=True) + NORM_EPS)
    return (y * g.astype(jnp.float32)).astype(x.dtype)


def _l2norm(x):
    xf = x.astype(jnp.float32)
    return xf * lax.rsqrt(jnp.sum(xf * xf, axis=-1, keepdims=True) + NORM_EPS)


def _masked_softmax(s, mask):
    return jax.nn.softmax(jnp.where(mask, s, NEG), axis=-1)


def _rope_tables(positions):
    inv = ROPE_THETA ** (-jnp.arange(0, HEAD_DIM, 2, dtype=jnp.float32) / HEAD_DIM)
    ang = positions.astype(jnp.float32)[..., None] * inv
    return jnp.cos(ang)[:, :, None, :], jnp.sin(ang)[:, :, None, :]


def _rope(x, cos, sin):
    x1, x2 = x[..., :HEAD_DIM // 2], x[..., HEAD_DIM // 2:]
    return jnp.concatenate([x1 * cos - x2 * sin, x2 * cos + x1 * sin], axis=-1).astype(x.dtype)


def _query_blocks(a):
    b, s = a.shape[:2]
    return jnp.moveaxis(a.reshape(b, s // Q_BLOCK, Q_BLOCK, *a.shape[2:]), 1, 0)


def _merge_blocks(o):
    o = jnp.moveaxis(o, 0, 1)
    return o.reshape(o.shape[0], -1, *o.shape[3:])


def _causal_block_mask(i, s):
    t = i * Q_BLOCK + jnp.arange(Q_BLOCK)
    return t[:, None] >= jnp.arange(s)[None, :]


def _diff_attention(q1, q2, k1, k2, v, lam):
    s_len = q1.shape[1]
    scale = HEAD_DIM ** -0.5

    def block(args):
        i, qa, qb = args
        mask = _causal_block_mask(i, s_len)

        def probs(q, k):
            s = jnp.einsum('bqhd,bkhd->bhqk', q, k).astype(jnp.float32) * scale
            return _masked_softmax(s, mask)
        p = probs(qa, k1) - lam * probs(qb, k2)
        return jnp.einsum('bhqk,bkhv->bqhv', p.astype(v.dtype), v)

    o = lax.map(block, (jnp.arange(s_len // Q_BLOCK), _query_blocks(q1), _query_blocks(q2)))
    return _merge_blocks(o)


def _forgetting_attention(q, k, v, cum):
    s_len = q.shape[1]
    scale = HEAD_DIM ** -0.5
    cum_h = jnp.moveaxis(cum, 1, 2)

    def block(args):
        i, qb, cb = args
        mask = _causal_block_mask(i, s_len)
        s = jnp.einsum('bqhd,bkhd->bhqk', qb, k).astype(jnp.float32) * scale
        s = s + (jnp.moveaxis(cb, 1, 2)[..., :, None] - cum_h[..., None, :])
        p = _masked_softmax(s, mask)
        return jnp.einsum('bhqk,bkhd->bqhd', p.astype(v.dtype), v)

    o = lax.map(block, (jnp.arange(s_len // Q_BLOCK), _query_blocks(q), _query_blocks(cum)))
    return _merge_blocks(o)


def _compress(kraw, pe, w1, w2):
    b, s_len = kraw.shape[:2]
    m = (s_len - L_CMP) // CMP_STRIDE + 1
    idx = jnp.arange(m)[:, None] * CMP_STRIDE + jnp.arange(L_CMP)[None, :]
    blk = kraw[:, idx] + pe[:, None, :]
    blk = jnp.moveaxis(blk, 3, 2).reshape(b, m, C_GROUPS, L_CMP * HEAD_DIM)
    return jax.nn.silu(blk @ w1) @ w2


def _nsa_compressed(q, kc, vc):
    b, s_len = q.shape[:2]
    m = kc.shape[1]
    qg = q.reshape(b, s_len, C_GROUPS, C_HPG, HEAD_DIM)
    s = jnp.einsum('btghd,bmgd->bghtm', qg, kc).astype(jnp.float32) * HEAD_DIM ** -0.5
    blk_end = jnp.arange(m) * CMP_STRIDE + L_CMP - 1
    mask = blk_end[None, :] <= jnp.arange(s_len)[:, None]
    p = _masked_softmax(s, mask) * mask.any(-1)[:, None]
    o = jnp.einsum('bghtm,bmgd->btghd', p.astype(vc.dtype), vc)
    return o.reshape(b, s_len, C_HEADS, HEAD_DIM), p


def _nsa_select_blocks(p_cmp, s_len):
    m = p_cmp.shape[-1]
    n_blk = s_len // L_SEL
    n_top = min(N_SEL, n_blk)
    start = jnp.arange(m) * CMP_STRIDE
    jstart = jnp.arange(n_blk) * L_SEL
    ov = ((start[:, None] < jstart[None, :] + L_SEL) &
          (start[:, None] + L_CMP > jstart[None, :])).astype(jnp.float32)
    imp = jnp.einsum('bghtm,mj->bgtj', p_cmp, ov)
    cur = jnp.arange(s_len) // L_SEL
    j = jnp.arange(n_blk)
    forced = (j[None, :] == 0) | (j[None, :] == cur[:, None]) | (j[None, :] == cur[:, None] - 1)
    score = jnp.where(j[None, :] > cur[:, None], NEG, imp + jnp.where(forced, FORCE_BONUS, 0.0))
    _, idx = lax.top_k(score, n_top)
    return idx


def _nsa_selected(q, k, v, idx):
    b, s_len = q.shape[:2]
    n_blk = s_len // L_SEL
    n_top = idx.shape[-1]
    kb = k.reshape(b, n_blk, L_SEL, C_GROUPS, HEAD_DIM).transpose(0, 3, 1, 2, 4)
    vb = v.reshape(b, n_blk, L_SEL, C_GROUPS, HEAD_DIM).transpose(0, 3, 1, 2, 4)
    qc = q.reshape(b, n_blk, L_SEL, C_GROUPS, C_HPG, HEAD_DIM).transpose(1, 0, 2, 3, 4, 5)
    ic = idx.reshape(b, C_GROUPS, n_blk, L_SEL, n_top).transpose(2, 0, 1, 3, 4)
    gather = jax.vmap(jax.vmap(lambda blocks, ix: blocks[ix]))
    scale = HEAD_DIM ** -0.5

    def block(args):
        i, qb, ib = args
        kg = gather(kb, ib).reshape(b, C_GROUPS, L_SEL, n_top * L_SEL, HEAD_DIM)
        vg = gather(vb, ib).reshape(b, C_GROUPS, L_SEL, n_top * L_SEL, HEAD_DIM)
        kpos = (ib[..., None] * L_SEL + jnp.arange(L_SEL)).reshape(b, C_GROUPS, L_SEL, n_top * L_SEL)
        t = i * L_SEL + jnp.arange(L_SEL)
        mask = kpos <= t[None, None, :, None]
        s = jnp.einsum('btghd,bgtnd->bghtn', qb, kg).astype(jnp.float32) * scale
        p = _masked_softmax(s, mask[:, :, None])
        return jnp.einsum('bghtn,bgtnd->btghd', p.astype(vg.dtype), vg)

    o = lax.map(block, (jnp.arange(n_blk), qc, ic))
    return jnp.moveaxis(o, 0, 1).reshape(b, s_len, C_HEADS, HEAD_DIM)


def _nsa_window(q, k, v):
    b, s_len = q.shape[:2]
    kpad = jnp.pad(k, ((0, 0), (WINDOW, 0), (0, 0), (0, 0)))
    vpad = jnp.pad(v, ((0, 0), (WINDOW, 0), (0, 0), (0, 0)))
    qb_all = _query_blocks(q.reshape(b, s_len, C_GROUPS, C_HPG, HEAD_DIM))
    scale = HEAD_DIM ** -0.5

    def block(args):
        i, qi = args
        kw = lax.dynamic_slice_in_dim(kpad, i * Q_BLOCK, Q_BLOCK + WINDOW, axis=1)
        vw = lax.dynamic_slice_in_dim(vpad, i * Q_BLOCK, Q_BLOCK + WINDOW, axis=1)
        t = i * Q_BLOCK + jnp.arange(Q_BLOCK)
        s_pos = i * Q_BLOCK - WINDOW + jnp.arange(Q_BLOCK + WINDOW)
        d = t[:, None] - s_pos[None, :]
        mask = (d >= 0) & (d < WINDOW) & (s_pos[None, :] >= 0)
        s = jnp.einsum('bqghd,bkgd->bghqk', qi, kw).astype(jnp.float32) * scale
        p = _masked_softmax(s, mask)
        return jnp.einsum('bghqk,bkgd->bqghd', p.astype(vw.dtype), vw)

    o = lax.map(block, (jnp.arange(s_len // Q_BLOCK), qb_all))
    return _merge_blocks(o).reshape(b, s_len, C_HEADS, HEAD_DIM)


def _causal_conv(x, w):
    ch = x.shape[-1]
    return lax.conv_general_dilated(x, w[:, None, :], window_strides=(1,),
                                    padding=[(CONV_WIDTH - 1, 0)],
                                    dimension_numbers=('NWC', 'WIO', 'NWC'),
                                    feature_group_count=ch)


def _gated_delta_chunked(q, k, v, g, beta):
    b, t_len, h, dk = q.shape
    dv = v.shape[-1]
    c = DN_CHUNK
    n = t_len // c
    f32 = jnp.float32

    def chunks(a):
        a = a.astype(f32).reshape(b, n, c, h, *a.shape[3:])
        return jnp.swapaxes(a, 2, 3)
    q = chunks(q) * dk ** -0.5
    k, v, g, beta = chunks(k), chunks(v), chunks(g), chunks(beta)
    gc = jnp.cumsum(g, axis=-1)
    ci = jnp.arange(c)
    causal = ci[:, None] >= ci[None, :]
    strict = ci[:, None] > ci[None, :]
    diff = gc[..., :, None] - gc[..., None, :]
    decay = jnp.where(causal, jnp.exp(jnp.where(causal, diff, 0.0)), 0.0)
    kb = k * beta[..., None]
    lmat = jnp.where(strict, jnp.einsum('bnhcd,bnhed->bnhce', kb, k) * decay, 0.0)
    a_mat = jnp.eye(c, dtype=f32) + lmat
    rhs = jnp.concatenate([v * beta[..., None], kb * jnp.exp(gc)[..., None]], axis=-1)
    sol = lax.linalg.triangular_solve(a_mat, rhs, left_side=True, lower=True, unit_diagonal=True)
    u, w = sol[..., :dv], sol[..., dv:]
    qk = jnp.einsum('bnhcd,bnhed->bnhce', q, k) * decay
    q_dec = q * jnp.exp(gc)[..., None]
    k_dec = k * jnp.exp(gc[..., -1:] - gc)[..., None]
    g_tot = jnp.exp(gc[..., -1])

    def step(state, inp):
        qk_i, qd_i, kd_i, u_i, w_i, gt_i = inp
        v_new = u_i - jnp.einsum('bhcd,bhdv->bhcv', w_i, state)
        o_i = jnp.einsum('bhcd,bhdv->bhcv', qd_i, state) + jnp.einsum('bhce,bhev->bhcv', qk_i, v_new)
        state = state * gt_i[..., None, None] + jnp.einsum('bhcd,bhcv->bhdv', kd_i, v_new)
        return state, o_i

    xs = tuple(jnp.moveaxis(a, 1, 0) for a in (qk, q_dec, k_dec, u, w, g_tot))
    _, o = lax.scan(step, jnp.zeros((b, h, dk, dv), f32), xs)
    return jnp.transpose(o, (1, 0, 3, 2, 4)).reshape(b, t_len, h, dv)


def _even_mixer(h, cos, sin, w_in, b_forget, lq1, lk1, lq2, lk2, subln_g, layer_idx):
    b, s_len, _ = h.shape
    aq, ak, av, bq, bk, bv, bf, gate = _split(h @ w_in, EVEN_SPLITS)
    aq = _rope(aq.reshape(b, s_len, A_HEADS * 2, HEAD_DIM), cos, sin).reshape(b, s_len, A_HEADS, 2, HEAD_DIM)
    ak = _rope(ak.reshape(b, s_len, A_HEADS * 2, HEAD_DIM), cos, sin).reshape(b, s_len, A_HEADS, 2, HEAD_DIM)
    av = av.reshape(b, s_len, A_HEADS, A_DV)
    lam_init = 0.8 - 0.6 * math.exp(-0.3 * layer_idx)
    f32 = jnp.float32
    lam = (jnp.exp(jnp.sum(lq1.astype(f32) * lk1.astype(f32)))
           - jnp.exp(jnp.sum(lq2.astype(f32) * lk2.astype(f32))) + lam_init)
    oa = _diff_attention(aq[..., 0, :], aq[..., 1, :], ak[..., 0, :], ak[..., 1, :], av, lam)
    oa = _rmsnorm(oa, subln_g) * (1.0 - lam_init)
    shp = (b, s_len, B_HEADS, HEAD_DIM)
    logf = jax.nn.log_sigmoid((bf + b_forget).astype(f32))
    cum = jnp.cumsum(logf, axis=1)
    ob = _forgetting_attention(bq.reshape(shp), bk.reshape(shp), bv.reshape(shp), cum)
    mix = jnp.concatenate([oa.reshape(b, s_len, -1), ob.reshape(b, s_len, -1)], axis=-1)
    return mix * jax.nn.silu(gate)


def _odd_mixer(h, cos, sin, w_in, cmp_pe_k, cmp_pe_v, cmp_w1_k, cmp_w2_k, cmp_w1_v, cmp_w2_v,
               conv_w, a_log, dt_bias, dn_norm_g):
    b, s_len, _ = h.shape
    (cq, kc, vc, ks, vs, kw, vw, cg, dq, dk, dv, da, db, gate) = _split(h @ w_in, ODD_SPLITS)
    kvs = (b, s_len, C_GROUPS, HEAD_DIM)
    q = cq.reshape(b, s_len, C_HEADS, HEAD_DIM)
    q_rot = _rope(q, cos, sin)
    k_cmp = _compress(kc.reshape(kvs), cmp_pe_k, cmp_w1_k, cmp_w2_k)
    v_cmp = _compress(vc.reshape(kvs), cmp_pe_v, cmp_w1_v, cmp_w2_v)
    o_cmp, p_cmp = _nsa_compressed(q, k_cmp, v_cmp)
    idx = _nsa_select_blocks(p_cmp, s_len)
    o_slc = _nsa_selected(q_rot, _rope(ks.reshape(kvs), cos, sin), vs.reshape(kvs), idx)
    o_win = _nsa_window(q_rot, _rope(kw.reshape(kvs), cos, sin), vw.reshape(kvs))
    gc = jax.nn.sigmoid(cg.reshape(b, s_len, C_HEADS, N_BRANCH))
    oc = gc[..., 0:1] * o_cmp + gc[..., 1:2] * o_slc + gc[..., 2:3] * o_win
    qkv = jax.nn.silu(_causal_conv(jnp.concatenate([dq, dk, dv], axis=-1), conv_w))
    dq, dk, dv = _split(qkv, [D_HEADS * D_DK, D_HEADS * D_DK, D_HEADS * D_DV])
    dq = _l2norm(dq.reshape(b, s_len, D_HEADS, D_DK))
    dk = _l2norm(dk.reshape(b, s_len, D_HEADS, D_DK))
    dv = dv.reshape(b, s_len, D_HEADS, D_DV)
    g_dec = -jnp.exp(a_log.astype(jnp.float32)) * jax.nn.softplus((da + dt_bias).astype(jnp.float32))
    beta = jax.nn.sigmoid(db.astype(jnp.float32))
    od = _gated_delta_chunked(dq, dk, dv, g_dec, beta).astype(h.dtype)
    od = _rmsnorm(od, dn_norm_g)
    mix = jnp.concatenate([oc.reshape(b, s_len, -1), od.reshape(b, s_len, -1)], axis=-1)
    return mix * jax.nn.silu(gate)


def setup_inputs(seed: int = 0) -> dict:
    key = jax.random.key(seed)
    ks = jax.random.split(key, 28)
    n_even = (DEPTH + 1) // 2
    n_odd = DEPTH // 2
    f32 = jnp.float32

    def nrm(i, shape, s):
        return s * jax.random.normal(ks[i], shape, f32)
    dt = jnp.exp(jax.random.uniform(ks[24], (n_odd, D_HEADS), f32, math.log(1e-3), math.log(1e-1)))
    return {
        'x': nrm(0, (BATCH, SEQ, D_MODEL), 1.0),
        'c': nrm(1, (BATCH, D_MODEL), 1.0),
        'positions': (jnp.arange(SEQ, dtype=jnp.int32)[None, :]
                      + jax.random.randint(ks[2], (BATCH, 1), 0, 4096, dtype=jnp.int32)),
        'norm_g': 1.0 + nrm(3, (DEPTH, D_MODEL), 0.1),
        'w_mod': nrm(4, (DEPTH, D_MODEL, 3 * D_MODEL), 0.3 * D_MODEL ** -0.5),
        'b_mod': nrm(5, (DEPTH, 3 * D_MODEL), 0.1),
        'w_out': nrm(6, (DEPTH, D_MIX, D_MODEL), D_MIX ** -0.5),
        'final_norm_g': 1.0 + nrm(7, (D_MODEL,), 0.1),
        'w_in_even': nrm(8, (n_even, D_MODEL, EVEN_IN), D_MODEL ** -0.5),
        'b_forget': 3.0 + nrm(9, (n_even, B_HEADS), 0.5),
        'lambda_q1': nrm(10, (n_even, HEAD_DIM), 0.1),
        'lambda_k1': nrm(11, (n_even, HEAD_DIM), 0.1),
        'lambda_q2': nrm(12, (n_even, HEAD_DIM), 0.1),
        'lambda_k2': nrm(13, (n_even, HEAD_DIM), 0.1),
        'subln_g': 1.0 + nrm(14, (n_even, A_DV), 0.1),
        'w_in_odd': nrm(15, (n_odd, D_MODEL, ODD_IN), D_MODEL ** -0.5),
        'cmp_pe_k': nrm(16, (n_odd, L_CMP, HEAD_DIM), 0.1),
        'cmp_pe_v': nrm(17, (n_odd, L_CMP, HEAD_DIM), 0.1),
        'cmp_w1_k': nrm(18, (n_odd, L_CMP * HEAD_DIM, CMP_HIDDEN), (L_CMP * HEAD_DIM) ** -0.5),
        'cmp_w2_k': nrm(19, (n_odd, CMP_HIDDEN, HEAD_DIM), CMP_HIDDEN ** -0.5),
        'cmp_w1_v': nrm(20, (n_odd, L_CMP * HEAD_DIM, CMP_HIDDEN), (L_CMP * HEAD_DIM) ** -0.5),
        'cmp_w2_v': nrm(21, (n_odd, CMP_HIDDEN, HEAD_DIM), CMP_HIDDEN ** -0.5),
        'conv_w': nrm(22, (n_odd, CONV_WIDTH, CONV_CH), 0.5),
        'a_log': jnp.log(jax.random.uniform(ks[23], (n_odd, D_HEADS), f32, 1.0, 16.0)),
        'dt_bias': dt + jnp.log(-jnp.expm1(-dt)),
        'dn_norm_g': 1.0 + nrm(25, (n_odd, D_DV), 0.1),
    }


def reference(x, c, positions, norm_g, w_mod, b_mod, w_out, final_norm_g, w_in_even, b_forget,
              lambda_q1, lambda_k1, lambda_q2, lambda_k2, subln_g, w_in_odd, cmp_pe_k, cmp_pe_v,
              cmp_w1_k, cmp_w2_k, cmp_w1_v, cmp_w2_v, conv_w, a_log, dt_bias, dn_norm_g):
    cos, sin = _rope_tables(positions)
    cf = jax.nn.silu(c)
    for l in range(DEPTH):
        mod = cf @ w_mod[l] + b_mod[l]
        shift, scale, gate = jnp.split(mod, 3, axis=-1)
        h = _rmsnorm(x, norm_g[l]) * (1.0 + scale[:, None, :]) + shift[:, None, :]
        if l % 2 == 0:
            e = l // 2
            y = _even_mixer(h, cos, sin, w_in_even[e], b_forget[e], lambda_q1[e], lambda_k1[e],
                            lambda_q2[e], lambda_k2[e], subln_g[e], l)
        else:
            o = l // 2
            y = _odd_mixer(h, cos, sin, w_in_odd[o], cmp_pe_k[o], cmp_pe_v[o], cmp_w1_k[o], cmp_w2_k[o],
                           cmp_w1_v[o], cmp_w2_v[o], conv_w[o], a_log[o], dt_bias[o], dn_norm_g[o])
        x = x + gate[:, None, :] * (y @ w_out[l])
    return _rmsnorm(x, final_norm_g)
```

```python
import functools
import math

import jax
import jax.numpy as jnp
import numpy as np
from jax import lax
from jax.experimental import pallas as pl
from jax.experimental.pallas import tpu as pltpu

F32 = jnp.float32
BF16 = jnp.bfloat16
HI = lax.Precision.HIGHEST

LANES = 128
HEAD_DIM = 64
HALF = HEAD_DIM // 2
ROPE_THETA = 10000.0
NORM_EPS = 1e-6
NEG = -1e30
MASK_BIG = 1e30
A_HEADS = 4
B_HEADS = 8
C_HEADS = 8
C_GROUPS = 2
C_HPG = C_HEADS // C_GROUPS
L_CMP = 32
CMP_STRIDE = 16
CMP_HIDDEN = 256
L_SEL = 64
N_SEL = 8
WINDOW = 512
N_BRANCH = 3
FORCE_BONUS = 1e4
D_HEADS = 4
D_DK = 128
D_DV = 128
CONV_WIDTH = 4
DN_CHUNK = 64
TOKEN_TILE = 512
VMEM_LIMIT = 56 * 1024 * 1024


def _cparams(sem):
    return pltpu.CompilerParams(dimension_semantics=sem, vmem_limit_bytes=VMEM_LIMIT)


def _dot(a, b, precision=None):
    return jnp.dot(a, b, precision=precision, preferred_element_type=F32)


def _dot_nt(a, b, precision=None):
    return lax.dot_general(a, b, (((1,), (1,)), ((), ())), precision=precision, preferred_element_type=F32)


def _dot_tn(a, b, precision=None):
    return lax.dot_general(a, b, (((0,), (0,)), ((), ())), precision=precision, preferred_element_type=F32)


def _softplus(z):
    return jnp.maximum(z, 0.0) + jnp.log1p(jnp.exp(-jnp.abs(z)))


def _silu(z):
    return z * jax.nn.sigmoid(z)


def _rope_table_kernel(pos_ref, inv_ref, cos_ref, sin_ref):
    ang = pos_ref[...].astype(F32) * inv_ref[...]
    lane = lax.broadcasted_iota(jnp.int32, (1, LANES), 1)
    cos_ref[...] = jnp.cos(ang)
    sin_ref[...] = jnp.where(lane < 2 * HALF, -1.0, 1.0) * jnp.sin(ang)


def _rope_tables(positions):
    b, s = positions.shape
    inv = ROPE_THETA ** (-jnp.arange(0, HEAD_DIM, 2, dtype=F32) / HEAD_DIM)
    inv = jnp.tile(inv, 4).reshape(1, LANES)
    out = jax.ShapeDtypeStruct((b, s, LANES), F32)
    return pl.pallas_call(
        _rope_table_kernel, grid=(b,),
        in_specs=[pl.BlockSpec((None, s, 1), lambda i: (i, 0, 0)),
                  pl.BlockSpec((1, LANES), lambda i: (0, 0))],
        out_specs=[pl.BlockSpec((None, s, LANES), lambda i: (i, 0, 0))] * 2,
        out_shape=[out, out], compiler_params=_cparams(("parallel",)), name="rope_tables",
    )(positions.reshape(b, s, 1), inv)


def _mod_kernel(c_ref, w_ref, b_ref, o_ref):
    o_ref[...] = _dot(_silu(c_ref[...]), w_ref[...], HI) + b_ref[...]


def _modulation(c, w_mod, b_mod):
    depth, d, n = w_mod.shape
    b = c.shape[0]
    tn = 1024
    mod = pl.pallas_call(
        _mod_kernel, grid=(depth, n // tn),
        in_specs=[pl.BlockSpec((b, d), lambda l, j: (0, 0)),
                  pl.BlockSpec((None, d, tn), lambda l, j: (l, 0, j)),
                  pl.BlockSpec((None, 1, tn), lambda l, j: (l, 0, j))],
        out_specs=pl.BlockSpec((None, b, tn), lambda l, j: (l, 0, j)),
        out_shape=jax.ShapeDtypeStruct((depth, b, n), F32),
        compiler_params=_cparams(("parallel", "parallel")), name="modulation",
    )(c, w_mod, b_mod.reshape(depth, 1, n))
    return mod.reshape(depth, b, 3, 1, d)


def _rope_block(blk, cos, sin):
    return blk * cos + pltpu.roll(blk, 2 * HALF, axis=1) * sin


def _proj_kernel(x_ref, shift_ref, scale_ref, g_ref, cos_ref, sin_ref, w_ref, *out_refs, segs):
    x = x_ref[...]
    h = x * lax.rsqrt(jnp.mean(x * x, axis=-1, keepdims=True) + NORM_EPS) * g_ref[...]
    h = (h * (1.0 + scale_ref[...]) + shift_ref[...]).astype(BF16)
    outs = list(out_refs)
    for kind, c0, c1 in segs:
        raw_ref = outs.pop(0) if kind == "rope+raw" else None
        o_ref = outs.pop(0)
        for a in range(c0, c1, 4 * LANES):
            e = min(a + 4 * LANES, c1)
            acc = _dot(h, w_ref[:, a:e])
            if raw_ref is not None:
                raw_ref[:, a - c0:e - c0] = acc
            if kind in ("rope", "rope+raw"):
                cos, sin = cos_ref[...], sin_ref[...]
                for j in range(0, e - a, LANES):
                    o_ref[:, a - c0 + j:a - c0 + j + LANES] = _rope_block(acc[:, j:j + LANES], cos, sin).astype(o_ref.dtype)
            else:
                o_ref[:, a - c0:e - c0] = acc.astype(o_ref.dtype)


def _projection(x, mod_l, g, cos_t, sin_t, w, segs):
    b, s, d = x.shape
    ts = min(TOKEN_TILE, s)
    n = w.shape[1]
    row = lambda width: pl.BlockSpec((None, ts, width), lambda i, j: (i, j, 0))
    out_specs, out_shapes = [], []
    for kind, c0, c1 in segs:
        if kind == "rope+raw":
            out_specs.append(row(c1 - c0))
            out_shapes.append(jax.ShapeDtypeStruct((b, s, c1 - c0), F32))
        out_specs.append(row(c1 - c0))
        out_shapes.append(jax.ShapeDtypeStruct((b, s, c1 - c0), F32 if kind == "f32" else BF16))
    modspec = lambda k: pl.BlockSpec((None, None, 1, d), lambda i, j: (i, k, 0, 0))
    return pl.pallas_call(
        functools.partial(_proj_kernel, segs=segs), grid=(b, s // ts),
        in_specs=[row(d), modspec(0), modspec(1),
                  pl.BlockSpec((1, d), lambda i, j: (0, 0)),
                  row(LANES), row(LANES),
                  pl.BlockSpec((d, n), lambda i, j: (0, 0), pipeline_mode=pl.Buffered(1))],
        out_specs=out_specs, out_shape=out_shapes,
        compiler_params=_cparams(("parallel", "parallel")), name="adaln_in_proj",
    )(x, mod_l, mod_l, g.reshape(1, d), cos_t, sin_t, w)


def _cum_kernel(x_ref, bias_ref, o_ref):
    r, s = x_ref.shape
    ii = lax.broadcasted_iota(jnp.int32, (LANES, LANES), 0)
    jj = lax.broadcasted_iota(jnp.int32, (LANES, LANES), 1)
    upper = (ii <= jj).astype(F32)
    carry = jnp.zeros((r, 1), F32)
    for c0 in range(0, s, LANES):
        z = x_ref[:, c0:c0 + LANES] + bias_ref[...]
        logf = jnp.minimum(z, 0.0) - jnp.log1p(jnp.exp(-jnp.abs(z)))
        loc = _dot(logf, upper, HI) + carry
        o_ref[:, c0:c0 + LANES] = loc
        carry = loc[:, LANES - 1:LANES]


def _forget_cumsum(bf_t, b_forget):
    r, s = bf_t.shape
    bias = jnp.tile(b_forget.astype(F32), r // B_HEADS).reshape(r, 1)
    return pl.pallas_call(
        _cum_kernel, out_shape=jax.ShapeDtypeStruct((r, s), F32),
        compiler_params=pltpu.CompilerParams(vmem_limit_bytes=VMEM_LIMIT), name="forget_cumsum",
    )(bf_t, bias)


def _flash_kernel(*refs, mode, tq, tk, lam_init):
    q_ref, k_ref, v_ref = refs[:3]
    m_ref, l_ref, acc_ref = refs[-3:]
    o_ref = refs[-4]
    extra = refs[3:-4]
    qi, ki = pl.program_id(2), pl.program_id(3)
    nk = pl.num_programs(3)
    if mode == "win":
        kv = qi - 1 + ki
        active = kv >= 0
    else:
        kv = ki
        active = ki <= qi
    lane = lax.broadcasted_iota(jnp.int32, (1, LANES), 1)
    if mode == "fox":
        slot_lanes = (lane < HEAD_DIM, lane >= HEAD_DIM)
    else:
        slot_lanes = ((lane % HEAD_DIM) < HALF, (lane % HEAD_DIM) >= HALF)

    @pl.when(ki == 0)
    def _():
        m_ref[...] = jnp.full(m_ref.shape, NEG, F32)
        l_ref[...] = jnp.zeros(l_ref.shape, F32)
        acc_ref[...] = jnp.zeros(acc_ref.shape, F32)

    @pl.when(active)
    def _():
        q = q_ref[...]
        k = k_ref[...]
        v = v_ref[...]
        rows = qi * tq + lax.broadcasted_iota(jnp.int32, (tq, tk), 0)
        cols = kv * tk + lax.broadcasted_iota(jnp.int32, (tq, tk), 1)
        keep = cols <= rows
        if mode == "win":
            keep = keep & (rows - cols < WINDOW)
        for s in range(2):
            qs = jnp.where(slot_lanes[s], q, jnp.zeros_like(q)) * (HEAD_DIM ** -0.5)
            sc = _dot_nt(qs, k)
            if mode == "fox":
                sc = sc - extra[0][s:s + 1, :]
            if mode == "sel":
                ll = lax.broadcasted_iota(jnp.int32, (LANES, tk), 0)
                cb = (kv * tk + lax.broadcasted_iota(jnp.int32, (LANES, tk), 1)) // L_SEL
                expand = jnp.where(ll == cb + s * (LANES // 4), MASK_BIG, 0.0).astype(BF16)
                sc = sc + _dot(extra[0][...], expand)
            sc = jnp.where(keep, sc, NEG)
            m_prev = m_ref[s]
            m_new = jnp.maximum(m_prev, jnp.max(sc, axis=1, keepdims=True))
            alpha = jnp.exp(m_prev - m_new)
            p = jnp.exp(sc - m_new)
            l_ref[s] = alpha * l_ref[s] + jnp.sum(p, axis=1, keepdims=True)
            acc_ref[s] = alpha * acc_ref[s] + _dot(p.astype(BF16), v)
            m_ref[s] = m_new

    @pl.when(ki == nk - 1)
    def _():
        o0 = acc_ref[0] / l_ref[0]
        o1 = acc_ref[1] / l_ref[1]
        if mode == "diff":
            lq1, lk1, lq2, lk2, subg = (r[...] for r in extra)
            lam = (jnp.exp(jnp.sum(lq1 * lk1, axis=1, keepdims=True))
                   - jnp.exp(jnp.sum(lq2 * lk2, axis=1, keepdims=True)) + lam_init)
            o = o0 - lam * o1
            o = o * lax.rsqrt(jnp.mean(o * o, axis=-1, keepdims=True) + NORM_EPS) * subg
            o_ref[...] = o * (1.0 - lam_init)
        else:
            o_ref[...] = jnp.where(lane < HEAD_DIM, o0, o1)


def _flash(mode, q, k, v, n_qblocks, q0, k0, v0, shared_kv, extra=(), extra_specs=(), lam_init=0.0):
    b, s, _ = q.shape
    tq = tk = min(TOKEN_TILE, s)
    nq = s // tq
    if mode == "win":
        assert WINDOW == tk
        nk = 2
        kv_of = lambda qi, ki: jnp.maximum(qi - 1 + ki, 0)
    else:
        nk = nq
        kv_of = lambda qi, ki: jnp.minimum(ki, qi)
    hsel = (lambda hb: 0) if shared_kv else (lambda hb: hb)
    in_specs = [
        pl.BlockSpec((None, tq, LANES), lambda i, hb, qi, ki: (i, qi, q0 + hb)),
        pl.BlockSpec((None, tk, LANES), lambda i, hb, qi, ki: (i, kv_of(qi, ki), k0 + hsel(hb))),
        pl.BlockSpec((None, tk, LANES), lambda i, hb, qi, ki: (i, kv_of(qi, ki), v0 + hsel(hb))),
    ]
    for spec in extra_specs:
        in_specs.append(spec(tq, tk, kv_of))
    return pl.pallas_call(
        functools.partial(_flash_kernel, mode=mode, tq=tq, tk=tk, lam_init=lam_init),
        grid=(b, n_qblocks, nq, nk), in_specs=in_specs,
        out_specs=pl.BlockSpec((None, tq, LANES), lambda i, hb, qi, ki: (i, qi, hb)),
        out_shape=jax.ShapeDtypeStruct((b, s, n_qblocks * LANES), F32),
        scratch_shapes=[pltpu.VMEM((2, tq, 1), F32), pltpu.VMEM((2, tq, 1), F32),
                        pltpu.VMEM((2, tq, LANES), F32)],
        compiler_params=_cparams(("parallel", "parallel", "parallel", "arbitrary")),
        name="flash_" + mode,
    )(q, k, v, *extra)


def _compress_kernel(xk_ref, xv_ref, w1ab_k, w1_k, pe_k, w2_k, w1ab_v, w1_v, pe_v, w2_v, ok_ref, ov_ref):
    def run(x_ref, w1ab, w1, pe, w2, o_ref):
        cn = x_ref.shape[1]
        pe_term = _dot(pe[...], w1[...], HI)
        out = jnp.zeros((cn, LANES), F32)
        for g in range(C_GROUPS):
            ab = _dot(x_ref[g], w1ab[...])
            first, second = ab[:, :CMP_HIDDEN], ab[:, CMP_HIDDEN:]
            hid = first + pltpu.roll(second, cn - 1, axis=0) + pe_term
            out = out + _dot(_silu(hid).astype(BF16), w2[g])
        o_ref[...] = out
    run(xk_ref, w1ab_k, w1_k, pe_k, w2_k, ok_ref)
    run(xv_ref, w1ab_v, w1_v, pe_v, w2_v, ov_ref)


def _compress(kc_chunks, vc_chunks, weights_k, weights_v):
    b, _, cn, width = kc_chunks.shape
    xspec = pl.BlockSpec((None, C_GROUPS, cn, width), lambda i: (i, 0, 0, 0))
    full = lambda a: pl.BlockSpec(a.shape, lambda i: (0,) * a.ndim)
    out = jax.ShapeDtypeStruct((b, cn, LANES), F32)
    return pl.pallas_call(
        _compress_kernel, grid=(b,),
        in_specs=[xspec, xspec] + [full(a) for a in weights_k + weights_v],
        out_specs=[pl.BlockSpec((None, cn, LANES), lambda i: (i, 0, 0))] * 2,
        out_shape=[out, out], compiler_params=_cparams(("parallel",)), name="nsa_compress",
    )(kc_chunks, vc_chunks, *weights_k, *weights_v)


def _cmp_select_kernel(q_ref, kc_ref, vc_ref, o_ref, sel_ref, *, tq, n_blk, n_cmp):
    qi = pl.program_id(1)
    cn = kc_ref.shape[0]
    kc = kc_ref[...]
    vc = vc_ref[...]
    lane = lax.broadcasted_iota(jnp.int32, (1, LANES), 1)
    slot_lanes = ((lane % HEAD_DIM) < HALF, (lane % HEAD_DIM) >= HALF)
    t_col = qi * tq + lax.broadcasted_iota(jnp.int32, (tq, 1), 0)
    m_row = lax.broadcasted_iota(jnp.int32, (1, cn), 1)
    valid = (m_row * CMP_STRIDE + L_CMP - 1 <= t_col) & (m_row < n_cmp)
    any_valid = (t_col >= L_CMP - 1).astype(F32)
    psum = [jnp.zeros((tq, cn), F32) for _ in range(C_GROUPS)]
    for p_blk in range(C_HPG):
        q = q_ref[:, p_blk * LANES:(p_blk + 1) * LANES]
        outs = []
        for s in range(C_GROUPS):
            qs = jnp.where(slot_lanes[s], q, 0.0) * (HEAD_DIM ** -0.5)
            sc = jnp.where(valid, _dot_nt(qs, kc, HI), NEG)
            e = jnp.exp(sc - jnp.max(sc, axis=1, keepdims=True))
            p = e / jnp.sum(e, axis=1, keepdims=True) * any_valid
            psum[s] = psum[s] + p
            outs.append(_dot(p, vc, HI))
        o_ref[:, p_blk * LANES:(p_blk + 1) * LANES] = jnp.where(lane < HEAD_DIM, outs[0], outs[1])
    jb = lax.broadcasted_iota(jnp.int32, (n_blk, cn), 0)
    mm = lax.broadcasted_iota(jnp.int32, (n_blk, cn), 1)
    overlap = ((mm * CMP_STRIDE < jb * L_SEL + L_SEL) & (mm * CMP_STRIDE + L_CMP > jb * L_SEL)
               & (mm < n_cmp)).astype(F32)
    j = lax.broadcasted_iota(jnp.int32, (n_blk, tq), 0)
    cur = (qi * tq + lax.broadcasted_iota(jnp.int32, (n_blk, tq), 1)) // L_SEL
    forced = (j == 0) | (j == cur) | (j == cur - 1)
    n_top = min(N_SEL, n_blk)
    pad_rows = LANES // 4 - n_blk
    parts = []
    for s in range(C_GROUPS):
        imp = _dot_nt(overlap, psum[s], HI)
        score = jnp.where(j > cur, NEG, imp + jnp.where(forced, FORCE_BONUS, 0.0))
        rank = jnp.zeros((n_blk, tq), jnp.int32)
        for jp in range(n_blk):
            r = score[jp:jp + 1, :]
            rank = rank + ((r > score) | ((r == score) & (jp < j))).astype(jnp.int32)
        parts.append(jnp.where(rank < n_top, 0.0, -1.0))
        if pad_rows:
            parts.append(jnp.zeros((pad_rows, tq), F32))
    parts.append(jnp.zeros((LANES // 2, tq), F32))
    sel_ref[...] = jnp.concatenate(parts, axis=0).T.astype(sel_ref.dtype)


def _cmp_select(q_raw, kcmp, vcmp, n_cmp):
    b, s, width = q_raw.shape
    cn = kcmp.shape[1]
    tq = min(TOKEN_TILE, s)
    n_blk = s // L_SEL
    return pl.pallas_call(
        functools.partial(_cmp_select_kernel, tq=tq, n_blk=n_blk, n_cmp=n_cmp), grid=(b, s // tq),
        in_specs=[pl.BlockSpec((None, tq, width), lambda i, j: (i, j, 0)),
                  pl.BlockSpec((None, cn, LANES), lambda i, j: (i, 0, 0)),
                  pl.BlockSpec((None, cn, LANES), lambda i, j: (i, 0, 0))],
        out_specs=[pl.BlockSpec((None, tq, width), lambda i, j: (i, j, 0)),
                   pl.BlockSpec((None, tq, LANES), lambda i, j: (i, j, 0))],
        out_shape=[jax.ShapeDtypeStruct((b, s, width), F32), jax.ShapeDtypeStruct((b, s, LANES), BF16)],
        compiler_params=_cparams(("parallel", "parallel")), name="nsa_cmp_select",
    )(q_raw, kcmp, vcmp)


def _gdn_kernel(q_ref, k_ref, v_ref, wq_ref, wk_ref, wv_ref, gate_ref, alog_ref, dtb_ref, ng_ref, o_ref,
                qs, ks, vs, gs, bs):
    s_len = q_ref.shape[0]
    c = DN_CHUNK
    rows = lax.broadcasted_iota(jnp.int32, (s_len, 1), 0)

    def conv(x_ref, w_ref):
        x = x_ref[...]
        w = w_ref[...]
        y = x * w[CONV_WIDTH - 1:CONV_WIDTH, :]
        for back in range(1, CONV_WIDTH):
            xs = jnp.where(rows >= back, pltpu.roll(x, back, axis=0), 0.0)
            y = y + xs * w[CONV_WIDTH - 1 - back:CONV_WIDTH - back, :]
        return _silu(y)

    def l2norm(a):
        return a * lax.rsqrt(jnp.sum(a * a, axis=-1, keepdims=True) + NORM_EPS)

    qs[...] = l2norm(conv(q_ref, wq_ref)) * (D_DK ** -0.5)
    ks[...] = l2norm(conv(k_ref, wk_ref))
    vs[...] = conv(v_ref, wv_ref)
    gs[...] = -jnp.exp(alog_ref[...]) * _softplus(gate_ref[:, 0:1] + dtb_ref[...])
    bs[...] = jax.nn.sigmoid(gate_ref[:, 1:2])

    ii = lax.broadcasted_iota(jnp.int32, (c, c), 0)
    jj = lax.broadcasted_iota(jnp.int32, (c, c), 1)
    causal, strict, eye = ii >= jj, ii > jj, ii == jj
    ng = ng_ref[...]

    def chunk(n, state):
        r0 = pl.multiple_of(n * c, c)
        q, k, v = qs[pl.ds(r0, c), :], ks[pl.ds(r0, c), :], vs[pl.ds(r0, c), :]
        g, beta = gs[pl.ds(r0, c), :], bs[pl.ds(r0, c), :]
        g_row = jnp.sum(jnp.where(eye, g, 0.0), axis=0, keepdims=True)
        gc_col = jnp.sum(jnp.where(causal, g_row, 0.0), axis=1, keepdims=True)
        gc_row = jnp.sum(jnp.where(causal, 0.0, g) + jnp.where(eye, g, 0.0), axis=0, keepdims=True)
        decay = jnp.where(causal, jnp.exp(jnp.where(causal, gc_col - gc_row, 0.0)), 0.0)
        eg = jnp.exp(gc_col)
        kb = k * beta
        x = -jnp.where(strict, _dot_nt(kb, k, HI) * decay, 0.0)
        inv = jnp.where(eye, 1.0, 0.0) + x
        for _ in range(int(math.log2(c)) - 1):
            x = _dot(x, x, HI)
            inv = inv + _dot(inv, x, HI)
        sol = _dot(inv, jnp.concatenate([v * beta, kb * eg], axis=1), HI)
        u, w = sol[:, :D_DV], sol[:, D_DV:]
        qk = _dot_nt(q, k, HI) * decay
        g_last = gc_col[c - 1:c, :]
        v_new = u - _dot(w, state, HI)
        o = _dot(q * eg, state, HI) + _dot(qk, v_new, HI)
        state = state * jnp.exp(g_last) + _dot_tn(k * jnp.exp(g_last - gc_col), v_new, HI)
        o_ref[pl.ds(r0, c), :] = o * lax.rsqrt(jnp.mean(o * o, axis=-1, keepdims=True) + NORM_EPS) * ng
        return state

    lax.fori_loop(0, s_len // c, chunk, jnp.zeros((D_DK, D_DV), F32))


def _gated_deltanet(qkv, gates, conv_w, a_log, dt_bias, norm_g):
    b, s, _ = qkv.shape
    col = lambda off: pl.BlockSpec((None, s, LANES), lambda i, h: (i, 0, off + h))
    wcol = lambda off: pl.BlockSpec((CONV_WIDTH, LANES), lambda i, h: (0, off + h))
    scalar = pl.BlockSpec((None, 1, 1), lambda i, h: (h, 0, 0))
    return pl.pallas_call(
        _gdn_kernel, grid=(b, D_HEADS),
        in_specs=[col(0), col(D_HEADS), col(2 * D_HEADS), wcol(0), wcol(D_HEADS), wcol(2 * D_HEADS),
                  pl.BlockSpec((None, None, s, 2), lambda i, h: (i, h, 0, 0)), scalar, scalar,
                  pl.BlockSpec((1, D_DV), lambda i, h: (0, 0))],
        out_specs=pl.BlockSpec((None, s, LANES), lambda i, h: (i, 0, h)),
        out_shape=jax.ShapeDtypeStruct((b, s, D_HEADS * D_DV), F32),
        scratch_shapes=[pltpu.VMEM((s, LANES), F32)] * 3 + [pltpu.VMEM((s, 1), F32)] * 2,
        compiler_params=_cparams(("parallel", "parallel")), name="gated_deltanet",
    )(qkv, qkv, qkv, conv_w, conv_w, conv_w, gates,
      a_log.astype(F32).reshape(D_HEADS, 1, 1), dt_bias.astype(F32).reshape(D_HEADS, 1, 1),
      norm_g.astype(F32).reshape(1, D_DV))


def _out_kernel(*refs, odd, final):
    x_ref, mg_ref, gate_ref, w_ref = refs[:4]
    o_ref = refs[-1]
    rest = list(refs[4:-1])
    fin_ref = rest.pop() if final else None
    half = w_ref.shape[0] // 2
    sg = _silu(gate_ref[...])
    if odd:
        cmp_ref, slc_ref, win_ref, od_ref, small_ref = rest
        lane = lax.broadcasted_iota(jnp.int32, (1, LANES), 1)
        bg = jax.nn.sigmoid(small_ref[...])
        blocks = []
        for p_blk in range(C_HPG):
            sl = slice(p_blk * LANES, (p_blk + 1) * LANES)
            acc = 0.0
            for br, ref in enumerate((cmp_ref, slc_ref, win_ref)):
                ca = 2 * D_HEADS + p_blk * N_BRANCH + br
                cb = 2 * D_HEADS + (p_blk + C_HPG) * N_BRANCH + br
                acc = acc + jnp.where(lane < HEAD_DIM, bg[:, ca:ca + 1], bg[:, cb:cb + 1]) * ref[:, sl]
            blocks.append(acc)
        first = jnp.concatenate(blocks, axis=1)
        second = od_ref[...]
    else:
        first, second = rest[0][...], rest[1][...]
    y = (_dot((first * sg[:, :half]).astype(BF16), w_ref[:half, :])
         + _dot((second * sg[:, half:]).astype(BF16), w_ref[half:, :]))
    out = x_ref[...] + mg_ref[...] * y
    if final:
        out = out * lax.rsqrt(jnp.mean(out * out, axis=-1, keepdims=True) + NORM_EPS) * fin_ref[...]
    o_ref[...] = out


def _out_projection(x, mod_l, gate, w, branches, final_g=None):
    b, s, d = x.shape
    ts = min(TOKEN_TILE, s)
    odd = len(branches) > 2
    row = lambda width: pl.BlockSpec((None, ts, width), lambda i, j: (i, j, 0))
    in_specs = [row(d), pl.BlockSpec((None, None, 1, d), lambda i, j: (i, 2, 0, 0)), row(gate.shape[-1]),
                pl.BlockSpec(w.shape, lambda i, j: (0, 0), pipeline_mode=pl.Buffered(1))]
    in_specs += [row(a.shape[-1]) for a in branches]
    args = [x, mod_l, gate, w, *branches]
    if final_g is not None:
        in_specs.append(pl.BlockSpec((1, d), lambda i, j: (0, 0)))
        args.append(final_g.reshape(1, d))
    return pl.pallas_call(
        functools.partial(_out_kernel, odd=odd, final=final_g is not None), grid=(b, s // ts),
        in_specs=in_specs, out_specs=row(d), out_shape=jax.ShapeDtypeStruct((b, s, d), F32),
        compiler_params=_cparams(("parallel", "parallel")), name="gated_out_proj",
    )(*args)


def _pair_cols(a0, b0):
    a, bb = np.arange(a0, a0 + HEAD_DIM), np.arange(b0, b0 + HEAD_DIM)
    return np.concatenate([a[:HALF], bb[:HALF], a[HALF:], bb[HALF:]])


def _paired_head_order(width):
    pairs = [np.concatenate([np.arange(p * HEAD_DIM, (p + 1) * HEAD_DIM),
                             np.arange((p + C_HPG) * HEAD_DIM, (p + C_HPG + 1) * HEAD_DIM)]) for p in range(C_HPG)]
    return np.concatenate(pairs + [np.arange(C_HEADS * HEAD_DIM, width)])


def _even_layout():
    aq, ak, av = 0, 512, 1024
    bq, bf, gate = 1536, 3072, 3080
    zero = gate + 1024
    cols = [_pair_cols(base + 2 * h * HEAD_DIM, base + (2 * h + 1) * HEAD_DIM) for base in (aq, ak) for h in range(A_HEADS)]
    cols.append(np.arange(av, bf))
    cols.append(np.concatenate([np.arange(bf, bf + B_HEADS), np.full(LANES - B_HEADS, zero)]))
    cols.append(np.arange(gate, gate + 1024))
    segs = (("rope", 0, 1024), ("bf16", 1024, 3072), ("f32", 3072, 3200), ("f32", 3200, 4224))
    return np.concatenate(cols), segs


def _odd_layout():
    cq, kc, vc, ks, vs, kw, vw, cg = 0, 512, 640, 768, 896, 1024, 1152, 1280
    dq, da, db, gate = 1304, 2840, 2844, 2848
    zero = gate + 1024
    cols = [_pair_cols(cq + p * HEAD_DIM, cq + (p + C_HPG) * HEAD_DIM) for p in range(C_HPG)]
    cols += [_pair_cols(ks, ks + HEAD_DIM), _pair_cols(kw, kw + HEAD_DIM)]
    cols += [np.arange(kc, kc + 256), np.arange(vs, vs + LANES), np.arange(vw, vw + LANES)]
    small = np.concatenate([np.arange(da, da + 2 * D_HEADS), np.arange(cg, cg + C_HEADS * N_BRANCH)])
    cols.append(np.concatenate([small, np.full(LANES - small.size, zero)]))
    cols += [np.arange(dq, dq + 1536), gate + _paired_head_order(1024)]
    segs = (("rope+raw", 0, 512), ("rope", 512, 768), ("bf16", 768, 1280), ("f32", 1280, 1408),
            ("f32", 1408, 2944), ("f32", 2944, 3968))
    return np.concatenate(cols), segs


def _permute_cols(w, cols):
    w = jnp.concatenate([w, jnp.zeros((w.shape[0], 1), w.dtype)], axis=1)
    return w[:, cols].astype(BF16)


def _compress_weights(pe, w1, w2, interleave):
    half = L_CMP // 2 * HEAD_DIM
    w1ab = jnp.concatenate([w1[:half], w1[half:]], axis=1).astype(BF16)
    w2p = jnp.zeros((C_GROUPS, CMP_HIDDEN, LANES), F32)
    for g in range(C_GROUPS):
        if interleave:
            w2p = w2p.at[g, :, g * HALF:(g + 1) * HALF].set(w2[:, :HALF])
            w2p = w2p.at[g, :, HEAD_DIM + g * HALF:HEAD_DIM + (g + 1) * HALF].set(w2[:, HALF:])
        else:
            w2p = w2p.at[g, :, g * HEAD_DIM:(g + 1) * HEAD_DIM].set(w2)
    return [w1ab, w1.astype(F32), pe.astype(F32).reshape(1, L_CMP * HEAD_DIM), w2p.astype(BF16)]


def _chunk_layout(a):
    b, s, _ = a.shape
    a = a.reshape(b, s // CMP_STRIDE, CMP_STRIDE, C_GROUPS, HEAD_DIM)
    return a.transpose(0, 3, 1, 2, 4).reshape(b, C_GROUPS, s // CMP_STRIDE, CMP_STRIDE * HEAD_DIM)


def _even_layer(x, mod_l, cos_t, sin_t, layer_idx, g, w_in, b_forget, lq1, lk1, lq2, lk2, subln_g, w_out, final_g):
    b, s, _ = x.shape
    cols, segs = _even_layout()
    qk_a, plain, small, gate = _projection(x, mod_l, g, cos_t, sin_t, _permute_cols(w_in, cols), segs)
    lam_init = 0.8 - 0.6 * math.exp(-0.3 * layer_idx)
    vec = lambda a: a.astype(F32).reshape(1, -1)
    const = lambda width: (lambda tq, tk, kv_of: pl.BlockSpec((1, width), lambda i, hb, qi, ki: (0, 0)))
    oa = _flash("diff", qk_a, qk_a, plain, A_HEADS, 0, A_HEADS, 0, False,
                extra=[vec(lq1), vec(lk1), vec(lq2), vec(lk2), vec(subln_g)],
                extra_specs=[const(HEAD_DIM)] * 4 + [const(LANES)], lam_init=lam_init)
    bf_t = small[:, :, :B_HEADS].transpose(0, 2, 1).reshape(b * B_HEADS, s)
    cum = _forget_cumsum(bf_t, b_forget).reshape(b, B_HEADS // 2, 2, s)
    cum_spec = lambda tq, tk, kv_of: pl.BlockSpec((None, None, 2, tk), lambda i, hb, qi, ki: (i, hb, 0, kv_of(qi, ki)))
    nb = B_HEADS // 2
    ob = _flash("fox", plain, plain, plain, nb, nb, 2 * nb, 3 * nb, False, extra=[cum], extra_specs=[cum_spec])
    return _out_projection(x, mod_l, gate, w_out.astype(BF16), [oa, ob], final_g)


def _odd_layer(x, mod_l, cos_t, sin_t, g, w_in, pe_k, pe_v, w1_k, w2_k, w1_v, w2_v, conv_w, a_log, dt_bias,
               dn_norm_g, w_out, final_g):
    b, s, _ = x.shape
    cols, segs = _odd_layout()
    q_raw, q_rot, k_rot, plain, small, dqkv, gate = _projection(
        x, mod_l, g, cos_t, sin_t, _permute_cols(w_in, cols), segs)
    n_cmp = (s - L_CMP) // CMP_STRIDE + 1
    kcmp, vcmp = _compress(_chunk_layout(plain[:, :, :LANES]), _chunk_layout(plain[:, :, LANES:2 * LANES]),
                           _compress_weights(pe_k, w1_k, w2_k, True), _compress_weights(pe_v, w1_v, w2_v, False))
    o_cmp, sel = _cmp_select(q_raw, kcmp, vcmp, n_cmp)
    sel_spec = lambda tq, tk, kv_of: pl.BlockSpec((None, tq, LANES), lambda i, hb, qi, ki: (i, qi, 0))
    o_slc = _flash("sel", q_rot, k_rot, plain, C_HPG, 0, 0, 2, True, extra=[sel], extra_specs=[sel_spec])
    o_win = _flash("win", q_rot, k_rot, plain, C_HPG, 0, 1, 3, True)
    gates = small[:, :, :2 * D_HEADS].reshape(b, s, 2, D_HEADS).transpose(0, 3, 1, 2)
    od = _gated_deltanet(dqkv, gates, conv_w.astype(F32), a_log, dt_bias, dn_norm_g)
    rows = _paired_head_order(w_out.shape[0])
    return _out_projection(x, mod_l, gate, w_out[rows].astype(BF16), [o_cmp, o_slc, o_win, od, small], final_g)


def kernel(x, c, positions, norm_g, w_mod, b_mod, w_out, final_norm_g, w_in_even, b_forget, lambda_q1, lambda_k1,
           lambda_q2, lambda_k2, subln_g, w_in_odd, cmp_pe_k, cmp_pe_v, cmp_w1_k, cmp_w2_k, cmp_w1_v, cmp_w2_v,
           conv_w, a_log, dt_bias, dn_norm_g):
    depth = norm_g.shape[0]
    cos_t, sin_t = _rope_tables(positions)
    mod = _modulation(c, w_mod, b_mod)
    for l in range(depth):
        final_g = final_norm_g if l == depth - 1 else None
        i = l // 2
        if l % 2 == 0:
            x = _even_layer(x, mod[l], cos_t, sin_t, l, norm_g[l], w_in_even[i], b_forget[i], lambda_q1[i],
                            lambda_k1[i], lambda_q2[i], lambda_k2[i], subln_g[i], w_out[l], final_g)
        else:
            x = _odd_layer(x, mod[l], cos_t, sin_t, norm_g[l], w_in_odd[i], cmp_pe_k[i], cmp_pe_v[i], cmp_w1_k[i],
                           cmp_w2_k[i], cmp_w1_v[i], cmp_w2_v[i], conv_w[i], a_log[i], dt_bias[i], dn_norm_g[i],
                           w_out[l], final_g)
    return x
```

```python
import functools
import math

import jax
import jax.numpy as jnp
import numpy as np
from jax import lax
from jax.experimental import pallas as pl
from jax.experimental.pallas import tpu as pltpu

F32 = jnp.float32
BF16 = jnp.bfloat16
HI = lax.Precision.HIGHEST

LANES = 128
HEAD_DIM = 64
HALF = HEAD_DIM // 2
ROPE_THETA = 10000.0
NORM_EPS = 1e-6
NEG = -1e30
MASK_BIG = 1e30
A_HEADS = 4
B_HEADS = 8
C_HEADS = 8
C_GROUPS = 2
C_HPG = C_HEADS // C_GROUPS
L_CMP = 32
CMP_STRIDE = 16
CMP_HIDDEN = 256
L_SEL = 64
N_SEL = 8
WINDOW = 512
N_BRANCH = 3
FORCE_BONUS = 1e4
D_HEADS = 4
D_DK = 128
D_DV = 128
CONV_WIDTH = 4
DN_CHUNK = 64
TOKEN_TILE = 512
VMEM_LIMIT = 56 * 1024 * 1024


def _cparams(sem):
    return pltpu.CompilerParams(dimension_semantics=sem, vmem_limit_bytes=VMEM_LIMIT)


def _dot(a, b, precision=None):
    return jnp.dot(a, b, precision=precision, preferred_element_type=F32)


def _dot_nt(a, b, precision=None):
    return lax.dot_general(a, b, (((1,), (1,)), ((), ())), precision=precision, preferred_element_type=F32)


def _dot_tn(a, b, precision=None):
    return lax.dot_general(a, b, (((0,), (0,)), ((), ())), precision=precision, preferred_element_type=F32)


def _softplus(z):
    return jnp.maximum(z, 0.0) + jnp.log1p(jnp.exp(-jnp.abs(z)))


def _silu(z):
    return z * jax.nn.sigmoid(z)


def _rope_table_kernel(pos_ref, inv_ref, cos_ref, sin_ref):
    ang = pos_ref[...].astype(F32) * inv_ref[...]
    lane = lax.broadcasted_iota(jnp.int32, (1, LANES), 1)
    cos_ref[...] = jnp.cos(ang)
    sin_ref[...] = jnp.where(lane < 2 * HALF, -1.0, 1.0) * jnp.sin(ang)


def _rope_tables(positions):
    b, s = positions.shape
    inv = ROPE_THETA ** (-jnp.arange(0, HEAD_DIM, 2, dtype=F32) / HEAD_DIM)
    inv = jnp.tile(inv, 4).reshape(1, LANES)
    out = jax.ShapeDtypeStruct((b, s, LANES), F32)
    return pl.pallas_call(
        _rope_table_kernel, grid=(b,),
        in_specs=[pl.BlockSpec((None, s, 1), lambda i: (i, 0, 0)),
                  pl.BlockSpec((1, LANES), lambda i: (0, 0))],
        out_specs=[pl.BlockSpec((None, s, LANES), lambda i: (i, 0, 0))] * 2,
        out_shape=[out, out], compiler_params=_cparams(("parallel",)), name="rope_tables",
    )(positions.reshape(b, s, 1), inv)


def _mod_kernel(c_ref, w_ref, b_ref, o_ref):
    o_ref[...] = _dot(_silu(c_ref[...]), w_ref[...], HI) + b_ref[...]


def _modulation(c, w_mod, b_mod):
    depth, d, n = w_mod.shape
    b = c.shape[0]
    tn = 1024
    mod = pl.pallas_call(
        _mod_kernel, grid=(depth, n // tn),
        in_specs=[pl.BlockSpec((b, d), lambda l, j: (0, 0)),
                  pl.BlockSpec((None, d, tn), lambda l, j: (l, 0, j)),
                  pl.BlockSpec((None, 1, tn), lambda l, j: (l, 0, j))],
        out_specs=pl.BlockSpec((None, b, tn), lambda l, j: (l, 0, j)),
        out_shape=jax.ShapeDtypeStruct((depth, b, n), F32),
        compiler_params=_cparams(("parallel", "parallel")), name="modulation",
    )(c, w_mod, b_mod.reshape(depth, 1, n))
    return mod.reshape(depth, b, 3, 1, d)


def _rope_block(blk, cos, sin):
    return blk * cos + pltpu.roll(blk, 2 * HALF, axis=1) * sin


def _proj_kernel(x_ref, shift_ref, scale_ref, g_ref, cos_ref, sin_ref, w_ref, *out_refs, segs):
    x = x_ref[...]
    h = x * lax.rsqrt(jnp.mean(x * x, axis=-1, keepdims=True) + NORM_EPS) * g_ref[...]
    h = (h * (1.0 + scale_ref[...]) + shift_ref[...]).astype(BF16)
    outs = list(out_refs)
    for kind, c0, c1 in segs:
        raw_ref = outs.pop(0) if kind == "rope+raw" else None
        o_ref = outs.pop(0)
        for a in range(c0, c1, 4 * LANES):
            e = min(a + 4 * LANES, c1)
            acc = _dot(h, w_ref[:, a:e])
            if raw_ref is not None:
                raw_ref[:, a - c0:e - c0] = acc
            if kind in ("rope", "rope+raw"):
                cos, sin = cos_ref[...], sin_ref[...]
                for j in range(0, e - a, LANES):
                    o_ref[:, a - c0 + j:a - c0 + j + LANES] = _rope_block(acc[:, j:j + LANES], cos, sin).astype(o_ref.dtype)
            else:
                o_ref[:, a - c0:e - c0] = acc.astype(o_ref.dtype)


def _projection(x, mod_l, g, cos_t, sin_t, w, segs):
    b, s, d = x.shape
    ts = min(TOKEN_TILE, s)
    n = w.shape[1]
    row = lambda width: pl.BlockSpec((None, ts, width), lambda i, j: (i, j, 0))
    out_specs, out_shapes = [], []
    for kind, c0, c1 in segs:
        if kind == "rope+raw":
            out_specs.append(row(c1 - c0))
            out_shapes.append(jax.ShapeDtypeStruct((b, s, c1 - c0), F32))
        out_specs.append(row(c1 - c0))
        out_shapes.append(jax.ShapeDtypeStruct((b, s, c1 - c0), F32 if kind == "f32" else BF16))
    modspec = lambda k: pl.BlockSpec((None, None, 1, d), lambda i, j: (i, k, 0, 0))
    return pl.pallas_call(
        functools.partial(_proj_kernel, segs=segs), grid=(b, s // ts),
        in_specs=[row(d), modspec(0), modspec(1),
                  pl.BlockSpec((1, d), lambda i, j: (0, 0)),
                  row(LANES), row(LANES),
                  pl.BlockSpec((d, n), lambda i, j: (0, 0), pipeline_mode=pl.Buffered(1))],
        out_specs=out_specs, out_shape=out_shapes,
        compiler_params=_cparams(("parallel", "parallel")), name="adaln_in_proj",
    )(x, mod_l, mod_l, g.reshape(1, d), cos_t, sin_t, w)


def _cum_kernel(x_ref, bias_ref, o_ref):
    r, s = x_ref.shape
    ii = lax.broadcasted_iota(jnp.int32, (LANES, LANES), 0)
    jj = lax.broadcasted_iota(jnp.int32, (LANES, LANES), 1)
    upper = (ii <= jj).astype(F32)
    carry = jnp.zeros((r, 1), F32)
    for c0 in range(0, s, LANES):
        z = x_ref[:, c0:c0 + LANES] + bias_ref[...]
        logf = jnp.minimum(z, 0.0) - jnp.log1p(jnp.exp(-jnp.abs(z)))
        loc = _dot(logf, upper, HI) + carry
        o_ref[:, c0:c0 + LANES] = loc
        carry = loc[:, LANES - 1:LANES]


def _forget_cumsum(bf_t, b_forget):
    r, s = bf_t.shape
    bias = jnp.tile(b_forget.astype(F32), r // B_HEADS).reshape(r, 1)
    return pl.pallas_call(
        _cum_kernel, out_shape=jax.ShapeDtypeStruct((r, s), F32),
        compiler_params=pltpu.CompilerParams(vmem_limit_bytes=VMEM_LIMIT), name="forget_cumsum",
    )(bf_t, bias)


def _flash_kernel(*refs, mode, tq, tk, lam_init):
    q_ref, k_ref, v_ref = refs[:3]
    m_ref, l_ref, acc_ref = refs[-3:]
    o_ref = refs[-4]
    extra = refs[3:-4]
    qi, ki = pl.program_id(2), pl.program_id(3)
    nk = pl.num_programs(3)
    if mode == "win":
        kv = qi - 1 + ki
        active = kv >= 0
    else:
        kv = ki
        active = ki <= qi
    lane = lax.broadcasted_iota(jnp.int32, (1, LANES), 1)
    if mode == "fox":
        slot_lanes = (lane < HEAD_DIM, lane >= HEAD_DIM)
    else:
        slot_lanes = ((lane % HEAD_DIM) < HALF, (lane % HEAD_DIM) >= HALF)

    @pl.when(ki == 0)
    def _():
        m_ref[...] = jnp.full(m_ref.shape, NEG, F32)
        l_ref[...] = jnp.zeros(l_ref.shape, F32)
        acc_ref[...] = jnp.zeros(acc_ref.shape, F32)

    @pl.when(active)
    def _():
        q = q_ref[...]
        k = k_ref[...]
        v = v_ref[...]
        rows = qi * tq + lax.broadcasted_iota(jnp.int32, (tq, tk), 0)
        cols = kv * tk + lax.broadcasted_iota(jnp.int32, (tq, tk), 1)
        keep = cols <= rows
        if mode == "win":
            keep = keep & (rows - cols < WINDOW)
        for s in range(2):
            qs = jnp.where(slot_lanes[s], q, jnp.zeros_like(q)) * (HEAD_DIM ** -0.5)
            sc = _dot_nt(qs, k)
            if mode == "fox":
                sc = sc - extra[0][s:s + 1, :]
            if mode == "sel":
                ll = lax.broadcasted_iota(jnp.int32, (LANES, tk), 0)
                cb = (kv * tk + lax.broadcasted_iota(jnp.int32, (LANES, tk), 1)) // L_SEL
                expand = jnp.where(ll == cb + s * (LANES // 4), MASK_BIG, 0.0).astype(BF16)
                sc = sc + _dot(extra[0][...], expand)
            sc = jnp.where(keep, sc, NEG)
            m_prev = m_ref[s]
            m_new = jnp.maximum(m_prev, jnp.max(sc, axis=1, keepdims=True))
            alpha = jnp.exp(m_prev - m_new)
            p = jnp.exp(sc - m_new)
            l_ref[s] = alpha * l_ref[s] + jnp.sum(p, axis=1, keepdims=True)
            acc_ref[s] = alpha * acc_ref[s] + _dot(p.astype(BF16), v)
            m_ref[s] = m_new

    @pl.when(ki == nk - 1)
    def _():
        o0 = acc_ref[0] / l_ref[0]
        o1 = acc_ref[1] / l_ref[1]
        if mode == "diff":
            lq1, lk1, lq2, lk2, subg = (r[...] for r in extra)
            lam = (jnp.exp(jnp.sum(lq1 * lk1, axis=1, keepdims=True))
                   - jnp.exp(jnp.sum(lq2 * lk2, axis=1, keepdims=True)) + lam_init)
            o = o0 - lam * o1
            o = o * lax.rsqrt(jnp.mean(o * o, axis=-1, keepdims=True) + NORM_EPS) * subg
            o_ref[...] = o * (1.0 - lam_init)
        else:
            o_ref[...] = jnp.where(lane < HEAD_DIM, o0, o1)


def _flash(mode, q, k, v, n_qblocks, q0, k0, v0, shared_kv, extra=(), extra_specs=(), lam_init=0.0):
    b, s, _ = q.shape
    tq = tk = min(TOKEN_TILE, s)
    nq = s // tq
    if mode == "win":
        assert WINDOW == tk
        nk = 2
        kv_of = lambda qi, ki: jnp.maximum(qi - 1 + ki, 0)
    else:
        nk = nq
        kv_of = lambda qi, ki: jnp.minimum(ki, qi)
    hsel = (lambda hb: 0) if shared_kv else (lambda hb: hb)
    in_specs = [
        pl.BlockSpec((None, tq, LANES), lambda i, hb, qi, ki: (i, qi, q0 + hb)),
        pl.BlockSpec((None, tk, LANES), lambda i, hb, qi, ki: (i, kv_of(qi, ki), k0 + hsel(hb))),
        pl.BlockSpec((None, tk, LANES), lambda i, hb, qi, ki: (i, kv_of(qi, ki), v0 + hsel(hb))),
    ]
    for spec in extra_specs:
        in_specs.append(spec(tq, tk, kv_of))
    return pl.pallas_call(
        functools.partial(_flash_kernel, mode=mode, tq=tq, tk=tk, lam_init=lam_init),
        grid=(b, n_qblocks, nq, nk), in_specs=in_specs,
        out_specs=pl.BlockSpec((None, tq, LANES), lambda i, hb, qi, ki: (i, qi, hb)),
        out_shape=jax.ShapeDtypeStruct((b, s, n_qblocks * LANES), F32),
        scratch_shapes=[pltpu.VMEM((2, tq, 1), F32), pltpu.VMEM((2, tq, 1), F32),
                        pltpu.VMEM((2, tq, LANES), F32)],
        compiler_params=_cparams(("parallel", "parallel", "parallel", "arbitrary")),
        name="flash_" + mode,
    )(q, k, v, *extra)


def _compress_kernel(xk_ref, xv_ref, w1ab_k, w1_k, pe_k, w2_k, w1ab_v, w1_v, pe_v, w2_v, ok_ref, ov_ref):
    def run(x_ref, w1ab, w1, pe, w2, o_ref):
        cn = x_ref.shape[1]
        pe_term = _dot(pe[...], w1[...], HI)
        out = jnp.zeros((cn, LANES), F32)
        for g in range(C_GROUPS):
            ab = _dot(x_ref[g], w1ab[...])
            first, second = ab[:, :CMP_HIDDEN], ab[:, CMP_HIDDEN:]
            hid = first + pltpu.roll(second, cn - 1, axis=0) + pe_term
            out = out + _dot(_silu(hid).astype(BF16), w2[g])
        o_ref[...] = out
    run(xk_ref, w1ab_k, w1_k, pe_k, w2_k, ok_ref)
    run(xv_ref, w1ab_v, w1_v, pe_v, w2_v, ov_ref)


def _compress(kc_chunks, vc_chunks, weights_k, weights_v):
    b, _, cn, width = kc_chunks.shape
    xspec = pl.BlockSpec((None, C_GROUPS, cn, width), lambda i: (i, 0, 0, 0))
    full = lambda a: pl.BlockSpec(a.shape, lambda i: (0,) * a.ndim)
    out = jax.ShapeDtypeStruct((b, cn, LANES), F32)
    return pl.pallas_call(
        _compress_kernel, grid=(b,),
        in_specs=[xspec, xspec] + [full(a) for a in weights_k + weights_v],
        out_specs=[pl.BlockSpec((None, cn, LANES), lambda i: (i, 0, 0))] * 2,
        out_shape=[out, out], compiler_params=_cparams(("parallel",)), name="nsa_compress",
    )(kc_chunks, vc_chunks, *weights_k, *weights_v)


def _cmp_select_kernel(q_ref, kc_ref, vc_ref, o_ref, sel_ref, *, tq, n_blk, n_cmp):
    qi = pl.program_id(1)
    cn = kc_ref.shape[0]
    kc = kc_ref[...]
    vc = vc_ref[...]
    lane = lax.broadcasted_iota(jnp.int32, (1, LANES), 1)
    slot_lanes = ((lane % HEAD_DIM) < HALF, (lane % HEAD_DIM) >= HALF)
    t_col = qi * tq + lax.broadcasted_iota(jnp.int32, (tq, 1), 0)
    m_row = lax.broadcasted_iota(jnp.int32, (1, cn), 1)
    valid = (m_row * CMP_STRIDE + L_CMP - 1 <= t_col) & (m_row < n_cmp)
    any_valid = (t_col >= L_CMP - 1).astype(F32)
    psum = [jnp.zeros((tq, cn), F32) for _ in range(C_GROUPS)]
    for p_blk in range(C_HPG):
        q = q_ref[:, p_blk * LANES:(p_blk + 1) * LANES]
        outs = []
        for s in range(C_GROUPS):
            qs = jnp.where(slot_lanes[s], q, 0.0) * (HEAD_DIM ** -0.5)
            sc = jnp.where(valid, _dot_nt(qs, kc, HI), NEG)
            e = jnp.exp(sc - jnp.max(sc, axis=1, keepdims=True))
            p = e / jnp.sum(e, axis=1, keepdims=True) * any_valid
            psum[s] = psum[s] + p
            outs.append(_dot(p, vc, HI))
        o_ref[:, p_blk * LANES:(p_blk + 1) * LANES] = jnp.where(lane < HEAD_DIM, outs[0], outs[1])
    jb = lax.broadcasted_iota(jnp.int32, (n_blk, cn), 0)
    mm = lax.broadcasted_iota(jnp.int32, (n_blk, cn), 1)
    overlap = ((mm * CMP_STRIDE < jb * L_SEL + L_SEL) & (mm * CMP_STRIDE + L_CMP > jb * L_SEL)
               & (mm < n_cmp)).astype(F32)
    j = lax.broadcasted_iota(jnp.int32, (n_blk, tq), 0)
    cur = (qi * tq + lax.broadcasted_iota(jnp.int32, (n_blk, tq), 1)) // L_SEL
    forced = (j == 0) | (j == cur) | (j == cur - 1)
    n_top = min(N_SEL, n_blk)
    pad_rows = LANES // 4 - n_blk
    parts = []
    for s in range(C_GROUPS):
        imp = _dot_nt(overlap, psum[s], HI)
        score = jnp.where(j > cur, NEG, imp + jnp.where(forced, FORCE_BONUS, 0.0))
        rank = jnp.zeros((n_blk, tq), jnp.int32)
        for jp in range(n_blk):
            r = score[jp:jp + 1, :]
            rank = rank + ((r > score) | ((r == score) & (jp < j))).astype(jnp.int32)
        parts.append(jnp.where(rank < n_top, 0.0, -1.0))
        if pad_rows:
            parts.append(jnp.zeros((pad_rows, tq), F32))
    parts.append(jnp.zeros((LANES // 2, tq), F32))
    sel_ref[...] = jnp.concatenate(parts, axis=0).T.astype(sel_ref.dtype)


def _cmp_select(q_raw, kcmp, vcmp, n_cmp):
    b, s, width = q_raw.shape
    cn = kcmp.shape[1]
    tq = min(TOKEN_TILE, s)
    n_blk = s // L_SEL
    return pl.pallas_call(
        functools.partial(_cmp_select_kernel, tq=tq, n_blk=n_blk, n_cmp=n_cmp), grid=(b, s // tq),
        in_specs=[pl.BlockSpec((None, tq, width), lambda i, j: (i, j, 0)),
                  pl.BlockSpec((None, cn, LANES), lambda i, j: (i, 0, 0)),
                  pl.BlockSpec((None, cn, LANES), lambda i, j: (i, 0, 0))],
        out_specs=[pl.BlockSpec((None, tq, width), lambda i, j: (i, j, 0)),
                   pl.BlockSpec((None, tq, LANES), lambda i, j: (i, j, 0))],
        out_shape=[jax.ShapeDtypeStruct((b, s, width), F32), jax.ShapeDtypeStruct((b, s, LANES), BF16)],
        compiler_params=_cparams(("parallel", "parallel")), name="nsa_cmp_select",
    )(q_raw, kcmp, vcmp)


GDN_HEADS = 2
GDN_BATCH = 4


def _gdn_kernel(q_ref, k_ref, v_ref, wq_ref, wk_ref, wv_ref, gate_ref, alog_ref, dtb_ref, ng_ref, o_ref,
                qs, ks, vs, gs, bs, mc_s, n_s, gt_s):
    s_len = q_ref.shape[0]
    c = DN_CHUNK
    n_chunks = s_len // c
    rows = lax.broadcasted_iota(jnp.int32, (s_len, 1), 0)

    def conv(x, w):
        y = x * w[CONV_WIDTH - 1:CONV_WIDTH, :]
        for back in range(1, CONV_WIDTH):
            xs = jnp.where(rows >= back, pltpu.roll(x, back, axis=0), 0.0)
            y = y + xs * w[CONV_WIDTH - 1 - back:CONV_WIDTH - back, :]
        return _silu(y)

    def l2norm(a):
        return a * lax.rsqrt(jnp.sum(a * a, axis=-1, keepdims=True) + NORM_EPS)

    lanes = [slice(j * LANES, (j + 1) * LANES) for j in range(GDN_HEADS)]
    for j in range(GDN_HEADS):
        qs[j] = l2norm(conv(q_ref[:, lanes[j]], wq_ref[:, lanes[j]])) * (D_DK ** -0.5)
        ks[j] = l2norm(conv(k_ref[:, lanes[j]], wk_ref[:, lanes[j]]))
        vs[j] = conv(v_ref[:, lanes[j]], wv_ref[:, lanes[j]])
        gate = gate_ref[j]
        gs[j] = -jnp.exp(alog_ref[j]) * _softplus(gate[:, 0:1] + dtb_ref[j])
        bs[j] = jax.nn.sigmoid(gate[:, 1:2])

    sc = GDN_BATCH * c
    ii = lax.broadcasted_iota(jnp.int32, (sc, sc), 0)
    jj = lax.broadcasted_iota(jnp.int32, (sc, sc), 1)
    same = (ii // c) == (jj // c)
    causal, strict, upper, eye = same & (ii >= jj), same & (ii > jj), same & (ii <= jj), ii == jj
    chunk_end = same & (jj % c == c - 1)
    ng = ng_ref[...]

    def load(n, j):
        r0 = pl.multiple_of(n * sc, sc)
        return (qs[j, pl.ds(r0, sc), :], ks[j, pl.ds(r0, sc), :], vs[j, pl.ds(r0, sc), :],
                gs[j, pl.ds(r0, sc), :], bs[j, pl.ds(r0, sc), :])

    def prepare(q, k, v, g, beta):
        g_row = jnp.sum(jnp.where(eye, g, 0.0), axis=0, keepdims=True)
        gc_col = jnp.sum(jnp.where(causal, g_row, 0.0), axis=1, keepdims=True)
        gc_row = jnp.sum(jnp.where(upper, g, 0.0), axis=0, keepdims=True)
        g_last = jnp.sum(jnp.where(chunk_end, gc_row, 0.0), axis=1, keepdims=True)
        decay = jnp.where(causal, jnp.exp(jnp.where(causal, gc_col - gc_row, 0.0)), 0.0)
        eg = jnp.exp(gc_col)
        kb = k * beta
        kbf = k.astype(BF16)
        x = -jnp.where(strict, _dot_nt(kb.astype(BF16), kbf) * decay, 0.0)
        inv = jnp.where(eye, 1.0, 0.0) + x
        xb = x.astype(BF16)
        x = _dot(xb, xb)
        for _ in range(int(math.log2(c)) - 2):
            xb = x.astype(BF16)
            both = _dot(jnp.concatenate([xb, inv.astype(BF16)], axis=0), xb)
            x, inv = both[:sc], inv + both[sc:]
        inv = inv + _dot(inv.astype(BF16), x.astype(BF16))
        sol = _dot(inv.astype(BF16), jnp.concatenate([v * beta, kb * eg], axis=1).astype(BF16)).astype(BF16)
        qk = (_dot_nt(q.astype(BF16), kbf) * decay).astype(BF16)
        qo = _dot(qk, sol)
        kd = k * jnp.exp(g_last - gc_col)
        mn = [_dot(kd[t * c:(t + 1) * c].T.astype(BF16), sol[t * c:(t + 1) * c]) for t in range(GDN_BATCH)]
        g_tot = [jnp.exp(g_last[t * c:t * c + 1]) for t in range(GDN_BATCH)]
        return qo, q * eg, mn, g_tot

    def store(n, j, qo, q_dec, mn, g_tot):
        r0 = pl.multiple_of(n * sc, sc)
        o_ref[pl.ds(r0, sc), lanes[j]] = qo[:, :D_DV]
        qs[j, pl.ds(r0, sc), :] = q_dec - qo[:, D_DV:]
        for t in range(GDN_BATCH):
            n_s[j, n * GDN_BATCH + t] = mn[t][:, :D_DV]
            mc_s[j, n * GDN_BATCH + t] = mn[t][:, D_DV:].astype(BF16)
            gt_s[j, n * GDN_BATCH + t] = jnp.broadcast_to(g_tot[t], (8, LANES))

    def prepare_some(i, carry):
        results = [prepare(*operands) for operands in [load(i, j) for j in range(GDN_HEADS)]]
        for j, res in enumerate(results):
            store(i, j, *res)
        return carry

    lax.fori_loop(0, n_chunks // GDN_BATCH, prepare_some, 0)

    def advance(n, states):
        r0 = pl.multiple_of(n * c, c)
        operands = [(o_ref[pl.ds(r0, c), lanes[j]], qs[j, pl.ds(r0, c), :], gt_s[j, n], mc_s[j, n], n_s[j, n])
                    for j in range(GDN_HEADS)]
        out = []
        for j, (o0, q_eff, g_tot, mc, nn) in enumerate(operands):
            sb = states[j].astype(BF16)
            o = o0 + _dot(q_eff.astype(BF16), sb)
            out.append(states[j] * g_tot[0:1, :] - _dot(mc, sb) + nn)
            operands[j] = o * lax.rsqrt(jnp.mean(o * o, axis=-1, keepdims=True) + NORM_EPS) * ng
        for j in range(GDN_HEADS):
            o_ref[pl.ds(r0, c), lanes[j]] = operands[j]
        return tuple(out)

    lax.fori_loop(0, n_chunks, advance, tuple(jnp.zeros((D_DK, D_DV), F32) for _ in range(GDN_HEADS)))


def _gated_deltanet(qkv, gates, conv_w, a_log, dt_bias, norm_g):
    b, s, _ = qkv.shape
    hp, width = GDN_HEADS, GDN_HEADS * LANES
    groups = D_HEADS // hp
    n_chunks = s // DN_CHUNK
    assert n_chunks % GDN_BATCH == 0
    col = lambda off: pl.BlockSpec((None, s, width), lambda i, h: (i, 0, off + h))
    wcol = lambda off: pl.BlockSpec((CONV_WIDTH, width), lambda i, h: (0, off + h))
    scalar = pl.BlockSpec((hp, 1, 1), lambda i, h: (h, 0, 0))
    return pl.pallas_call(
        _gdn_kernel, grid=(b, groups),
        in_specs=[col(0), col(groups), col(2 * groups), wcol(0), wcol(groups), wcol(2 * groups),
                  pl.BlockSpec((None, hp, s, 2), lambda i, h: (i, h, 0, 0)), scalar, scalar,
                  pl.BlockSpec((1, D_DV), lambda i, h: (0, 0))],
        out_specs=pl.BlockSpec((None, s, width), lambda i, h: (i, 0, h)),
        out_shape=jax.ShapeDtypeStruct((b, s, D_HEADS * D_DV), F32),
        scratch_shapes=[pltpu.VMEM((hp, s, LANES), F32)] * 3 + [pltpu.VMEM((hp, s, 1), F32)] * 2
        + [pltpu.VMEM((hp, n_chunks, D_DK, D_DV), BF16), pltpu.VMEM((hp, n_chunks, D_DK, D_DV), F32),
           pltpu.VMEM((hp, n_chunks, 8, LANES), F32)],
        compiler_params=_cparams(("parallel", "parallel")), name="gated_deltanet",
    )(qkv, qkv, qkv, conv_w, conv_w, conv_w, gates,
      a_log.astype(F32).reshape(D_HEADS, 1, 1), dt_bias.astype(F32).reshape(D_HEADS, 1, 1),
      norm_g.astype(F32).reshape(1, D_DV))


def _out_kernel(*refs, odd, final):
    x_ref, mg_ref, gate_ref, w_ref = refs[:4]
    o_ref = refs[-1]
    rest = list(refs[4:-1])
    fin_ref = rest.pop() if final else None
    half = w_ref.shape[0] // 2
    sg = _silu(gate_ref[...])
    if odd:
        cmp_ref, slc_ref, win_ref, od_ref, small_ref = rest
        lane = lax.broadcasted_iota(jnp.int32, (1, LANES), 1)
        bg = jax.nn.sigmoid(small_ref[...])
        blocks = []
        for p_blk in range(C_HPG):
            sl = slice(p_blk * LANES, (p_blk + 1) * LANES)
            acc = 0.0
            for br, ref in enumerate((cmp_ref, slc_ref, win_ref)):
                ca = 2 * D_HEADS + p_blk * N_BRANCH + br
                cb = 2 * D_HEADS + (p_blk + C_HPG) * N_BRANCH + br
                acc = acc + jnp.where(lane < HEAD_DIM, bg[:, ca:ca + 1], bg[:, cb:cb + 1]) * ref[:, sl]
            blocks.append(acc)
        first = jnp.concatenate(blocks, axis=1)
        second = od_ref[...]
    else:
        first, second = rest[0][...], rest[1][...]
    y = (_dot((first * sg[:, :half]).astype(BF16), w_ref[:half, :])
         + _dot((second * sg[:, half:]).astype(BF16), w_ref[half:, :]))
    out = x_ref[...] + mg_ref[...] * y
    if final:
        out = out * lax.rsqrt(jnp.mean(out * out, axis=-1, keepdims=True) + NORM_EPS) * fin_ref[...]
    o_ref[...] = out


def _out_projection(x, mod_l, gate, w, branches, final_g=None):
    b, s, d = x.shape
    ts = min(TOKEN_TILE, s)
    odd = len(branches) > 2
    row = lambda width: pl.BlockSpec((None, ts, width), lambda i, j: (i, j, 0))
    in_specs = [row(d), pl.BlockSpec((None, None, 1, d), lambda i, j: (i, 2, 0, 0)), row(gate.shape[-1]),
                pl.BlockSpec(w.shape, lambda i, j: (0, 0), pipeline_mode=pl.Buffered(1))]
    in_specs += [row(a.shape[-1]) for a in branches]
    args = [x, mod_l, gate, w, *branches]
    if final_g is not None:
        in_specs.append(pl.BlockSpec((1, d), lambda i, j: (0, 0)))
        args.append(final_g.reshape(1, d))
    return pl.pallas_call(
        functools.partial(_out_kernel, odd=odd, final=final_g is not None), grid=(b, s // ts),
        in_specs=in_specs, out_specs=row(d), out_shape=jax.ShapeDtypeStruct((b, s, d), F32),
        compiler_params=_cparams(("parallel", "parallel")), name="gated_out_proj",
    )(*args)


def _pair_cols(a0, b0):
    a, bb = np.arange(a0, a0 + HEAD_DIM), np.arange(b0, b0 + HEAD_DIM)
    return np.concatenate([a[:HALF], bb[:HALF], a[HALF:], bb[HALF:]])


def _paired_head_order(width):
    pairs = [np.concatenate([np.arange(p * HEAD_DIM, (p + 1) * HEAD_DIM),
                             np.arange((p + C_HPG) * HEAD_DIM, (p + C_HPG + 1) * HEAD_DIM)]) for p in range(C_HPG)]
    return np.concatenate(pairs + [np.arange(C_HEADS * HEAD_DIM, width)])


def _even_layout():
    aq, ak, av = 0, 512, 1024
    bq, bf, gate = 1536, 3072, 3080
    zero = gate + 1024
    cols = [_pair_cols(base + 2 * h * HEAD_DIM, base + (2 * h + 1) * HEAD_DIM) for base in (aq, ak) for h in range(A_HEADS)]
    cols.append(np.arange(av, bf))
    cols.append(np.concatenate([np.arange(bf, bf + B_HEADS), np.full(LANES - B_HEADS, zero)]))
    cols.append(np.arange(gate, gate + 1024))
    segs = (("rope", 0, 1024), ("bf16", 1024, 3072), ("f32", 3072, 3200), ("f32", 3200, 4224))
    return np.concatenate(cols), segs


def _odd_layout():
    cq, kc, vc, ks, vs, kw, vw, cg = 0, 512, 640, 768, 896, 1024, 1152, 1280
    dq, da, db, gate = 1304, 2840, 2844, 2848
    zero = gate + 1024
    cols = [_pair_cols(cq + p * HEAD_DIM, cq + (p + C_HPG) * HEAD_DIM) for p in range(C_HPG)]
    cols += [_pair_cols(ks, ks + HEAD_DIM), _pair_cols(kw, kw + HEAD_DIM)]
    cols += [np.arange(kc, kc + 256), np.arange(vs, vs + LANES), np.arange(vw, vw + LANES)]
    small = np.concatenate([np.arange(da, da + 2 * D_HEADS), np.arange(cg, cg + C_HEADS * N_BRANCH)])
    cols.append(np.concatenate([small, np.full(LANES - small.size, zero)]))
    cols += [np.arange(dq, dq + 1536), gate + _paired_head_order(1024)]
    segs = (("rope+raw", 0, 512), ("rope", 512, 768), ("bf16", 768, 1280), ("f32", 1280, 1408),
            ("f32", 1408, 2944), ("f32", 2944, 3968))
    return np.concatenate(cols), segs


def _permute_cols(w, cols):
    w = jnp.concatenate([w, jnp.zeros((w.shape[0], 1), w.dtype)], axis=1)
    return w[:, cols].astype(BF16)


def _compress_weights(pe, w1, w2, interleave):
    half = L_CMP // 2 * HEAD_DIM
    w1ab = jnp.concatenate([w1[:half], w1[half:]], axis=1).astype(BF16)
    w2p = jnp.zeros((C_GROUPS, CMP_HIDDEN, LANES), F32)
    for g in range(C_GROUPS):
        if interleave:
            w2p = w2p.at[g, :, g * HALF:(g + 1) * HALF].set(w2[:, :HALF])
            w2p = w2p.at[g, :, HEAD_DIM + g * HALF:HEAD_DIM + (g + 1) * HALF].set(w2[:, HALF:])
        else:
            w2p = w2p.at[g, :, g * HEAD_DIM:(g + 1) * HEAD_DIM].set(w2)
    return [w1ab, w1.astype(F32), pe.astype(F32).reshape(1, L_CMP * HEAD_DIM), w2p.astype(BF16)]


def _chunk_layout(a):
    b, s, _ = a.shape
    a = a.reshape(b, s // CMP_STRIDE, CMP_STRIDE, C_GROUPS, HEAD_DIM)
    return a.transpose(0, 3, 1, 2, 4).reshape(b, C_GROUPS, s // CMP_STRIDE, CMP_STRIDE * HEAD_DIM)


def _even_layer(x, mod_l, cos_t, sin_t, layer_idx, g, w_in, b_forget, lq1, lk1, lq2, lk2, subln_g, w_out, final_g):
    b, s, _ = x.shape
    cols, segs = _even_layout()
    qk_a, plain, small, gate = _projection(x, mod_l, g, cos_t, sin_t, _permute_cols(w_in, cols), segs)
    lam_init = 0.8 - 0.6 * math.exp(-0.3 * layer_idx)
    vec = lambda a: a.astype(F32).reshape(1, -1)
    const = lambda width: (lambda tq, tk, kv_of: pl.BlockSpec((1, width), lambda i, hb, qi, ki: (0, 0)))
    oa = _flash("diff", qk_a, qk_a, plain, A_HEADS, 0, A_HEADS, 0, False,
                extra=[vec(lq1), vec(lk1), vec(lq2), vec(lk2), vec(subln_g)],
                extra_specs=[const(HEAD_DIM)] * 4 + [const(LANES)], lam_init=lam_init)
    bf_t = small[:, :, :B_HEADS].transpose(0, 2, 1).reshape(b * B_HEADS, s)
    cum = _forget_cumsum(bf_t, b_forget).reshape(b, B_HEADS // 2, 2, s)
    cum_spec = lambda tq, tk, kv_of: pl.BlockSpec((None, None, 2, tk), lambda i, hb, qi, ki: (i, hb, 0, kv_of(qi, ki)))
    nb = B_HEADS // 2
    ob = _flash("fox", plain, plain, plain, nb, nb, 2 * nb, 3 * nb, False, extra=[cum], extra_specs=[cum_spec])
    return _out_projection(x, mod_l, gate, w_out.astype(BF16), [oa, ob], final_g)


def _odd_layer(x, mod_l, cos_t, sin_t, g, w_in, pe_k, pe_v, w1_k, w2_k, w1_v, w2_v, conv_w, a_log, dt_bias,
               dn_norm_g, w_out, final_g):
    b, s, _ = x.shape
    cols, segs = _odd_layout()
    q_raw, q_rot, k_rot, plain, small, dqkv, gate = _projection(
        x, mod_l, g, cos_t, sin_t, _permute_cols(w_in, cols), segs)
    n_cmp = (s - L_CMP) // CMP_STRIDE + 1
    kcmp, vcmp = _compress(_chunk_layout(plain[:, :, :LANES]), _chunk_layout(plain[:, :, LANES:2 * LANES]),
                           _compress_weights(pe_k, w1_k, w2_k, True), _compress_weights(pe_v, w1_v, w2_v, False))
    o_cmp, sel = _cmp_select(q_raw, kcmp, vcmp, n_cmp)
    sel_spec = lambda tq, tk, kv_of: pl.BlockSpec((None, tq, LANES), lambda i, hb, qi, ki: (i, qi, 0))
    o_slc = _flash("sel", q_rot, k_rot, plain, C_HPG, 0, 0, 2, True, extra=[sel], extra_specs=[sel_spec])
    o_win = _flash("win", q_rot, k_rot, plain, C_HPG, 0, 1, 3, True)
    gates = small[:, :, :2 * D_HEADS].reshape(b, s, 2, D_HEADS).transpose(0, 3, 1, 2)
    od = _gated_deltanet(dqkv, gates, conv_w.astype(F32), a_log, dt_bias, dn_norm_g)
    rows = _paired_head_order(w_out.shape[0])
    return _out_projection(x, mod_l, gate, w_out[rows].astype(BF16), [o_cmp, o_slc, o_win, od, small], final_g)


def kernel(x, c, positions, norm_g, w_mod, b_mod, w_out, final_norm_g, w_in_even, b_forget, lambda_q1, lambda_k1,
           lambda_q2, lambda_k2, subln_g, w_in_odd, cmp_pe_k, cmp_pe_v, cmp_w1_k, cmp_w2_k, cmp_w1_v, cmp_w2_v,
           conv_w, a_log, dt_bias, dn_norm_g):
    depth = norm_g.shape[0]
    cos_t, sin_t = _rope_tables(positions)
    mod = _modulation(c, w_mod, b_mod)
    for l in range(depth):
        final_g = final_norm_g if l == depth - 1 else None
        i = l // 2
        if l % 2 == 0:
            x = _even_layer(x, mod[l], cos_t, sin_t, l, norm_g[l], w_in_even[i], b_forget[i], lambda_q1[i],
                            lambda_k1[i], lambda_q2[i], lambda_k2[i], subln_g[i], w_out[l], final_g)
        else:
            x = _odd_layer(x, mod[l], cos_t, sin_t, norm_g[l], w_in_odd[i], cmp_pe_k[i], cmp_pe_v[i], cmp_w1_k[i],
                           cmp_w2_k[i], cmp_w1_v[i], cmp_w2_v[i], conv_w[i], a_log[i], dt_bias[i], dn_norm_g[i],
                           w_out[l], final_g)
    return x
```

```python
import functools
import math

import jax
import jax.numpy as jnp
import numpy as np
from jax import lax
from jax.experimental import pallas as pl
from jax.experimental.pallas import tpu as pltpu

F32 = jnp.float32
BF16 = jnp.bfloat16
HI = lax.Precision.HIGHEST

LANES = 128
HEAD_DIM = 64
HALF = HEAD_DIM // 2
ROPE_THETA = 10000.0
NORM_EPS = 1e-6
NEG = -1e30
MASK_BIG = 1e30
A_HEADS = 4
B_HEADS = 8
C_HEADS = 8
C_GROUPS = 2
C_HPG = C_HEADS // C_GROUPS
L_CMP = 32
CMP_STRIDE = 16
CMP_HIDDEN = 256
L_SEL = 64
N_SEL = 8
WINDOW = 512
N_BRANCH = 3
FORCE_BONUS = 1e4
D_HEADS = 4
D_DK = 128
D_DV = 128
CONV_WIDTH = 4
DN_CHUNK = 64
TOKEN_TILE = 512
VMEM_LIMIT = 56 * 1024 * 1024


def _cparams(sem):
    return pltpu.CompilerParams(dimension_semantics=sem, vmem_limit_bytes=VMEM_LIMIT)


def _dot(a, b, precision=None):
    return jnp.dot(a, b, precision=precision, preferred_element_type=F32)


def _dot_nt(a, b, precision=None):
    return lax.dot_general(a, b, (((1,), (1,)), ((), ())), precision=precision, preferred_element_type=F32)


def _dot_tn(a, b, precision=None):
    return lax.dot_general(a, b, (((0,), (0,)), ((), ())), precision=precision, preferred_element_type=F32)


def _softplus(z):
    return jnp.maximum(z, 0.0) + jnp.log1p(jnp.exp(-jnp.abs(z)))


def _silu(z):
    return z * jax.nn.sigmoid(z)


def _rope_table_kernel(pos_col_ref, pos_row_ref, inv_row_ref, inv_col_ref, cos_ref, sin_ref, cos_t_ref, sin_t_ref):
    ang = pos_col_ref[...].astype(F32) * inv_row_ref[...]
    lane = lax.broadcasted_iota(jnp.int32, (1, LANES), 1)
    cos_ref[...] = jnp.cos(ang)
    sin_ref[...] = jnp.where(lane < 2 * HALF, -1.0, 1.0) * jnp.sin(ang)
    ang_t = inv_col_ref[...] * pos_row_ref[...].astype(F32)
    row = lax.broadcasted_iota(jnp.int32, (LANES, 1), 0)
    cos_t_ref[...] = jnp.cos(ang_t)
    sin_t_ref[...] = jnp.where(row < 2 * HALF, -1.0, 1.0) * jnp.sin(ang_t)


def _rope_tables(positions):
    b, s = positions.shape
    inv = ROPE_THETA ** (-jnp.arange(0, HEAD_DIM, 2, dtype=F32) / HEAD_DIM)
    inv = jnp.tile(inv, 4)
    tok = jax.ShapeDtypeStruct((b, s, LANES), F32)
    feat = jax.ShapeDtypeStruct((b, LANES, s), F32)
    return pl.pallas_call(
        _rope_table_kernel, grid=(b,),
        in_specs=[pl.BlockSpec((None, s, 1), lambda i: (i, 0, 0)),
                  pl.BlockSpec((None, 1, s), lambda i: (i, 0, 0)),
                  pl.BlockSpec((1, LANES), lambda i: (0, 0)),
                  pl.BlockSpec((LANES, 1), lambda i: (0, 0))],
        out_specs=[pl.BlockSpec((None, s, LANES), lambda i: (i, 0, 0))] * 2
        + [pl.BlockSpec((None, LANES, s), lambda i: (i, 0, 0))] * 2,
        out_shape=[tok, tok, feat, feat], compiler_params=_cparams(("parallel",)), name="rope_tables",
    )(positions.reshape(b, s, 1), positions.reshape(b, 1, s), inv.reshape(1, LANES), inv.reshape(LANES, 1))


def _mod_kernel(c_ref, w_ref, b_ref, o_ref):
    o_ref[...] = _dot(_silu(c_ref[...]), w_ref[...], HI) + b_ref[...]


def _modulation(c, w_mod, b_mod):
    depth, d, n = w_mod.shape
    b = c.shape[0]
    tn = 1024
    mod = pl.pallas_call(
        _mod_kernel, grid=(depth, n // tn),
        in_specs=[pl.BlockSpec((b, d), lambda l, j: (0, 0)),
                  pl.BlockSpec((None, d, tn), lambda l, j: (l, 0, j)),
                  pl.BlockSpec((None, 1, tn), lambda l, j: (l, 0, j))],
        out_specs=pl.BlockSpec((None, b, tn), lambda l, j: (l, 0, j)),
        out_shape=jax.ShapeDtypeStruct((depth, b, n), F32),
        compiler_params=_cparams(("parallel", "parallel")), name="modulation",
    )(c, w_mod, b_mod.reshape(depth, 1, n))
    return mod.reshape(depth, b, 3, 1, d)


def _proj_kernel(x_ref, shift_ref, scale_ref, g_ref, cos_ref, sin_ref, cos_t_ref, sin_t_ref, w_ref, wt_ref,
                 *out_refs, segs, tsegs):
    x = x_ref[...]
    h = x * lax.rsqrt(jnp.mean(x * x, axis=-1, keepdims=True) + NORM_EPS) * g_ref[...]
    h = (h * (1.0 + scale_ref[...]) + shift_ref[...]).astype(BF16)
    outs = list(out_refs)
    step = 4 * LANES
    for kind, c0, c1 in segs:
        o_ref = outs.pop(0)
        for a in range(c0, c1, step):
            e = min(a + step, c1)
            acc = _dot(h, w_ref[:, a:e])
            if kind == "rope":
                cos, sin = cos_ref[...], sin_ref[...]
                for j in range(0, e - a, LANES):
                    blk = acc[:, j:j + LANES]
                    rot = blk * cos + pltpu.roll(blk, 2 * HALF, axis=1) * sin
                    o_ref[:, a - c0 + j:a - c0 + j + LANES] = rot.astype(o_ref.dtype)
            else:
                o_ref[:, a - c0:e - c0] = acc.astype(o_ref.dtype)
    for kind, r0, r1 in tsegs:
        raw_ref = outs.pop(0) if kind == "rope+raw" else None
        o_ref = outs.pop(0)
        for a in range(r0, r1, step):
            e = min(a + step, r1)
            acc = _dot_nt(wt_ref[a:e, :], h)
            if raw_ref is not None:
                raw_ref[a - r0:e - r0, :] = acc
            if kind in ("rope", "rope+raw"):
                cos, sin = cos_t_ref[...], sin_t_ref[...]
                for j in range(0, e - a, LANES):
                    blk = acc[j:j + LANES, :]
                    rot = blk * cos + pltpu.roll(blk, 2 * HALF, axis=0) * sin
                    o_ref[a - r0 + j:a - r0 + j + LANES, :] = rot.astype(o_ref.dtype)
            else:
                o_ref[a - r0:e - r0, :] = acc.astype(o_ref.dtype)


def _projection(x, mod_l, g, rope, w, wt, segs, tsegs):
    b, s, d = x.shape
    ts = min(TOKEN_TILE, s)
    row = lambda width: pl.BlockSpec((None, ts, width), lambda i, j: (i, j, 0))
    col = lambda height: pl.BlockSpec((None, height, ts), lambda i, j: (i, 0, j))
    out_specs, out_shapes = [], []
    for kind, c0, c1 in segs:
        out_specs.append(row(c1 - c0))
        out_shapes.append(jax.ShapeDtypeStruct((b, s, c1 - c0), F32 if kind == "f32" else BF16))
    for kind, r0, r1 in tsegs:
        if kind == "rope+raw":
            out_specs.append(col(r1 - r0))
            out_shapes.append(jax.ShapeDtypeStruct((b, r1 - r0, s), F32))
        out_specs.append(col(r1 - r0))
        out_shapes.append(jax.ShapeDtypeStruct((b, r1 - r0, s), BF16))
    modspec = lambda k: pl.BlockSpec((None, None, 1, d), lambda i, j: (i, k, 0, 0))
    resident = lambda a: pl.BlockSpec(a.shape, lambda i, j: (0, 0), pipeline_mode=pl.Buffered(1))
    return pl.pallas_call(
        functools.partial(_proj_kernel, segs=segs, tsegs=tsegs), grid=(b, s // ts),
        in_specs=[row(d), modspec(0), modspec(1), pl.BlockSpec((1, d), lambda i, j: (0, 0)),
                  row(LANES), row(LANES), col(LANES), col(LANES), resident(w), resident(wt)],
        out_specs=out_specs, out_shape=out_shapes,
        compiler_params=_cparams(("parallel", "parallel")), name="adaln_in_proj",
    )(x, mod_l, mod_l, g.reshape(1, d), *rope, w, wt)


def _cum_kernel(x_ref, bias_ref, o_ref):
    s = x_ref.shape[0]
    ii = lax.broadcasted_iota(jnp.int32, (LANES, LANES), 0)
    jj = lax.broadcasted_iota(jnp.int32, (LANES, LANES), 1)
    lower = (ii >= jj).astype(F32)
    carry = jnp.zeros((1, LANES), F32)
    for r0 in range(0, s, LANES):
        z = x_ref[r0:r0 + LANES, :] + bias_ref[...]
        logf = jnp.minimum(z, 0.0) - jnp.log1p(jnp.exp(-jnp.abs(z)))
        loc = _dot(lower, logf, HI) + carry
        o_ref[r0:r0 + LANES, :] = loc
        carry = loc[LANES - 1:LANES, :]


def _forget_cumsum(logits, b_forget):
    b, s, width = logits.shape
    pairs = width // LANES
    bias = jnp.zeros((pairs, 1, LANES), F32).at[:, 0, :2].set(b_forget.astype(F32).reshape(pairs, 2))
    return pl.pallas_call(
        _cum_kernel, grid=(b, pairs),
        in_specs=[pl.BlockSpec((None, s, LANES), lambda i, p: (i, 0, p)),
                  pl.BlockSpec((None, 1, LANES), lambda i, p: (p, 0, 0))],
        out_specs=pl.BlockSpec((None, None, s, LANES), lambda i, p: (i, p, 0, 0)),
        out_shape=jax.ShapeDtypeStruct((b, pairs, s, LANES), F32),
        compiler_params=_cparams(("parallel", "parallel")), name="forget_cumsum",
    )(logits, bias)


def _flash_kernel(*refs, mode, tq, tk, lam_init):
    qt_ref, k_ref, vt_ref = refs[:3]
    m_ref, l_ref, acc_ref = refs[-3:]
    o_ref = refs[-4]
    extra = refs[3:-4]
    qi, ki = pl.program_id(2), pl.program_id(3)
    nk = pl.num_programs(3)
    kv = qi - 1 + ki if mode == "win" else ki
    row = lax.broadcasted_iota(jnp.int32, (LANES, 1), 0)
    if mode == "fox":
        slot_rows = (row < HEAD_DIM, row >= HEAD_DIM)
    else:
        slot_rows = ((row % HEAD_DIM) < HALF, (row % HEAD_DIM) >= HALF)
    vrows = acc_ref.shape[1]

    @pl.when(ki == 0)
    def _():
        m_ref[...] = jnp.full(m_ref.shape, NEG, F32)
        l_ref[...] = jnp.zeros(l_ref.shape, F32)
        acc_ref[...] = jnp.zeros(acc_ref.shape, F32)

    def step(masked):
        k = k_ref[...]
        qt = qt_ref[...]
        sts = [_dot(k, jnp.where(slot_rows[s], qt, jnp.zeros_like(qt)) * (HEAD_DIM ** -0.5)) for s in range(2)]
        if masked:
            kpos = kv * tk + lax.broadcasted_iota(jnp.int32, (tk, tq), 0)
            qpos = qi * tq + lax.broadcasted_iota(jnp.int32, (tk, tq), 1)
            keep = kpos <= qpos
            if mode == "win":
                keep = keep & (qpos - kpos < WINDOW)
        for s in range(2):
            st = sts[s]
            if mode == "fox":
                st = st - extra[0][:, s:s + 1]
            if mode == "sel":
                cb = (kv * tk + lax.broadcasted_iota(jnp.int32, (tk, LANES), 0)) // L_SEL
                ll = lax.broadcasted_iota(jnp.int32, (tk, LANES), 1)
                expand = jnp.where(ll == cb + s * (LANES // 4), MASK_BIG, 0.0).astype(BF16)
                st = st + _dot(expand, extra[0][...])
            if masked:
                st = jnp.where(keep, st, NEG)
            m_prev = m_ref[s]
            m_new = jnp.maximum(m_prev, jnp.max(st, axis=0, keepdims=True))
            alpha = jnp.exp(m_prev - m_new)
            p = jnp.exp(st - m_new)
            l_ref[s] = alpha * l_ref[s] + jnp.sum(p, axis=0, keepdims=True)
            vt = vt_ref[...] if vrows == LANES else vt_ref[s * vrows:(s + 1) * vrows, :]
            acc_ref[s] = alpha * acc_ref[s] + _dot(vt, p.astype(BF16))
            m_ref[s] = m_new

    if mode == "win":
        pl.when(kv >= 0)(functools.partial(step, True))
    else:
        pl.when(ki < qi)(functools.partial(step, False))
        pl.when(ki == qi)(functools.partial(step, True))

    @pl.when(ki == nk - 1)
    def _():
        o0 = acc_ref[0] / l_ref[0]
        o1 = acc_ref[1] / l_ref[1]
        if mode == "diff":
            lq1, lk1, lq2, lk2, subg = (r[...] for r in extra)
            lam = (jnp.exp(jnp.sum(lq1 * lk1, axis=1, keepdims=True))
                   - jnp.exp(jnp.sum(lq2 * lk2, axis=1, keepdims=True)) + lam_init)
            o = o0 - lam * o1
            o = o * lax.rsqrt(jnp.mean(o * o, axis=0, keepdims=True) + NORM_EPS) * subg * (1.0 - lam_init)
        else:
            o = jnp.concatenate([o0, o1], axis=0)
        o_ref[...] = o.T


def _flash(mode, qt, k, vt, n_qblocks, q0, k0, v0, shared_kv, extra=(), extra_specs=(), lam_init=0.0):
    b, s, _ = k.shape
    tq = tk = min(TOKEN_TILE, s)
    nq = s // tq
    if mode == "win":
        assert WINDOW == tk
        nk = 2
        kv_of = lambda qi, ki: jnp.maximum(qi - 1 + ki, 0)
    else:
        nk = nq
        kv_of = lambda qi, ki: jnp.minimum(ki, qi)
    hsel = (lambda hb: 0) if shared_kv else (lambda hb: hb)
    in_specs = [
        pl.BlockSpec((None, LANES, tq), lambda i, hb, qi, ki: (i, q0 + hb, qi)),
        pl.BlockSpec((None, tk, LANES), lambda i, hb, qi, ki: (i, kv_of(qi, ki), k0 + hsel(hb))),
        pl.BlockSpec((None, LANES, tk), lambda i, hb, qi, ki: (i, v0 + hsel(hb), kv_of(qi, ki))),
    ]
    for spec in extra_specs:
        in_specs.append(spec(tq, tk, kv_of))
    vrows = LANES if mode == "diff" else HEAD_DIM
    return pl.pallas_call(
        functools.partial(_flash_kernel, mode=mode, tq=tq, tk=tk, lam_init=lam_init),
        grid=(b, n_qblocks, nq, nk), in_specs=in_specs,
        out_specs=pl.BlockSpec((None, tq, LANES), lambda i, hb, qi, ki: (i, qi, hb)),
        out_shape=jax.ShapeDtypeStruct((b, s, n_qblocks * LANES), F32),
        scratch_shapes=[pltpu.VMEM((2, 1, tq), F32), pltpu.VMEM((2, 1, tq), F32),
                        pltpu.VMEM((2, vrows, tq), F32)],
        compiler_params=_cparams(("parallel", "parallel", "parallel", "arbitrary")),
        name="flash_" + mode,
    )(qt, k, vt, *extra)


def _compress_kernel(xk_ref, xv_ref, w1ab_k, w1_k, pe_k, w2_k, w1ab_v, w1_v, pe_v, w2_v, ok_ref, ov_ref):
    def run(x_ref, w1ab, w1, pe, w2, o_ref, transposed):
        cn = x_ref.shape[1]
        pe_term = _dot(pe[...], w1[...], HI)
        out = 0.0
        for g in range(C_GROUPS):
            ab = _dot(x_ref[g], w1ab[...])
            first, second = ab[:, :CMP_HIDDEN], ab[:, CMP_HIDDEN:]
            hid = first + pltpu.roll(second, cn - 1, axis=0) + pe_term
            act = _silu(hid).astype(BF16)
            out = out + (_dot_nt(w2[g], act) if transposed else _dot(act, w2[g]))
        o_ref[...] = out
    run(xk_ref, w1ab_k, w1_k, pe_k, w2_k, ok_ref, False)
    run(xv_ref, w1ab_v, w1_v, pe_v, w2_v, ov_ref, True)


def _compress(kc_chunks, vc_chunks, weights_k, weights_v):
    b, _, cn, width = kc_chunks.shape
    xspec = pl.BlockSpec((None, C_GROUPS, cn, width), lambda i: (i, 0, 0, 0))
    full = lambda a: pl.BlockSpec(a.shape, lambda i: (0,) * a.ndim)
    return pl.pallas_call(
        _compress_kernel, grid=(b,),
        in_specs=[xspec, xspec] + [full(a) for a in weights_k + weights_v],
        out_specs=[pl.BlockSpec((None, cn, LANES), lambda i: (i, 0, 0)),
                   pl.BlockSpec((None, LANES, cn), lambda i: (i, 0, 0))],
        out_shape=[jax.ShapeDtypeStruct((b, cn, LANES), F32), jax.ShapeDtypeStruct((b, LANES, cn), F32)],
        compiler_params=_cparams(("parallel",)), name="nsa_compress",
    )(kc_chunks, vc_chunks, *weights_k, *weights_v)


def _cmp_select_kernel(qt_ref, kc_ref, vct_ref, o_ref, sel_ref, *, tq, n_blk, n_cmp):
    qi = pl.program_id(1)
    cn = kc_ref.shape[0]
    kc = kc_ref[...]
    row = lax.broadcasted_iota(jnp.int32, (LANES, 1), 0)
    slot_rows = ((row % HEAD_DIM) < HALF, (row % HEAD_DIM) >= HALF)
    t_row = qi * tq + lax.broadcasted_iota(jnp.int32, (1, tq), 1)
    m_col = lax.broadcasted_iota(jnp.int32, (cn, 1), 0)
    valid = (m_col * CMP_STRIDE + L_CMP - 1 <= t_row) & (m_col < n_cmp)
    any_valid = (t_row >= L_CMP - 1).astype(F32)
    psum = [jnp.zeros((cn, tq), F32) for _ in range(C_GROUPS)]
    for p_blk in range(C_HPG):
        qt = qt_ref[p_blk * LANES:(p_blk + 1) * LANES, :]
        outs = []
        for s in range(C_GROUPS):
            qs = jnp.where(slot_rows[s], qt, 0.0) * (HEAD_DIM ** -0.5)
            sc = jnp.where(valid, _dot(kc, qs, HI), NEG)
            e = jnp.exp(sc - jnp.max(sc, axis=0, keepdims=True))
            p = e / jnp.sum(e, axis=0, keepdims=True) * any_valid
            psum[s] = psum[s] + p
            outs.append(_dot(vct_ref[s * HEAD_DIM:(s + 1) * HEAD_DIM, :], p, HI))
        o_ref[:, p_blk * LANES:(p_blk + 1) * LANES] = jnp.concatenate(outs, axis=0).T
    jb = lax.broadcasted_iota(jnp.int32, (n_blk, cn), 0)
    mm = lax.broadcasted_iota(jnp.int32, (n_blk, cn), 1)
    overlap = ((mm * CMP_STRIDE < jb * L_SEL + L_SEL) & (mm * CMP_STRIDE + L_CMP > jb * L_SEL)
               & (mm < n_cmp)).astype(F32)
    j = lax.broadcasted_iota(jnp.int32, (n_blk, tq), 0)
    cur = (qi * tq + lax.broadcasted_iota(jnp.int32, (n_blk, tq), 1)) // L_SEL
    forced = (j == 0) | (j == cur) | (j == cur - 1)
    n_top = min(N_SEL, n_blk)
    pad_rows = LANES // 4 - n_blk
    parts = []
    for s in range(C_GROUPS):
        imp = _dot(overlap, psum[s], HI)
        score = jnp.where(j > cur, NEG, imp + jnp.where(forced, FORCE_BONUS, 0.0))
        rank = jnp.zeros((n_blk, tq), jnp.int32)
        for jp in range(n_blk):
            r = score[jp:jp + 1, :]
            rank = rank + ((r > score) | ((r == score) & (jp < j))).astype(jnp.int32)
        parts.append(jnp.where(rank < n_top, 0.0, -1.0))
        if pad_rows:
            parts.append(jnp.zeros((pad_rows, tq), F32))
    parts.append(jnp.zeros((LANES // 2, tq), F32))
    sel_ref[...] = jnp.concatenate(parts, axis=0).astype(sel_ref.dtype)


def _cmp_select(q_raw_t, kcmp, vcmp_t, n_cmp):
    b, width, s = q_raw_t.shape
    cn = kcmp.shape[1]
    tq = min(TOKEN_TILE, s)
    n_blk = s // L_SEL
    return pl.pallas_call(
        functools.partial(_cmp_select_kernel, tq=tq, n_blk=n_blk, n_cmp=n_cmp), grid=(b, s // tq),
        in_specs=[pl.BlockSpec((None, width, tq), lambda i, j: (i, 0, j)),
                  pl.BlockSpec((None, cn, LANES), lambda i, j: (i, 0, 0)),
                  pl.BlockSpec((None, LANES, cn), lambda i, j: (i, 0, 0))],
        out_specs=[pl.BlockSpec((None, tq, width), lambda i, j: (i, j, 0)),
                   pl.BlockSpec((None, LANES, tq), lambda i, j: (i, 0, j))],
        out_shape=[jax.ShapeDtypeStruct((b, s, width), F32), jax.ShapeDtypeStruct((b, LANES, s), BF16)],
        compiler_params=_cparams(("parallel", "parallel")), name="nsa_cmp_select",
    )(q_raw_t, kcmp, vcmp_t)


GDN_HEADS = 2
GDN_BATCH = 4


def _gdn_kernel(q_ref, k_ref, v_ref, wq_ref, wk_ref, wv_ref, gate_ref, alog_ref, dtb_ref, ng_ref, o_ref,
                qs, ks, vs, gs, bs, mc_s, n_s, gt_s):
    s_len = q_ref.shape[0]
    c = DN_CHUNK
    n_chunks = s_len // c
    rows = lax.broadcasted_iota(jnp.int32, (s_len, 1), 0)

    def conv(x, w):
        y = x * w[CONV_WIDTH - 1:CONV_WIDTH, :]
        for back in range(1, CONV_WIDTH):
            xs = jnp.where(rows >= back, pltpu.roll(x, back, axis=0), 0.0)
            y = y + xs * w[CONV_WIDTH - 1 - back:CONV_WIDTH - back, :]
        return _silu(y)

    def l2norm(a):
        return a * lax.rsqrt(jnp.sum(a * a, axis=-1, keepdims=True) + NORM_EPS)

    lanes = [slice(j * LANES, (j + 1) * LANES) for j in range(GDN_HEADS)]
    for j in range(GDN_HEADS):
        qs[j] = l2norm(conv(q_ref[:, lanes[j]], wq_ref[:, lanes[j]])) * (D_DK ** -0.5)
        ks[j] = l2norm(conv(k_ref[:, lanes[j]], wk_ref[:, lanes[j]]))
        vs[j] = conv(v_ref[:, lanes[j]], wv_ref[:, lanes[j]])
        gate = gate_ref[j]
        gs[j] = -jnp.exp(alog_ref[j]) * _softplus(gate[:, 0:1] + dtb_ref[j])
        bs[j] = jax.nn.sigmoid(gate[:, 1:2])

    sc = GDN_BATCH * c
    ii = lax.broadcasted_iota(jnp.int32, (sc, sc), 0)
    jj = lax.broadcasted_iota(jnp.int32, (sc, sc), 1)
    same = (ii // c) == (jj // c)
    causal, strict, upper, eye = same & (ii >= jj), same & (ii > jj), same & (ii <= jj), ii == jj
    chunk_end = same & (jj % c == c - 1)
    ng = ng_ref[...]

    def load(n, j):
        r0 = pl.multiple_of(n * sc, sc)
        return (qs[j, pl.ds(r0, sc), :], ks[j, pl.ds(r0, sc), :], vs[j, pl.ds(r0, sc), :],
                gs[j, pl.ds(r0, sc), :], bs[j, pl.ds(r0, sc), :])

    def prepare(q, k, v, g, beta):
        g_row = jnp.sum(jnp.where(eye, g, 0.0), axis=0, keepdims=True)
        gc_col = jnp.sum(jnp.where(causal, g_row, 0.0), axis=1, keepdims=True)
        gc_row = jnp.sum(jnp.where(upper, g, 0.0), axis=0, keepdims=True)
        g_last = jnp.sum(jnp.where(chunk_end, gc_row, 0.0), axis=1, keepdims=True)
        decay = jnp.where(causal, jnp.exp(jnp.where(causal, gc_col - gc_row, 0.0)), 0.0)
        eg = jnp.exp(gc_col)
        kb = k * beta
        kbf = k.astype(BF16)
        x = -jnp.where(strict, _dot_nt(kb.astype(BF16), kbf) * decay, 0.0)
        inv = jnp.where(eye, 1.0, 0.0) + x
        xb = x.astype(BF16)
        x = _dot(xb, xb)
        for _ in range(int(math.log2(c)) - 2):
            xb = x.astype(BF16)
            both = _dot(jnp.concatenate([xb, inv.astype(BF16)], axis=0), xb)
            x, inv = both[:sc], inv + both[sc:]
        inv = inv + _dot(inv.astype(BF16), x.astype(BF16))
        sol = _dot(inv.astype(BF16), jnp.concatenate([v * beta, kb * eg], axis=1).astype(BF16)).astype(BF16)
        qk = (_dot_nt(q.astype(BF16), kbf) * decay).astype(BF16)
        qo = _dot(qk, sol)
        kd = k * jnp.exp(g_last - gc_col)
        mn = [_dot(kd[t * c:(t + 1) * c].T.astype(BF16), sol[t * c:(t + 1) * c]) for t in range(GDN_BATCH)]
        g_tot = [jnp.exp(g_last[t * c:t * c + 1]) for t in range(GDN_BATCH)]
        return qo, q * eg, mn, g_tot

    def store(n, j, qo, q_dec, mn, g_tot):
        r0 = pl.multiple_of(n * sc, sc)
        o_ref[pl.ds(r0, sc), lanes[j]] = qo[:, :D_DV]
        qs[j, pl.ds(r0, sc), :] = q_dec - qo[:, D_DV:]
        for t in range(GDN_BATCH):
            n_s[j, n * GDN_BATCH + t] = mn[t][:, :D_DV]
            mc_s[j, n * GDN_BATCH + t] = mn[t][:, D_DV:].astype(BF16)
            gt_s[j, n * GDN_BATCH + t] = jnp.broadcast_to(g_tot[t], (8, LANES))

    def prepare_some(i, carry):
        results = [prepare(*operands) for operands in [load(i, j) for j in range(GDN_HEADS)]]
        for j, res in enumerate(results):
            store(i, j, *res)
        return carry

    lax.fori_loop(0, n_chunks // GDN_BATCH, prepare_some, 0)

    def advance(n, states):
        r0 = pl.multiple_of(n * c, c)
        operands = [(o_ref[pl.ds(r0, c), lanes[j]], qs[j, pl.ds(r0, c), :], gt_s[j, n], mc_s[j, n], n_s[j, n])
                    for j in range(GDN_HEADS)]
        out = []
        for j, (o0, q_eff, g_tot, mc, nn) in enumerate(operands):
            sb = states[j].astype(BF16)
            o = o0 + _dot(q_eff.astype(BF16), sb)
            out.append(states[j] * g_tot[0:1, :] - _dot(mc, sb) + nn)
            operands[j] = o * lax.rsqrt(jnp.mean(o * o, axis=-1, keepdims=True) + NORM_EPS) * ng
        for j in range(GDN_HEADS):
            o_ref[pl.ds(r0, c), lanes[j]] = operands[j]
        return tuple(out)

    lax.fori_loop(0, n_chunks, advance, tuple(jnp.zeros((D_DK, D_DV), F32) for _ in range(GDN_HEADS)))


def _gated_deltanet(qkv, gates, conv_w, a_log, dt_bias, norm_g):
    b, s, _ = qkv.shape
    hp, width = GDN_HEADS, GDN_HEADS * LANES
    groups = D_HEADS // hp
    n_chunks = s // DN_CHUNK
    assert n_chunks % GDN_BATCH == 0
    col = lambda off: pl.BlockSpec((None, s, width), lambda i, h: (i, 0, off + h))
    wcol = lambda off: pl.BlockSpec((CONV_WIDTH, width), lambda i, h: (0, off + h))
    scalar = pl.BlockSpec((hp, 1, 1), lambda i, h: (h, 0, 0))
    return pl.pallas_call(
        _gdn_kernel, grid=(b, groups),
        in_specs=[col(0), col(groups), col(2 * groups), wcol(0), wcol(groups), wcol(2 * groups),
                  pl.BlockSpec((None, hp, s, 2), lambda i, h: (i, h, 0, 0)), scalar, scalar,
                  pl.BlockSpec((1, D_DV), lambda i, h: (0, 0))],
        out_specs=pl.BlockSpec((None, s, width), lambda i, h: (i, 0, h)),
        out_shape=jax.ShapeDtypeStruct((b, s, D_HEADS * D_DV), F32),
        scratch_shapes=[pltpu.VMEM((hp, s, LANES), F32)] * 3 + [pltpu.VMEM((hp, s, 1), F32)] * 2
        + [pltpu.VMEM((hp, n_chunks, D_DK, D_DV), BF16), pltpu.VMEM((hp, n_chunks, D_DK, D_DV), F32),
           pltpu.VMEM((hp, n_chunks, 8, LANES), F32)],
        compiler_params=_cparams(("parallel", "parallel")), name="gated_deltanet",
    )(qkv, qkv, qkv, conv_w, conv_w, conv_w, gates,
      a_log.astype(F32).reshape(D_HEADS, 1, 1), dt_bias.astype(F32).reshape(D_HEADS, 1, 1),
      norm_g.astype(F32).reshape(1, D_DV))


def _out_kernel(*refs, odd, final):
    x_ref, mg_ref, gate_ref, w_ref = refs[:4]
    o_ref = refs[-1]
    rest = list(refs[4:-1])
    fin_ref = rest.pop() if final else None
    half = w_ref.shape[0] // 2
    sg = _silu(gate_ref[...])
    if odd:
        cmp_ref, slc_ref, win_ref, od_ref, small_ref = rest
        lane = lax.broadcasted_iota(jnp.int32, (1, LANES), 1)
        bg = jax.nn.sigmoid(small_ref[...])
        blocks = []
        for p_blk in range(C_HPG):
            sl = slice(p_blk * LANES, (p_blk + 1) * LANES)
            acc = 0.0
            for br, ref in enumerate((cmp_ref, slc_ref, win_ref)):
                ca = 2 * D_HEADS + p_blk * N_BRANCH + br
                cb = 2 * D_HEADS + (p_blk + C_HPG) * N_BRANCH + br
                acc = acc + jnp.where(lane < HEAD_DIM, bg[:, ca:ca + 1], bg[:, cb:cb + 1]) * ref[:, sl]
            blocks.append(acc)
        first = jnp.concatenate(blocks, axis=1)
        second = od_ref[...]
    else:
        first, second = rest[0][...], rest[1][...]
    y = (_dot((first * sg[:, :half]).astype(BF16), w_ref[:half, :])
         + _dot((second * sg[:, half:]).astype(BF16), w_ref[half:, :]))
    out = x_ref[...] + mg_ref[...] * y
    if final:
        out = out * lax.rsqrt(jnp.mean(out * out, axis=-1, keepdims=True) + NORM_EPS) * fin_ref[...]
    o_ref[...] = out


def _out_projection(x, mod_l, gate, w, branches, final_g=None):
    b, s, d = x.shape
    ts = min(TOKEN_TILE, s)
    odd = len(branches) > 2
    row = lambda width: pl.BlockSpec((None, ts, width), lambda i, j: (i, j, 0))
    in_specs = [row(d), pl.BlockSpec((None, None, 1, d), lambda i, j: (i, 2, 0, 0)), row(gate.shape[-1]),
                pl.BlockSpec(w.shape, lambda i, j: (0, 0), pipeline_mode=pl.Buffered(1))]
    in_specs += [row(a.shape[-1]) for a in branches]
    args = [x, mod_l, gate, w, *branches]
    if final_g is not None:
        in_specs.append(pl.BlockSpec((1, d), lambda i, j: (0, 0)))
        args.append(final_g.reshape(1, d))
    return pl.pallas_call(
        functools.partial(_out_kernel, odd=odd, final=final_g is not None), grid=(b, s // ts),
        in_specs=in_specs, out_specs=row(d), out_shape=jax.ShapeDtypeStruct((b, s, d), F32),
        compiler_params=_cparams(("parallel", "parallel")), name="gated_out_proj",
    )(*args)


def _pair_cols(a0, b0):
    a, bb = np.arange(a0, a0 + HEAD_DIM), np.arange(b0, b0 + HEAD_DIM)
    return np.concatenate([a[:HALF], bb[:HALF], a[HALF:], bb[HALF:]])


def _paired_head_order(width):
    pairs = [np.concatenate([np.arange(p * HEAD_DIM, (p + 1) * HEAD_DIM),
                             np.arange((p + C_HPG) * HEAD_DIM, (p + C_HPG + 1) * HEAD_DIM)]) for p in range(C_HPG)]
    return np.concatenate(pairs + [np.arange(C_HEADS * HEAD_DIM, width)])


def _even_layout():
    aq, ak, av = 0, 512, 1024
    bq, bk, bv, bf, gate = 1536, 2048, 2560, 3072, 3080
    zero = gate + 1024
    cols = [_pair_cols(ak + 2 * h * HEAD_DIM, ak + (2 * h + 1) * HEAD_DIM) for h in range(A_HEADS)]
    cols.append(np.arange(bk, bk + 512))
    cols += [np.concatenate([np.arange(bf + 2 * p, bf + 2 * p + 2), np.full(LANES - 2, zero)]) for p in range(B_HEADS // 2)]
    cols.append(np.arange(gate, gate + 1024))
    segs = (("rope", 0, 512), ("bf16", 512, 1024), ("f32", 1024, 1536), ("f32", 1536, 2560))
    rows = [_pair_cols(aq + 2 * h * HEAD_DIM, aq + (2 * h + 1) * HEAD_DIM) for h in range(A_HEADS)]
    rows += [np.arange(bq, bq + 512), np.arange(av, av + 512), np.arange(bv, bv + 512)]
    tsegs = (("rope", 0, 512), ("bf16", 512, 2048))
    return np.concatenate(cols), segs, np.concatenate(rows), tsegs


def _odd_layout():
    cq, kc, vc, ks, vs, kw, vw, cg = 0, 512, 640, 768, 896, 1024, 1152, 1280
    dq, da, db, gate = 1304, 2840, 2844, 2848
    zero = gate + 1024
    cols = [_pair_cols(ks, ks + HEAD_DIM), _pair_cols(kw, kw + HEAD_DIM), np.arange(kc, kc + 256)]
    small = np.concatenate([np.arange(da, da + 2 * D_HEADS), np.arange(cg, cg + C_HEADS * N_BRANCH)])
    cols.append(np.concatenate([small, np.full(LANES - small.size, zero)]))
    cols += [np.arange(dq, dq + 1536), gate + _paired_head_order(1024)]
    segs = (("rope", 0, 256), ("bf16", 256, 512), ("f32", 512, 640), ("f32", 640, 2176), ("f32", 2176, 3200))
    rows = [_pair_cols(cq + p * HEAD_DIM, cq + (p + C_HPG) * HEAD_DIM) for p in range(C_HPG)]
    rows += [np.arange(vs, vs + LANES), np.arange(vw, vw + LANES)]
    tsegs = (("rope+raw", 0, 512), ("bf16", 512, 768))
    return np.concatenate(cols), segs, np.concatenate(rows), tsegs


def _layout_weights(w, cols, rows):
    w = jnp.concatenate([w, jnp.zeros((w.shape[0], 1), w.dtype)], axis=1)
    return w[:, cols].astype(BF16), w[:, rows].T.astype(BF16)


def _compress_weights(pe, w1, w2, for_keys):
    half = L_CMP // 2 * HEAD_DIM
    w1ab = jnp.concatenate([w1[:half], w1[half:]], axis=1).astype(BF16)
    w2p = jnp.zeros((C_GROUPS, CMP_HIDDEN, LANES), F32)
    for g in range(C_GROUPS):
        if for_keys:
            w2p = w2p.at[g, :, g * HALF:(g + 1) * HALF].set(w2[:, :HALF])
            w2p = w2p.at[g, :, HEAD_DIM + g * HALF:HEAD_DIM + (g + 1) * HALF].set(w2[:, HALF:])
        else:
            w2p = w2p.at[g, :, g * HEAD_DIM:(g + 1) * HEAD_DIM].set(w2)
    if not for_keys:
        w2p = w2p.transpose(0, 2, 1)
    return [w1ab, w1.astype(F32), pe.astype(F32).reshape(1, L_CMP * HEAD_DIM), w2p.astype(BF16)]


def _chunk_layout(a):
    b, s, _ = a.shape
    a = a.reshape(b, s // CMP_STRIDE, CMP_STRIDE, C_GROUPS, HEAD_DIM)
    return a.transpose(0, 3, 1, 2, 4).reshape(b, C_GROUPS, s // CMP_STRIDE, CMP_STRIDE * HEAD_DIM)


def _even_layer(x, mod_l, rope, layer_idx, g, w_in, b_forget, lq1, lk1, lq2, lk2, subln_g, w_out, final_g):
    cols, segs, rows, tsegs = _even_layout()
    w, wt = _layout_weights(w_in, cols, rows)
    k_a, k_b, forget_logits, gate, qt_a, rest_t = _projection(x, mod_l, g, rope, w, wt, segs, tsegs)
    lam_init = 0.8 - 0.6 * math.exp(-0.3 * layer_idx)
    vec = lambda a: a.astype(F32).reshape(1, -1)
    const = lambda shape: (lambda tq, tk, kv_of: pl.BlockSpec(shape, lambda i, hb, qi, ki: (0, 0)))
    nb = B_HEADS // 2
    oa = _flash("diff", qt_a, k_a, rest_t, A_HEADS, 0, 0, nb, False,
                extra=[vec(lq1), vec(lk1), vec(lq2), vec(lk2), subln_g.astype(F32).reshape(LANES, 1)],
                extra_specs=[const((1, HEAD_DIM))] * 4 + [const((LANES, 1))], lam_init=lam_init)
    cum = _forget_cumsum(forget_logits, b_forget)
    cum_spec = lambda tq, tk, kv_of: pl.BlockSpec((None, None, tk, LANES),
                                                  lambda i, hb, qi, ki: (i, hb, kv_of(qi, ki), 0))
    ob = _flash("fox", rest_t, k_b, rest_t, nb, 0, 0, nb + A_HEADS, False, extra=[cum], extra_specs=[cum_spec])
    return _out_projection(x, mod_l, gate, w_out.astype(BF16), [oa, ob], final_g)


def _odd_layer(x, mod_l, rope, g, w_in, pe_k, pe_v, w1_k, w2_k, w1_v, w2_v, conv_w, a_log, dt_bias,
               dn_norm_g, w_out, final_g):
    b, s, _ = x.shape
    cols, segs, rows, tsegs = _odd_layout()
    w, wt = _layout_weights(w_in, cols, rows)
    k_rot, cmp_in, small, dqkv, gate, q_raw_t, q_rot_t, v_t = _projection(x, mod_l, g, rope, w, wt, segs, tsegs)
    n_cmp = (s - L_CMP) // CMP_STRIDE + 1
    kcmp, vcmp_t = _compress(_chunk_layout(cmp_in[:, :, :LANES]), _chunk_layout(cmp_in[:, :, LANES:]),
                             _compress_weights(pe_k, w1_k, w2_k, True), _compress_weights(pe_v, w1_v, w2_v, False))
    o_cmp, sel = _cmp_select(q_raw_t, kcmp, vcmp_t, n_cmp)
    sel_spec = lambda tq, tk, kv_of: pl.BlockSpec((None, LANES, tq), lambda i, hb, qi, ki: (i, 0, qi))
    o_slc = _flash("sel", q_rot_t, k_rot, v_t, C_HPG, 0, 0, 0, True, extra=[sel], extra_specs=[sel_spec])
    o_win = _flash("win", q_rot_t, k_rot, v_t, C_HPG, 0, 1, 1, True)
    gates = small[:, :, :2 * D_HEADS].reshape(b, s, 2, D_HEADS).transpose(0, 3, 1, 2)
    od = _gated_deltanet(dqkv, gates, conv_w.astype(F32), a_log, dt_bias, dn_norm_g)
    rows = _paired_head_order(w_out.shape[0])
    return _out_projection(x, mod_l, gate, w_out[rows].astype(BF16), [o_cmp, o_slc, o_win, od, small], final_g)


def kernel(x, c, positions, norm_g, w_mod, b_mod, w_out, final_norm_g, w_in_even, b_forget, lambda_q1, lambda_k1,
           lambda_q2, lambda_k2, subln_g, w_in_odd, cmp_pe_k, cmp_pe_v, cmp_w1_k, cmp_w2_k, cmp_w1_v, cmp_w2_v,
           conv_w, a_log, dt_bias, dn_norm_g):
    depth = norm_g.shape[0]
    rope = _rope_tables(positions)
    mod = _modulation(c, w_mod, b_mod)
    for l in range(depth):
        final_g = final_norm_g if l == depth - 1 else None
        i = l // 2
        if l % 2 == 0:
            x = _even_layer(x, mod[l], rope, l, norm_g[l], w_in_even[i], b_forget[i], lambda_q1[i],
                            lambda_k1[i], lambda_q2[i], lambda_k2[i], subln_g[i], w_out[l], final_g)
        else:
            x = _odd_layer(x, mod[l], rope, norm_g[l], w_in_odd[i], cmp_pe_k[i], cmp_pe_v[i], cmp_w1_k[i],
                           cmp_w2_k[i], cmp_w1_v[i], cmp_w2_v[i], conv_w[i], a_log[i], dt_bias[i], dn_norm_g[i],
                           w_out[l], final_g)
    return x
```

```python
import functools
import math

import jax
import jax.numpy as jnp
import numpy as np
from jax import lax
from jax.experimental import pallas as pl
from jax.experimental.pallas import tpu as pltpu

F32 = jnp.float32
BF16 = jnp.bfloat16
HI = lax.Precision.HIGHEST

LANES = 128
HEAD_DIM = 64
HALF = HEAD_DIM // 2
ROPE_THETA = 10000.0
NORM_EPS = 1e-6
NEG = -1e30
MASK_BIG = 1e30
LOG2E = math.log2(math.e)
QK_SCALE = HEAD_DIM ** -0.5 * LOG2E
BIAS_PIECES = 3
A_HEADS = 4
B_HEADS = 8
C_HEADS = 8
C_GROUPS = 2
C_HPG = C_HEADS // C_GROUPS
L_CMP = 32
CMP_STRIDE = 16
CMP_HIDDEN = 256
L_SEL = 64
N_SEL = 8
WINDOW = 512
N_BRANCH = 3
FORCE_BONUS = 1e4
D_HEADS = 4
D_DK = 128
D_DV = 128
CONV_WIDTH = 4
DN_CHUNK = 64
TOKEN_TILE = 512
VMEM_LIMIT = 56 * 1024 * 1024


def _cparams(sem):
    return pltpu.CompilerParams(dimension_semantics=sem, vmem_limit_bytes=VMEM_LIMIT)


def _dot(a, b, precision=None):
    return jnp.dot(a, b, precision=precision, preferred_element_type=F32)


def _dot_nt(a, b, precision=None):
    return lax.dot_general(a, b, (((1,), (1,)), ((), ())), precision=precision, preferred_element_type=F32)


def _dot_tn(a, b, precision=None):
    return lax.dot_general(a, b, (((0,), (0,)), ((), ())), precision=precision, preferred_element_type=F32)


def _softplus(z):
    return jnp.maximum(z, 0.0) + jnp.log1p(jnp.exp(-jnp.abs(z)))


def _silu(z):
    return z * jax.nn.sigmoid(z)


def _rope_table_kernel(pos_col_ref, pos_row_ref, inv_row_ref, inv_col_ref, cos_ref, sin_ref, cos_t_ref, sin_t_ref):
    ang = pos_col_ref[...].astype(F32) * inv_row_ref[...]
    lane = lax.broadcasted_iota(jnp.int32, (1, LANES), 1)
    cos_ref[...] = jnp.cos(ang)
    sin_ref[...] = jnp.where(lane < 2 * HALF, -1.0, 1.0) * jnp.sin(ang)
    ang_t = inv_col_ref[...] * pos_row_ref[...].astype(F32)
    row = lax.broadcasted_iota(jnp.int32, (LANES, 1), 0)
    cos_t_ref[...] = jnp.cos(ang_t)
    sin_t_ref[...] = jnp.where(row < 2 * HALF, -1.0, 1.0) * jnp.sin(ang_t)


def _rope_tables(positions):
    b, s = positions.shape
    inv = ROPE_THETA ** (-jnp.arange(0, HEAD_DIM, 2, dtype=F32) / HEAD_DIM)
    inv = jnp.tile(inv, 4)
    tok = jax.ShapeDtypeStruct((b, s, LANES), F32)
    feat = jax.ShapeDtypeStruct((b, LANES, s), F32)
    return pl.pallas_call(
        _rope_table_kernel, grid=(b,),
        in_specs=[pl.BlockSpec((None, s, 1), lambda i: (i, 0, 0)),
                  pl.BlockSpec((None, 1, s), lambda i: (i, 0, 0)),
                  pl.BlockSpec((1, LANES), lambda i: (0, 0)),
                  pl.BlockSpec((LANES, 1), lambda i: (0, 0))],
        out_specs=[pl.BlockSpec((None, s, LANES), lambda i: (i, 0, 0))] * 2
        + [pl.BlockSpec((None, LANES, s), lambda i: (i, 0, 0))] * 2,
        out_shape=[tok, tok, feat, feat], compiler_params=_cparams(("parallel",)), name="rope_tables",
    )(positions.reshape(b, s, 1), positions.reshape(b, 1, s), inv.reshape(1, LANES), inv.reshape(LANES, 1))


def _mod_kernel(c_ref, w_ref, b_ref, o_ref):
    o_ref[...] = _dot(_silu(c_ref[...]), w_ref[...], HI) + b_ref[...]


def _modulation(c, w_mod, b_mod):
    depth, d, n = w_mod.shape
    b = c.shape[0]
    tn = 1024
    mod = pl.pallas_call(
        _mod_kernel, grid=(depth, n // tn),
        in_specs=[pl.BlockSpec((b, d), lambda l, j: (0, 0)),
                  pl.BlockSpec((None, d, tn), lambda l, j: (l, 0, j)),
                  pl.BlockSpec((None, 1, tn), lambda l, j: (l, 0, j))],
        out_specs=pl.BlockSpec((None, b, tn), lambda l, j: (l, 0, j)),
        out_shape=jax.ShapeDtypeStruct((depth, b, n), F32),
        compiler_params=_cparams(("parallel", "parallel")), name="modulation",
    )(c, w_mod, b_mod.reshape(depth, 1, n))
    return mod.reshape(depth, b, 3, 1, d)


def _proj_kernel(x_ref, shift_ref, scale_ref, g_ref, cos_ref, sin_ref, cos_t_ref, sin_t_ref, w_ref, wt_ref,
                 *out_refs, segs, tsegs):
    x = x_ref[...]
    h = x * lax.rsqrt(jnp.mean(x * x, axis=-1, keepdims=True) + NORM_EPS) * g_ref[...]
    h = (h * (1.0 + scale_ref[...]) + shift_ref[...]).astype(BF16)
    outs = list(out_refs)
    step = 4 * LANES
    for kind, c0, c1 in segs:
        o_ref = outs.pop(0)
        for a in range(c0, c1, step):
            e = min(a + step, c1)
            acc = _dot(h, w_ref[:, a:e])
            if kind == "rope":
                cos, sin = cos_ref[...], sin_ref[...]
                for j in range(0, e - a, LANES):
                    blk = acc[:, j:j + LANES]
                    rot = blk * cos + pltpu.roll(blk, 2 * HALF, axis=1) * sin
                    o_ref[:, a - c0 + j:a - c0 + j + LANES] = rot.astype(o_ref.dtype)
            else:
                o_ref[:, a - c0:e - c0] = acc.astype(o_ref.dtype)
    for kind, r0, r1 in tsegs:
        raw_ref = outs.pop(0) if kind == "rope+raw" else None
        o_ref = outs.pop(0)
        for a in range(r0, r1, step):
            e = min(a + step, r1)
            acc = _dot_nt(wt_ref[a:e, :], h)
            if raw_ref is not None:
                raw_ref[a - r0:e - r0, :] = acc
            if kind in ("rope", "rope+raw"):
                cos, sin = cos_t_ref[...], sin_t_ref[...]
                for j in range(0, e - a, LANES):
                    blk = acc[j:j + LANES, :]
                    rot = blk * cos + pltpu.roll(blk, 2 * HALF, axis=0) * sin
                    o_ref[a - r0 + j:a - r0 + j + LANES, :] = rot.astype(o_ref.dtype)
            else:
                o_ref[a - r0:e - r0, :] = acc.astype(o_ref.dtype)


def _projection(x, mod_l, g, rope, w, wt, segs, tsegs):
    b, s, d = x.shape
    ts = min(TOKEN_TILE, s)
    row = lambda width: pl.BlockSpec((None, ts, width), lambda i, j: (i, j, 0))
    col = lambda height: pl.BlockSpec((None, height, ts), lambda i, j: (i, 0, j))
    out_specs, out_shapes = [], []
    for kind, c0, c1 in segs:
        out_specs.append(row(c1 - c0))
        out_shapes.append(jax.ShapeDtypeStruct((b, s, c1 - c0), F32 if kind == "f32" else BF16))
    for kind, r0, r1 in tsegs:
        if kind == "rope+raw":
            out_specs.append(col(r1 - r0))
            out_shapes.append(jax.ShapeDtypeStruct((b, r1 - r0, s), F32))
        out_specs.append(col(r1 - r0))
        out_shapes.append(jax.ShapeDtypeStruct((b, r1 - r0, s), BF16))
    modspec = lambda k: pl.BlockSpec((None, None, 1, d), lambda i, j: (i, k, 0, 0))
    resident = lambda a: pl.BlockSpec(a.shape, lambda i, j: (0, 0), pipeline_mode=pl.Buffered(1))
    return pl.pallas_call(
        functools.partial(_proj_kernel, segs=segs, tsegs=tsegs), grid=(b, s // ts),
        in_specs=[row(d), modspec(0), modspec(1), pl.BlockSpec((1, d), lambda i, j: (0, 0)),
                  row(LANES), row(LANES), col(LANES), col(LANES), resident(w), resident(wt)],
        out_specs=out_specs, out_shape=out_shapes,
        compiler_params=_cparams(("parallel", "parallel")), name="adaln_in_proj",
    )(x, mod_l, mod_l, g.reshape(1, d), *rope, w, wt)


def _cum_kernel(x_ref, bias_ref, o_ref):
    s = x_ref.shape[0]
    ii = lax.broadcasted_iota(jnp.int32, (LANES, LANES), 0)
    jj = lax.broadcasted_iota(jnp.int32, (LANES, LANES), 1)
    lower = (ii >= jj).astype(F32)
    lane = lax.broadcasted_iota(jnp.int32, (1, LANES), 1)
    carry = jnp.zeros((1, LANES), F32)
    for r0 in range(0, s, LANES):
        z = x_ref[r0:r0 + LANES, :] + bias_ref[...]
        logf = jnp.minimum(z, 0.0) - jnp.log1p(jnp.exp(-jnp.abs(z)))
        loc = _dot(lower, logf, HI) + carry
        carry = loc[LANES - 1:LANES, :]
        val = loc * (-LOG2E)
        hi = val.astype(BF16)
        rest = val - hi.astype(F32)
        mid = rest.astype(BF16)
        lo = (rest - mid.astype(F32)).astype(BF16)
        piece = jnp.where(lane % BIAS_PIECES == 0, hi, jnp.where(lane % BIAS_PIECES == 1, mid, lo))
        o_ref[r0:r0 + LANES, :] = jnp.where(lane < 2 * BIAS_PIECES, piece, jnp.zeros_like(piece))


def _forget_cumsum(logits, b_forget):
    b, s, width = logits.shape
    pairs = width // LANES
    per_lane = jnp.repeat(b_forget.astype(F32).reshape(pairs, 2), BIAS_PIECES, axis=1)
    bias = jnp.zeros((pairs, 1, LANES), F32).at[:, 0, :2 * BIAS_PIECES].set(per_lane)
    return pl.pallas_call(
        _cum_kernel, grid=(b, pairs),
        in_specs=[pl.BlockSpec((None, s, LANES), lambda i, p: (i, 0, p)),
                  pl.BlockSpec((None, 1, LANES), lambda i, p: (p, 0, 0))],
        out_specs=pl.BlockSpec((None, None, s, LANES), lambda i, p: (i, p, 0, 0)),
        out_shape=jax.ShapeDtypeStruct((b, pairs, s, LANES), BF16),
        compiler_params=_cparams(("parallel", "parallel")), name="forget_cumsum",
    )(logits, bias)


def _flash_kernel(*refs, mode, tq, tk, nblk, shared_kv, lam_init):
    qt_ref, k_ref, vt_ref = refs[:3]
    m_ref, l_ref, acc_ref = refs[-3:]
    o_ref = refs[-4]
    extra = refs[3:-4]
    qi, ki = pl.program_id(2), pl.program_id(3)
    nk = pl.num_programs(3)
    kv = qi - 1 + ki if mode == "win" else ki
    row = lax.broadcasted_iota(jnp.int32, (LANES, 1), 0)
    if mode == "fox":
        slot_rows = (row < HEAD_DIM, row >= HEAD_DIM)
    else:
        slot_rows = ((row % HEAD_DIM) < HALF, (row % HEAD_DIM) >= HALF)
    vrows = acc_ref.shape[1]
    block = lambda j: slice(j * LANES, (j + 1) * LANES)

    @pl.when(ki == 0)
    def _():
        m_ref[...] = jnp.full(m_ref.shape, NEG, F32)
        l_ref[...] = jnp.zeros(l_ref.shape, F32)
        acc_ref[...] = jnp.zeros(acc_ref.shape, F32)

    def step(masked):
        rr = lax.broadcasted_iota(jnp.int32, (LANES, tq), 0)
        q_extra, k_extra = [None, None], [None, None]
        for s in range(2):
            if mode == "fox":
                q_extra[s] = jnp.where((rr >= BIAS_PIECES * s) & (rr < BIAS_PIECES * (s + 1)), 1.0, 0.0).astype(BF16)
            if mode == "sel":
                cb = (kv * tk + lax.broadcasted_iota(jnp.int32, (tk, LANES), 0)) // L_SEL
                ll = lax.broadcasted_iota(jnp.int32, (tk, LANES), 1)
                k_extra[s] = jnp.where(ll == cb + s * (LANES // 4), MASK_BIG, 0.0).astype(BF16)
                q_extra[s] = extra[0][...]
        sts = []
        for j in range(nblk):
            qt = qt_ref[block(j), :]
            k = k_ref[...] if shared_kv else k_ref[:, block(j)]
            for s in range(2):
                qs = jnp.where(slot_rows[s], qt, jnp.zeros_like(qt))
                if q_extra[s] is None:
                    sts.append(_dot(k, qs))
                else:
                    k_more = extra[0][j] if mode == "fox" else k_extra[s]
                    sts.append(_dot(jnp.concatenate([k, k_more], axis=1), jnp.concatenate([qs, q_extra[s]], axis=0)))
        if masked:
            kpos = kv * tk + lax.broadcasted_iota(jnp.int32, (tk, tq), 0)
            qpos = qi * tq + lax.broadcasted_iota(jnp.int32, (tk, tq), 1)
            keep = kpos <= qpos
            if mode == "win":
                keep = keep & (qpos - kpos < WINDOW)
        for idx, st in enumerate(sts):
            j, s = divmod(idx, 2)
            if masked:
                st = jnp.where(keep, st, NEG)
            m_prev = m_ref[idx]
            m_new = jnp.maximum(m_prev, jnp.max(st, axis=0, keepdims=True))
            alpha = jnp.exp2(m_prev - m_new)
            p = jnp.exp2(st - m_new)
            l_ref[idx] = alpha * l_ref[idx] + jnp.sum(p, axis=0, keepdims=True)
            base = 0 if shared_kv else j * LANES
            if vrows == LANES:
                vt = vt_ref[base:base + LANES, :]
            else:
                vt = vt_ref[base + s * vrows:base + (s + 1) * vrows, :]
            acc_ref[idx] = alpha * acc_ref[idx] + _dot(vt, p.astype(BF16))
            m_ref[idx] = m_new

    if mode == "win":
        pl.when(kv >= 0)(functools.partial(step, True))
    else:
        pl.when(ki < qi)(functools.partial(step, False))
        pl.when(ki == qi)(functools.partial(step, True))

    @pl.when(ki == nk - 1)
    def _():
        for j in range(nblk):
            o0 = acc_ref[2 * j] / l_ref[2 * j]
            o1 = acc_ref[2 * j + 1] / l_ref[2 * j + 1]
            if mode == "diff":
                lq1, lk1, lq2, lk2, subg = (r[...] for r in extra)
                lam = (jnp.exp(jnp.sum(lq1 * lk1, axis=1, keepdims=True))
                       - jnp.exp(jnp.sum(lq2 * lk2, axis=1, keepdims=True)) + lam_init)
                o = o0 - lam * o1
                o = o * lax.rsqrt(jnp.mean(o * o, axis=0, keepdims=True) + NORM_EPS) * subg * (1.0 - lam_init)
            else:
                o = jnp.concatenate([o0, o1], axis=0)
            o_ref[:, block(j)] = o.T


def _flash(mode, qt, k, vt, n_qblocks, nblk, q0, k0, v0, shared_kv, extra=(), extra_specs=(), lam_init=0.0):
    b, s, _ = k.shape
    tq = tk = min(TOKEN_TILE, s)
    nq = s // tq
    if mode == "win":
        assert WINDOW == tk
        nk = 2
        kv_of = lambda qi, ki: jnp.maximum(qi - 1 + ki, 0)
    else:
        nk = nq
        kv_of = lambda qi, ki: jnp.minimum(ki, qi)
    assert n_qblocks % nblk == 0 and q0 % nblk == 0 and (shared_kv or (k0 % nblk == 0 and v0 % nblk == 0))
    wide = nblk * LANES
    if shared_kv:
        k_spec = pl.BlockSpec((None, tk, LANES), lambda i, hb, qi, ki: (i, kv_of(qi, ki), k0))
        v_spec = pl.BlockSpec((None, LANES, tk), lambda i, hb, qi, ki: (i, v0, kv_of(qi, ki)))
    else:
        k_spec = pl.BlockSpec((None, tk, wide), lambda i, hb, qi, ki: (i, kv_of(qi, ki), k0 // nblk + hb))
        v_spec = pl.BlockSpec((None, wide, tk), lambda i, hb, qi, ki: (i, v0 // nblk + hb, kv_of(qi, ki)))
    in_specs = [pl.BlockSpec((None, wide, tq), lambda i, hb, qi, ki: (i, q0 // nblk + hb, qi)), k_spec, v_spec]
    for spec in extra_specs:
        in_specs.append(spec(tq, tk, kv_of))
    vrows = LANES if mode == "diff" else HEAD_DIM
    return pl.pallas_call(
        functools.partial(_flash_kernel, mode=mode, tq=tq, tk=tk, nblk=nblk, shared_kv=shared_kv, lam_init=lam_init),
        grid=(b, n_qblocks // nblk, nq, nk), in_specs=in_specs,
        out_specs=pl.BlockSpec((None, tq, wide), lambda i, hb, qi, ki: (i, qi, hb)),
        out_shape=jax.ShapeDtypeStruct((b, s, n_qblocks * LANES), F32),
        scratch_shapes=[pltpu.VMEM((2 * nblk, 1, tq), F32), pltpu.VMEM((2 * nblk, 1, tq), F32),
                        pltpu.VMEM((2 * nblk, vrows, tq), F32)],
        compiler_params=_cparams(("parallel", "parallel", "parallel", "arbitrary")),
        name="flash_" + mode,
    )(qt, k, vt, *extra)


def _compress_kernel(xk_ref, xv_ref, w1ab_k, w1_k, pe_k, w2_k, w1ab_v, w1_v, pe_v, w2_v, ok_ref, ov_ref):
    def run(x_ref, w1ab, w1, pe, w2, o_ref, transposed):
        cn = x_ref.shape[1]
        pe_term = _dot(pe[...], w1[...], HI)
        out = 0.0
        for g in range(C_GROUPS):
            ab = _dot(x_ref[g], w1ab[...])
            first, second = ab[:, :CMP_HIDDEN], ab[:, CMP_HIDDEN:]
            hid = first + pltpu.roll(second, cn - 1, axis=0) + pe_term
            act = _silu(hid).astype(BF16)
            out = out + (_dot_nt(w2[g], act) if transposed else _dot(act, w2[g]))
        o_ref[...] = out
    run(xk_ref, w1ab_k, w1_k, pe_k, w2_k, ok_ref, False)
    run(xv_ref, w1ab_v, w1_v, pe_v, w2_v, ov_ref, True)


def _compress(kc_chunks, vc_chunks, weights_k, weights_v):
    b, _, cn, width = kc_chunks.shape
    xspec = pl.BlockSpec((None, C_GROUPS, cn, width), lambda i: (i, 0, 0, 0))
    full = lambda a: pl.BlockSpec(a.shape, lambda i: (0,) * a.ndim)
    return pl.pallas_call(
        _compress_kernel, grid=(b,),
        in_specs=[xspec, xspec] + [full(a) for a in weights_k + weights_v],
        out_specs=[pl.BlockSpec((None, cn, LANES), lambda i: (i, 0, 0)),
                   pl.BlockSpec((None, LANES, cn), lambda i: (i, 0, 0))],
        out_shape=[jax.ShapeDtypeStruct((b, cn, LANES), F32), jax.ShapeDtypeStruct((b, LANES, cn), F32)],
        compiler_params=_cparams(("parallel",)), name="nsa_compress",
    )(kc_chunks, vc_chunks, *weights_k, *weights_v)


def _cmp_select_kernel(qt_ref, kc_ref, vct_ref, o_ref, sel_ref, *, tq, n_blk, n_cmp):
    qi = pl.program_id(1)
    cn = kc_ref.shape[0]
    kc = kc_ref[...]
    row = lax.broadcasted_iota(jnp.int32, (LANES, 1), 0)
    slot_rows = ((row % HEAD_DIM) < HALF, (row % HEAD_DIM) >= HALF)
    t_row = qi * tq + lax.broadcasted_iota(jnp.int32, (1, tq), 1)
    m_col = lax.broadcasted_iota(jnp.int32, (cn, 1), 0)
    valid = (m_col * CMP_STRIDE + L_CMP - 1 <= t_row) & (m_col < n_cmp)
    any_valid = (t_row >= L_CMP - 1).astype(F32)
    psum = [jnp.zeros((cn, tq), F32) for _ in range(C_GROUPS)]
    for p_blk in range(C_HPG):
        qt = qt_ref[p_blk * LANES:(p_blk + 1) * LANES, :]
        outs = []
        for s in range(C_GROUPS):
            qs = jnp.where(slot_rows[s], qt, 0.0)
            sc = jnp.where(valid, _dot(kc, qs, HI), NEG)
            e = jnp.exp2(sc - jnp.max(sc, axis=0, keepdims=True))
            p = e / jnp.sum(e, axis=0, keepdims=True) * any_valid
            psum[s] = psum[s] + p
            outs.append(_dot(vct_ref[s * HEAD_DIM:(s + 1) * HEAD_DIM, :], p, HI))
        o_ref[:, p_blk * LANES:(p_blk + 1) * LANES] = jnp.concatenate(outs, axis=0).T
    jb = lax.broadcasted_iota(jnp.int32, (n_blk, cn), 0)
    mm = lax.broadcasted_iota(jnp.int32, (n_blk, cn), 1)
    overlap = ((mm * CMP_STRIDE < jb * L_SEL + L_SEL) & (mm * CMP_STRIDE + L_CMP > jb * L_SEL)
               & (mm < n_cmp)).astype(F32)
    j = lax.broadcasted_iota(jnp.int32, (n_blk, tq), 0)
    cur = (qi * tq + lax.broadcasted_iota(jnp.int32, (n_blk, tq), 1)) // L_SEL
    forced = (j == 0) | (j == cur) | (j == cur - 1)
    n_top = min(N_SEL, n_blk)
    pad_rows = LANES // 4 - n_blk
    parts = []
    for s in range(C_GROUPS):
        imp = _dot(overlap, psum[s], HI)
        score = jnp.where(j > cur, NEG, imp + jnp.where(forced, FORCE_BONUS, 0.0))
        rank = jnp.zeros((n_blk, tq), jnp.int32)
        for jp in range(n_blk):
            r = score[jp:jp + 1, :]
            rank = rank + ((r > score) | ((r == score) & (jp < j))).astype(jnp.int32)
        parts.append(jnp.where(rank < n_top, 0.0, -1.0))
        if pad_rows:
            parts.append(jnp.zeros((pad_rows, tq), F32))
    parts.append(jnp.zeros((LANES // 2, tq), F32))
    sel_ref[...] = jnp.concatenate(parts, axis=0).astype(sel_ref.dtype)


def _cmp_select(q_raw_t, kcmp, vcmp_t, n_cmp):
    b, width, s = q_raw_t.shape
    cn = kcmp.shape[1]
    tq = min(TOKEN_TILE, s)
    n_blk = s // L_SEL
    return pl.pallas_call(
        functools.partial(_cmp_select_kernel, tq=tq, n_blk=n_blk, n_cmp=n_cmp), grid=(b, s // tq),
        in_specs=[pl.BlockSpec((None, width, tq), lambda i, j: (i, 0, j)),
                  pl.BlockSpec((None, cn, LANES), lambda i, j: (i, 0, 0)),
                  pl.BlockSpec((None, LANES, cn), lambda i, j: (i, 0, 0))],
        out_specs=[pl.BlockSpec((None, tq, width), lambda i, j: (i, j, 0)),
                   pl.BlockSpec((None, LANES, tq), lambda i, j: (i, 0, j))],
        out_shape=[jax.ShapeDtypeStruct((b, s, width), F32), jax.ShapeDtypeStruct((b, LANES, s), BF16)],
        compiler_params=_cparams(("parallel", "parallel")), name="nsa_cmp_select",
    )(q_raw_t, kcmp, vcmp_t)


GDN_HEADS = 2
GDN_BATCH = 4


def _gdn_kernel(q_ref, k_ref, v_ref, wq_ref, wk_ref, wv_ref, gate_ref, alog_ref, dtb_ref, ng_ref, o_ref,
                qs, ks, vs, gs, bs, mc_s, n_s, gt_s):
    s_len = q_ref.shape[0]
    c = DN_CHUNK
    n_chunks = s_len // c
    rows = lax.broadcasted_iota(jnp.int32, (s_len, 1), 0)

    def conv(x, w):
        y = x * w[CONV_WIDTH - 1:CONV_WIDTH, :]
        for back in range(1, CONV_WIDTH):
            xs = jnp.where(rows >= back, pltpu.roll(x, back, axis=0), 0.0)
            y = y + xs * w[CONV_WIDTH - 1 - back:CONV_WIDTH - back, :]
        return _silu(y)

    def l2norm(a):
        return a * lax.rsqrt(jnp.sum(a * a, axis=-1, keepdims=True) + NORM_EPS)

    lanes = [slice(j * LANES, (j + 1) * LANES) for j in range(GDN_HEADS)]
    for j in range(GDN_HEADS):
        qs[j] = l2norm(conv(q_ref[:, lanes[j]], wq_ref[:, lanes[j]])) * (D_DK ** -0.5)
        ks[j] = l2norm(conv(k_ref[:, lanes[j]], wk_ref[:, lanes[j]]))
        vs[j] = conv(v_ref[:, lanes[j]], wv_ref[:, lanes[j]])
        gate = gate_ref[j]
        gs[j] = -jnp.exp(alog_ref[j]) * _softplus(gate[:, 0:1] + dtb_ref[j])
        bs[j] = jax.nn.sigmoid(gate[:, 1:2])

    sc = GDN_BATCH * c
    ii = lax.broadcasted_iota(jnp.int32, (sc, sc), 0)
    jj = lax.broadcasted_iota(jnp.int32, (sc, sc), 1)
    same = (ii // c) == (jj // c)
    causal, strict, upper, eye = same & (ii >= jj), same & (ii > jj), same & (ii <= jj), ii == jj
    chunk_end = same & (jj % c == c - 1)
    ng = ng_ref[...]

    def load(n, j):
        r0 = pl.multiple_of(n * sc, sc)
        return (qs[j, pl.ds(r0, sc), :], ks[j, pl.ds(r0, sc), :], vs[j, pl.ds(r0, sc), :],
                gs[j, pl.ds(r0, sc), :], bs[j, pl.ds(r0, sc), :])

    def prepare(q, k, v, g, beta):
        g_row = jnp.sum(jnp.where(eye, g, 0.0), axis=0, keepdims=True)
        gc_col = jnp.sum(jnp.where(causal, g_row, 0.0), axis=1, keepdims=True)
        gc_row = jnp.sum(jnp.where(upper, g, 0.0), axis=0, keepdims=True)
        g_last = jnp.sum(jnp.where(chunk_end, gc_row, 0.0), axis=1, keepdims=True)
        decay = jnp.where(causal, jnp.exp(jnp.where(causal, gc_col - gc_row, 0.0)), 0.0)
        eg = jnp.exp(gc_col)
        kb = k * beta
        kbf = k.astype(BF16)
        x = -jnp.where(strict, _dot_nt(kb.astype(BF16), kbf) * decay, 0.0)
        inv = jnp.where(eye, 1.0, 0.0) + x
        xb = x.astype(BF16)
        x = _dot(xb, xb)
        for _ in range(int(math.log2(c)) - 2):
            xb = x.astype(BF16)
            both = _dot(jnp.concatenate([xb, inv.astype(BF16)], axis=0), xb)
            x, inv = both[:sc], inv + both[sc:]
        inv = inv + _dot(inv.astype(BF16), x.astype(BF16))
        sol = _dot(inv.astype(BF16), jnp.concatenate([v * beta, kb * eg], axis=1).astype(BF16)).astype(BF16)
        qk = (_dot_nt(q.astype(BF16), kbf) * decay).astype(BF16)
        qo = _dot(qk, sol)
        kd = k * jnp.exp(g_last - gc_col)
        mn = [_dot(kd[t * c:(t + 1) * c].T.astype(BF16), sol[t * c:(t + 1) * c]) for t in range(GDN_BATCH)]
        g_tot = [jnp.exp(g_last[t * c:t * c + 1]) for t in range(GDN_BATCH)]
        return qo, q * eg, mn, g_tot

    def store(n, j, qo, q_dec, mn, g_tot):
        r0 = pl.multiple_of(n * sc, sc)
        o_ref[pl.ds(r0, sc), lanes[j]] = qo[:, :D_DV]
        qs[j, pl.ds(r0, sc), :] = q_dec - qo[:, D_DV:]
        for t in range(GDN_BATCH):
            n_s[j, n * GDN_BATCH + t] = mn[t][:, :D_DV]
            mc_s[j, n * GDN_BATCH + t] = mn[t][:, D_DV:].astype(BF16)
            gt_s[j, n * GDN_BATCH + t] = jnp.broadcast_to(g_tot[t], (8, LANES))

    def prepare_some(i, carry):
        results = [prepare(*operands) for operands in [load(i, j) for j in range(GDN_HEADS)]]
        for j, res in enumerate(results):
            store(i, j, *res)
        return carry

    lax.fori_loop(0, n_chunks // GDN_BATCH, prepare_some, 0)

    def advance(n, states):
        r0 = pl.multiple_of(n * c, c)
        operands = [(o_ref[pl.ds(r0, c), lanes[j]], qs[j, pl.ds(r0, c), :], gt_s[j, n], mc_s[j, n], n_s[j, n])
                    for j in range(GDN_HEADS)]
        out = []
        for j, (o0, q_eff, g_tot, mc, nn) in enumerate(operands):
            sb = states[j].astype(BF16)
            o = o0 + _dot(q_eff.astype(BF16), sb)
            out.append(states[j] * g_tot[0:1, :] - _dot(mc, sb) + nn)
            operands[j] = o * lax.rsqrt(jnp.mean(o * o, axis=-1, keepdims=True) + NORM_EPS) * ng
        for j in range(GDN_HEADS):
            o_ref[pl.ds(r0, c), lanes[j]] = operands[j]
        return tuple(out)

    lax.fori_loop(0, n_chunks, advance, tuple(jnp.zeros((D_DK, D_DV), F32) for _ in range(GDN_HEADS)))


def _gated_deltanet(qkv, gates, conv_w, a_log, dt_bias, norm_g):
    b, s, _ = qkv.shape
    hp, width = GDN_HEADS, GDN_HEADS * LANES
    groups = D_HEADS // hp
    n_chunks = s // DN_CHUNK
    assert n_chunks % GDN_BATCH == 0
    col = lambda off: pl.BlockSpec((None, s, width), lambda i, h: (i, 0, off + h))
    wcol = lambda off: pl.BlockSpec((CONV_WIDTH, width), lambda i, h: (0, off + h))
    scalar = pl.BlockSpec((hp, 1, 1), lambda i, h: (h, 0, 0))
    return pl.pallas_call(
        _gdn_kernel, grid=(b, groups),
        in_specs=[col(0), col(groups), col(2 * groups), wcol(0), wcol(groups), wcol(2 * groups),
                  pl.BlockSpec((None, hp, s, 2), lambda i, h: (i, h, 0, 0)), scalar, scalar,
                  pl.BlockSpec((1, D_DV), lambda i, h: (0, 0))],
        out_specs=pl.BlockSpec((None, s, width), lambda i, h: (i, 0, h)),
        out_shape=jax.ShapeDtypeStruct((b, s, D_HEADS * D_DV), F32),
        scratch_shapes=[pltpu.VMEM((hp, s, LANES), F32)] * 3 + [pltpu.VMEM((hp, s, 1), F32)] * 2
        + [pltpu.VMEM((hp, n_chunks, D_DK, D_DV), BF16), pltpu.VMEM((hp, n_chunks, D_DK, D_DV), F32),
           pltpu.VMEM((hp, n_chunks, 8, LANES), F32)],
        compiler_params=_cparams(("parallel", "parallel")), name="gated_deltanet",
    )(qkv, qkv, qkv, conv_w, conv_w, conv_w, gates,
      a_log.astype(F32).reshape(D_HEADS, 1, 1), dt_bias.astype(F32).reshape(D_HEADS, 1, 1),
      norm_g.astype(F32).reshape(1, D_DV))


def _out_kernel(*refs, odd, final):
    x_ref, mg_ref, gate_ref, w_ref = refs[:4]
    o_ref = refs[-1]
    rest = list(refs[4:-1])
    fin_ref = rest.pop() if final else None
    half = w_ref.shape[0] // 2
    sg = _silu(gate_ref[...])
    if odd:
        cmp_ref, slc_ref, win_ref, od_ref, small_ref = rest
        lane = lax.broadcasted_iota(jnp.int32, (1, LANES), 1)
        bg = jax.nn.sigmoid(small_ref[...])
        blocks = []
        for p_blk in range(C_HPG):
            sl = slice(p_blk * LANES, (p_blk + 1) * LANES)
            acc = 0.0
            for br, ref in enumerate((cmp_ref, slc_ref, win_ref)):
                ca = 2 * D_HEADS + p_blk * N_BRANCH + br
                cb = 2 * D_HEADS + (p_blk + C_HPG) * N_BRANCH + br
                acc = acc + jnp.where(lane < HEAD_DIM, bg[:, ca:ca + 1], bg[:, cb:cb + 1]) * ref[:, sl]
            blocks.append(acc)
        first = jnp.concatenate(blocks, axis=1)
        second = od_ref[...]
    else:
        first, second = rest[0][...], rest[1][...]
    y = (_dot((first * sg[:, :half]).astype(BF16), w_ref[:half, :])
         + _dot((second * sg[:, half:]).astype(BF16), w_ref[half:, :]))
    out = x_ref[...] + mg_ref[...] * y
    if final:
        out = out * lax.rsqrt(jnp.mean(out * out, axis=-1, keepdims=True) + NORM_EPS) * fin_ref[...]
    o_ref[...] = out


def _out_projection(x, mod_l, gate, w, branches, final_g=None):
    b, s, d = x.shape
    ts = min(TOKEN_TILE, s)
    odd = len(branches) > 2
    row = lambda width: pl.BlockSpec((None, ts, width), lambda i, j: (i, j, 0))
    in_specs = [row(d), pl.BlockSpec((None, None, 1, d), lambda i, j: (i, 2, 0, 0)), row(gate.shape[-1]),
                pl.BlockSpec(w.shape, lambda i, j: (0, 0), pipeline_mode=pl.Buffered(1))]
    in_specs += [row(a.shape[-1]) for a in branches]
    args = [x, mod_l, gate, w, *branches]
    if final_g is not None:
        in_specs.append(pl.BlockSpec((1, d), lambda i, j: (0, 0)))
        args.append(final_g.reshape(1, d))
    return pl.pallas_call(
        functools.partial(_out_kernel, odd=odd, final=final_g is not None), grid=(b, s // ts),
        in_specs=in_specs, out_specs=row(d), out_shape=jax.ShapeDtypeStruct((b, s, d), F32),
        compiler_params=_cparams(("parallel", "parallel")), name="gated_out_proj",
    )(*args)


def _pair_cols(a0, b0):
    a, bb = np.arange(a0, a0 + HEAD_DIM), np.arange(b0, b0 + HEAD_DIM)
    return np.concatenate([a[:HALF], bb[:HALF], a[HALF:], bb[HALF:]])


def _paired_head_order(width):
    pairs = [np.concatenate([np.arange(p * HEAD_DIM, (p + 1) * HEAD_DIM),
                             np.arange((p + C_HPG) * HEAD_DIM, (p + C_HPG + 1) * HEAD_DIM)]) for p in range(C_HPG)]
    return np.concatenate(pairs + [np.arange(C_HEADS * HEAD_DIM, width)])


def _even_layout():
    aq, ak, av = 0, 512, 1024
    bq, bk, bv, bf, gate = 1536, 2048, 2560, 3072, 3080
    zero = gate + 1024
    cols = [_pair_cols(ak + 2 * h * HEAD_DIM, ak + (2 * h + 1) * HEAD_DIM) for h in range(A_HEADS)]
    cols.append(np.arange(bk, bk + 512))
    cols += [np.concatenate([np.repeat(np.arange(bf + 2 * p, bf + 2 * p + 2), BIAS_PIECES),
                             np.full(LANES - 2 * BIAS_PIECES, zero)]) for p in range(B_HEADS // 2)]
    cols.append(np.arange(gate, gate + 1024))
    segs = (("rope", 0, 512), ("bf16", 512, 1024), ("f32", 1024, 1536), ("f32", 1536, 2560))
    rows = [_pair_cols(aq + 2 * h * HEAD_DIM, aq + (2 * h + 1) * HEAD_DIM) for h in range(A_HEADS)]
    rows += [np.arange(bq, bq + 512), np.arange(av, av + 512), np.arange(bv, bv + 512)]
    tsegs = (("rope", 0, 512), ("bf16", 512, 2048))
    return np.concatenate(cols), segs, np.concatenate(rows), tsegs


def _odd_layout():
    cq, kc, vc, ks, vs, kw, vw, cg = 0, 512, 640, 768, 896, 1024, 1152, 1280
    dq, da, db, gate = 1304, 2840, 2844, 2848
    zero = gate + 1024
    cols = [_pair_cols(ks, ks + HEAD_DIM), _pair_cols(kw, kw + HEAD_DIM), np.arange(kc, kc + 256)]
    small = np.concatenate([np.arange(da, da + 2 * D_HEADS), np.arange(cg, cg + C_HEADS * N_BRANCH)])
    cols.append(np.concatenate([small, np.full(LANES - small.size, zero)]))
    cols += [np.arange(dq, dq + 1536), gate + _paired_head_order(1024)]
    segs = (("rope", 0, 256), ("bf16", 256, 512), ("f32", 512, 640), ("f32", 640, 2176), ("f32", 2176, 3200))
    rows = [_pair_cols(cq + p * HEAD_DIM, cq + (p + C_HPG) * HEAD_DIM) for p in range(C_HPG)]
    rows += [np.arange(vs, vs + LANES), np.arange(vw, vw + LANES)]
    tsegs = (("rope+raw", 0, 512), ("bf16", 512, 768))
    return np.concatenate(cols), segs, np.concatenate(rows), tsegs


def _layout_weights(w, cols, rows, n_query_rows):
    w = jnp.concatenate([w, jnp.zeros((w.shape[0], 1), w.dtype)], axis=1)
    scale = jnp.where(jnp.arange(rows.size) < n_query_rows, QK_SCALE, 1.0).astype(w.dtype)
    return w[:, cols].astype(BF16), (w[:, rows] * scale).T.astype(BF16)


def _compress_weights(pe, w1, w2, for_keys):
    half = L_CMP // 2 * HEAD_DIM
    w1ab = jnp.concatenate([w1[:half], w1[half:]], axis=1).astype(BF16)
    w2p = jnp.zeros((C_GROUPS, CMP_HIDDEN, LANES), F32)
    for g in range(C_GROUPS):
        if for_keys:
            w2p = w2p.at[g, :, g * HALF:(g + 1) * HALF].set(w2[:, :HALF])
            w2p = w2p.at[g, :, HEAD_DIM + g * HALF:HEAD_DIM + (g + 1) * HALF].set(w2[:, HALF:])
        else:
            w2p = w2p.at[g, :, g * HEAD_DIM:(g + 1) * HEAD_DIM].set(w2)
    if not for_keys:
        w2p = w2p.transpose(0, 2, 1)
    return [w1ab, w1.astype(F32), pe.astype(F32).reshape(1, L_CMP * HEAD_DIM), w2p.astype(BF16)]


def _chunk_layout(a):
    b, s, _ = a.shape
    a = a.reshape(b, s // CMP_STRIDE, CMP_STRIDE, C_GROUPS, HEAD_DIM)
    return a.transpose(0, 3, 1, 2, 4).reshape(b, C_GROUPS, s // CMP_STRIDE, CMP_STRIDE * HEAD_DIM)


def _even_layer(x, mod_l, rope, layer_idx, g, w_in, b_forget, lq1, lk1, lq2, lk2, subln_g, w_out, final_g):
    cols, segs, rows, tsegs = _even_layout()
    w, wt = _layout_weights(w_in, cols, rows, 2 * A_HEADS * HEAD_DIM + B_HEADS * HEAD_DIM)
    k_a, k_b, forget_logits, gate, qt_a, rest_t = _projection(x, mod_l, g, rope, w, wt, segs, tsegs)
    lam_init = 0.8 - 0.6 * math.exp(-0.3 * layer_idx)
    vec = lambda a: a.astype(F32).reshape(1, -1)
    const = lambda shape: (lambda tq, tk, kv_of: pl.BlockSpec(shape, lambda i, hb, qi, ki: (0, 0)))
    nb = B_HEADS // 2
    per_step = 2
    oa = _flash("diff", qt_a, k_a, rest_t, A_HEADS, per_step, 0, 0, nb, False,
                extra=[vec(lq1), vec(lk1), vec(lq2), vec(lk2), subln_g.astype(F32).reshape(LANES, 1)],
                extra_specs=[const((1, HEAD_DIM))] * 4 + [const((LANES, 1))], lam_init=lam_init)
    bias = _forget_cumsum(forget_logits, b_forget)
    bias_spec = lambda tq, tk, kv_of: pl.BlockSpec((None, per_step, tk, LANES),
                                                   lambda i, hb, qi, ki: (i, hb, kv_of(qi, ki), 0))
    ob = _flash("fox", rest_t, k_b, rest_t, nb, per_step, 0, 0, nb + A_HEADS, False,
                extra=[bias], extra_specs=[bias_spec])
    return _out_projection(x, mod_l, gate, w_out.astype(BF16), [oa, ob], final_g)


def _odd_layer(x, mod_l, rope, g, w_in, pe_k, pe_v, w1_k, w2_k, w1_v, w2_v, conv_w, a_log, dt_bias,
               dn_norm_g, w_out, final_g):
    b, s, _ = x.shape
    cols, segs, rows, tsegs = _odd_layout()
    w, wt = _layout_weights(w_in, cols, rows, C_HEADS * HEAD_DIM)
    k_rot, cmp_in, small, dqkv, gate, q_raw_t, q_rot_t, v_t = _projection(x, mod_l, g, rope, w, wt, segs, tsegs)
    n_cmp = (s - L_CMP) // CMP_STRIDE + 1
    kcmp, vcmp_t = _compress(_chunk_layout(cmp_in[:, :, :LANES]), _chunk_layout(cmp_in[:, :, LANES:]),
                             _compress_weights(pe_k, w1_k, w2_k, True), _compress_weights(pe_v, w1_v, w2_v, False))
    o_cmp, sel = _cmp_select(q_raw_t, kcmp, vcmp_t, n_cmp)
    sel_spec = lambda tq, tk, kv_of: pl.BlockSpec((None, LANES, tq), lambda i, hb, qi, ki: (i, 0, qi))
    o_slc = _flash("sel", q_rot_t, k_rot, v_t, C_HPG, C_HPG, 0, 0, 0, True, extra=[sel], extra_specs=[sel_spec])
    o_win = _flash("win", q_rot_t, k_rot, v_t, C_HPG, C_HPG, 0, 1, 1, True)
    gates = small[:, :, :2 * D_HEADS].reshape(b, s, 2, D_HEADS).transpose(0, 3, 1, 2)
    od = _gated_deltanet(dqkv, gates, conv_w.astype(F32), a_log, dt_bias, dn_norm_g)
    rows = _paired_head_order(w_out.shape[0])
    return _out_projection(x, mod_l, gate, w_out[rows].astype(BF16), [o_cmp, o_slc, o_win, od, small], final_g)


def kernel(x, c, positions, norm_g, w_mod, b_mod, w_out, final_norm_g, w_in_even, b_forget, lambda_q1, lambda_k1,
           lambda_q2, lambda_k2, subln_g, w_in_odd, cmp_pe_k, cmp_pe_v, cmp_w1_k, cmp_w2_k, cmp_w1_v, cmp_w2_v,
           conv_w, a_log, dt_bias, dn_norm_g):
    depth = norm_g.shape[0]
    rope = _rope_tables(positions)
    mod = _modulation(c, w_mod, b_mod)
    for l in range(depth):
        final_g = final_norm_g if l == depth - 1 else None
        i = l // 2
        if l % 2 == 0:
            x = _even_layer(x, mod[l], rope, l, norm_g[l], w_in_even[i], b_forget[i], lambda_q1[i],
                            lambda_k1[i], lambda_q2[i], lambda_k2[i], subln_g[i], w_out[l], final_g)
        else:
            x = _odd_layer(x, mod[l], rope, norm_g[l], w_in_odd[i], cmp_pe_k[i], cmp_pe_v[i], cmp_w1_k[i],
                           cmp_w2_k[i], cmp_w1_v[i], cmp_w2_v[i], conv_w[i], a_log[i], dt_bias[i], dn_norm_g[i],
                           w_out[l], final_g)
    return x
```

```python
import functools
import math

import jax
import jax.numpy as jnp
import numpy as np
from jax import lax
from jax.experimental import pallas as pl
from jax.experimental.pallas import tpu as pltpu

F32 = jnp.float32
BF16 = jnp.bfloat16
HI = lax.Precision.HIGHEST

LANES = 128
HEAD_DIM = 64
HALF = HEAD_DIM // 2
ROPE_THETA = 10000.0
NORM_EPS = 1e-6
NEG = -1e30
MASK_BIG = 1e30
LOG2E = math.log2(math.e)
QK_SCALE = HEAD_DIM ** -0.5 * LOG2E
BIAS_PIECES = 3
A_HEADS = 4
B_HEADS = 8
C_HEADS = 8
C_GROUPS = 2
C_HPG = C_HEADS // C_GROUPS
L_CMP = 32
CMP_STRIDE = 16
CMP_HIDDEN = 256
L_SEL = 64
N_SEL = 8
WINDOW = 512
N_BRANCH = 3
FORCE_BONUS = 1e4
D_HEADS = 4
D_DK = 128
D_DV = 128
CONV_WIDTH = 4
DN_CHUNK = 64
TOKEN_TILE = 512
VMEM_LIMIT = 56 * 1024 * 1024


def _cparams(sem):
    return pltpu.CompilerParams(dimension_semantics=sem, vmem_limit_bytes=VMEM_LIMIT)


def _dot(a, b, precision=None):
    return jnp.dot(a, b, precision=precision, preferred_element_type=F32)


def _dot_nt(a, b, precision=None):
    return lax.dot_general(a, b, (((1,), (1,)), ((), ())), precision=precision, preferred_element_type=F32)


def _dot_tn(a, b, precision=None):
    return lax.dot_general(a, b, (((0,), (0,)), ((), ())), precision=precision, preferred_element_type=F32)


def _split_bf16(a):
    hi = a.astype(BF16)
    return hi, (a - hi.astype(F32)).astype(BF16)


def _softplus(z):
    return jnp.maximum(z, 0.0) + jnp.log1p(jnp.exp(-jnp.abs(z)))


def _silu(z):
    return z * jax.nn.sigmoid(z)


def _rope_table_kernel(pos_ref, inv_ref, cos_ref, sin_ref, cos_t_ref, sin_t_ref):
    ang = pos_ref[...].astype(F32) * inv_ref[...]
    lane = lax.broadcasted_iota(jnp.int32, (1, LANES), 1)
    cos = jnp.cos(ang)
    sin = jnp.where(lane < 2 * HALF, -1.0, 1.0) * jnp.sin(ang)
    cos_ref[...] = cos
    sin_ref[...] = sin
    cos_t_ref[...] = cos.T
    sin_t_ref[...] = sin.T


def _rope_tables(positions):
    b, s = positions.shape
    inv = ROPE_THETA ** (-jnp.arange(0, HEAD_DIM, 2, dtype=F32) / HEAD_DIM)
    inv = jnp.tile(inv, 4).reshape(1, LANES)
    tok = jax.ShapeDtypeStruct((b, s, LANES), F32)
    feat = jax.ShapeDtypeStruct((b, LANES, s), F32)
    return pl.pallas_call(
        _rope_table_kernel, grid=(b,),
        in_specs=[pl.BlockSpec((None, s, 1), lambda i: (i, 0, 0)),
                  pl.BlockSpec((1, LANES), lambda i: (0, 0))],
        out_specs=[pl.BlockSpec((None, s, LANES), lambda i: (i, 0, 0))] * 2
        + [pl.BlockSpec((None, LANES, s), lambda i: (i, 0, 0))] * 2,
        out_shape=[tok, tok, feat, feat], compiler_params=_cparams(("parallel",)), name="rope_tables",
    )(positions.reshape(b, s, 1), inv)


def _mod_kernel(c_ref, w_ref, b_ref, o_ref):
    o_ref[...] = _dot(_silu(c_ref[...]), w_ref[...], HI) + b_ref[...]


def _modulation(c, w_mod, b_mod):
    depth, d, n = w_mod.shape
    b = c.shape[0]
    tn = 1024
    mod = pl.pallas_call(
        _mod_kernel, grid=(depth, n // tn),
        in_specs=[pl.BlockSpec((b, d), lambda l, j: (0, 0)),
                  pl.BlockSpec((None, d, tn), lambda l, j: (l, 0, j)),
                  pl.BlockSpec((None, 1, tn), lambda l, j: (l, 0, j))],
        out_specs=pl.BlockSpec((None, b, tn), lambda l, j: (l, 0, j)),
        out_shape=jax.ShapeDtypeStruct((depth, b, n), F32),
        compiler_params=_cparams(("parallel", "parallel")), name="modulation",
    )(c, w_mod, b_mod.reshape(depth, 1, n))
    return mod.reshape(depth, b, 3, 1, d)


def _proj_kernel(x_ref, shift_ref, scale_ref, g_ref, cos_ref, sin_ref, cos_t_ref, sin_t_ref, w_ref, wt_ref,
                 *out_refs, segs, tsegs):
    x = x_ref[...]
    h = x * lax.rsqrt(jnp.mean(x * x, axis=-1, keepdims=True) + NORM_EPS) * g_ref[...]
    h = (h * (1.0 + scale_ref[...]) + shift_ref[...]).astype(BF16)
    outs = list(out_refs)
    step = 4 * LANES
    for kind, c0, c1 in segs:
        o_ref = outs.pop(0)
        for a in range(c0, c1, step):
            e = min(a + step, c1)
            acc = _dot(h, w_ref[:, a:e])
            if kind == "rope":
                cos, sin = cos_ref[...], sin_ref[...]
                for j in range(0, e - a, LANES):
                    blk = acc[:, j:j + LANES]
                    rot = blk * cos + pltpu.roll(blk, 2 * HALF, axis=1) * sin
                    o_ref[:, a - c0 + j:a - c0 + j + LANES] = rot.astype(o_ref.dtype)
            else:
                o_ref[:, a - c0:e - c0] = acc.astype(o_ref.dtype)
    for kind, r0, r1 in tsegs:
        raw_ref = outs.pop(0) if kind == "rope+raw" else None
        o_ref = outs.pop(0)
        for a in range(r0, r1, step):
            e = min(a + step, r1)
            acc = _dot_nt(wt_ref[a:e, :], h)
            if raw_ref is not None:
                raw_ref[a - r0:e - r0, :] = acc
            if kind in ("rope", "rope+raw"):
                cos, sin = cos_t_ref[...], sin_t_ref[...]
                for j in range(0, e - a, LANES):
                    blk = acc[j:j + LANES, :]
                    rot = blk * cos + pltpu.roll(blk, 2 * HALF, axis=0) * sin
                    o_ref[a - r0 + j:a - r0 + j + LANES, :] = rot.astype(o_ref.dtype)
            else:
                o_ref[a - r0:e - r0, :] = acc.astype(o_ref.dtype)


def _projection(x, mod_l, g, rope, w, wt, segs, tsegs):
    b, s, d = x.shape
    ts = min(TOKEN_TILE, s)
    row = lambda width: pl.BlockSpec((None, ts, width), lambda i, j: (i, j, 0))
    col = lambda height: pl.BlockSpec((None, height, ts), lambda i, j: (i, 0, j))
    out_specs, out_shapes = [], []
    for kind, c0, c1 in segs:
        out_specs.append(row(c1 - c0))
        out_shapes.append(jax.ShapeDtypeStruct((b, s, c1 - c0), F32 if kind == "f32" else BF16))
    for kind, r0, r1 in tsegs:
        if kind == "rope+raw":
            out_specs.append(col(r1 - r0))
            out_shapes.append(jax.ShapeDtypeStruct((b, r1 - r0, s), F32))
        out_specs.append(col(r1 - r0))
        out_shapes.append(jax.ShapeDtypeStruct((b, r1 - r0, s), BF16))
    modspec = lambda k: pl.BlockSpec((None, None, 1, d), lambda i, j: (i, k, 0, 0))
    resident = lambda a: pl.BlockSpec(a.shape, lambda i, j: (0, 0), pipeline_mode=pl.Buffered(1))
    return pl.pallas_call(
        functools.partial(_proj_kernel, segs=segs, tsegs=tsegs), grid=(b, s // ts),
        in_specs=[row(d), modspec(0), modspec(1), pl.BlockSpec((1, d), lambda i, j: (0, 0)),
                  row(LANES), row(LANES), col(LANES), col(LANES), resident(w), resident(wt)],
        out_specs=out_specs, out_shape=out_shapes,
        compiler_params=_cparams(("parallel", "parallel")), name="adaln_in_proj",
    )(x, mod_l, mod_l, g.reshape(1, d), *rope, w, wt)


def _cum_kernel(x_ref, bias_ref, o_ref):
    s = x_ref.shape[0]
    ii = lax.broadcasted_iota(jnp.int32, (LANES, LANES), 0)
    jj = lax.broadcasted_iota(jnp.int32, (LANES, LANES), 1)
    lower = (ii >= jj).astype(F32)
    lane = lax.broadcasted_iota(jnp.int32, (1, LANES), 1)
    carry = jnp.zeros((1, LANES), F32)
    for r0 in range(0, s, LANES):
        z = x_ref[r0:r0 + LANES, :] + bias_ref[...]
        logf = jnp.minimum(z, 0.0) - jnp.log1p(jnp.exp(-jnp.abs(z)))
        loc = _dot(lower, logf, HI) + carry
        carry = loc[LANES - 1:LANES, :]
        val = loc * (-LOG2E)
        hi = val.astype(BF16)
        rest = val - hi.astype(F32)
        mid = rest.astype(BF16)
        lo = (rest - mid.astype(F32)).astype(BF16)
        piece = jnp.where(lane % BIAS_PIECES == 0, hi, jnp.where(lane % BIAS_PIECES == 1, mid, lo))
        o_ref[r0:r0 + LANES, :] = jnp.where(lane < 2 * BIAS_PIECES, piece, jnp.zeros_like(piece))


def _forget_cumsum(logits, b_forget):
    b, s, width = logits.shape
    pairs = width // LANES
    per_lane = jnp.repeat(b_forget.astype(F32).reshape(pairs, 2), BIAS_PIECES, axis=1)
    bias = jnp.zeros((pairs, 1, LANES), F32).at[:, 0, :2 * BIAS_PIECES].set(per_lane)
    return pl.pallas_call(
        _cum_kernel, grid=(b, pairs),
        in_specs=[pl.BlockSpec((None, s, LANES), lambda i, p: (i, 0, p)),
                  pl.BlockSpec((None, 1, LANES), lambda i, p: (p, 0, 0))],
        out_specs=pl.BlockSpec((None, None, s, LANES), lambda i, p: (i, p, 0, 0)),
        out_shape=jax.ShapeDtypeStruct((b, pairs, s, LANES), BF16),
        compiler_params=_cparams(("parallel", "parallel")), name="forget_cumsum",
    )(logits, bias)


def _triangle_step(t, nq):
    qi = sum((t >= j * (j + 1) // 2).astype(jnp.int32) for j in range(1, nq))
    return qi, t - qi * (qi + 1) // 2


def _flash_kernel(*refs, mode, tq, tk, nq, nblk, shared_kv, lam_init):
    qt_ref, k_ref, vt_ref = refs[:3]
    m_ref, l_ref, acc_ref = refs[-3:]
    o_ref = refs[-4]
    extra = refs[3:-4]
    if mode == "win":
        qi, ki = pl.program_id(2), pl.program_id(3)
        kv, last = qi - 1 + ki, ki == pl.num_programs(3) - 1
    else:
        qi, ki = _triangle_step(pl.program_id(2), nq)
        kv, last = ki, ki == qi
    row = lax.broadcasted_iota(jnp.int32, (LANES, 1), 0)
    if mode == "fox":
        slot_rows = (row < HEAD_DIM, row >= HEAD_DIM)
    else:
        slot_rows = ((row % HEAD_DIM) < HALF, (row % HEAD_DIM) >= HALF)
    vrows = acc_ref.shape[1]
    block = lambda j: slice(j * LANES, (j + 1) * LANES)

    @pl.when(ki == 0)
    def _():
        m_ref[...] = jnp.full(m_ref.shape, NEG, F32)
        l_ref[...] = jnp.zeros(l_ref.shape, F32)
        acc_ref[...] = jnp.zeros(acc_ref.shape, F32)

    def step(masked):
        rr = lax.broadcasted_iota(jnp.int32, (LANES, tq), 0)
        q_extra, k_extra = [None, None], [None, None]
        for s in range(2):
            if mode == "fox":
                q_extra[s] = jnp.where((rr >= BIAS_PIECES * s) & (rr < BIAS_PIECES * (s + 1)), 1.0, 0.0).astype(BF16)
            if mode == "sel":
                cb = (kv * tk + lax.broadcasted_iota(jnp.int32, (tk, LANES), 0)) // L_SEL
                ll = lax.broadcasted_iota(jnp.int32, (tk, LANES), 1)
                k_extra[s] = jnp.where(ll == cb + s * (LANES // 4), MASK_BIG, 0.0).astype(BF16)
                q_extra[s] = extra[0][...]
        sts = []
        for j in range(nblk):
            qt = qt_ref[block(j), :]
            k = k_ref[...] if shared_kv else k_ref[:, block(j)]
            for s in range(2):
                qs = jnp.where(slot_rows[s], qt, jnp.zeros_like(qt))
                if q_extra[s] is None:
                    sts.append(_dot(k, qs))
                else:
                    k_more = extra[0][j] if mode == "fox" else k_extra[s]
                    sts.append(_dot(jnp.concatenate([k, k_more], axis=1), jnp.concatenate([qs, q_extra[s]], axis=0)))
        if masked:
            kpos = kv * tk + lax.broadcasted_iota(jnp.int32, (tk, tq), 0)
            qpos = qi * tq + lax.broadcasted_iota(jnp.int32, (tk, tq), 1)
            keep = kpos <= qpos
            if mode == "win":
                keep = keep & (qpos - kpos < WINDOW)
        for idx, st in enumerate(sts):
            j, s = divmod(idx, 2)
            if masked:
                st = jnp.where(keep, st, NEG)
            m_prev = m_ref[idx]
            m_new = jnp.maximum(m_prev, jnp.max(st, axis=0, keepdims=True))
            alpha = jnp.exp2(m_prev - m_new)
            p = jnp.exp2(st - m_new)
            l_ref[idx] = alpha * l_ref[idx] + jnp.sum(p, axis=0, keepdims=True)
            base = 0 if shared_kv else j * LANES
            if vrows == LANES:
                vt = vt_ref[base:base + LANES, :]
            else:
                vt = vt_ref[base + s * vrows:base + (s + 1) * vrows, :]
            acc_ref[idx] = alpha * acc_ref[idx] + _dot(vt, p.astype(BF16))
            m_ref[idx] = m_new

    if mode == "win":
        pl.when(kv >= 0)(functools.partial(step, True))
    else:
        pl.when(ki < qi)(functools.partial(step, False))
        pl.when(ki == qi)(functools.partial(step, True))

    @pl.when(last)
    def _():
        for j in range(nblk):
            o0 = acc_ref[2 * j] / l_ref[2 * j]
            o1 = acc_ref[2 * j + 1] / l_ref[2 * j + 1]
            if mode == "diff":
                lq1, lk1, lq2, lk2, subg = (r[...] for r in extra)
                lam = (jnp.exp(jnp.sum(lq1 * lk1, axis=1, keepdims=True))
                       - jnp.exp(jnp.sum(lq2 * lk2, axis=1, keepdims=True)) + lam_init)
                o = o0 - lam * o1
                o = o * lax.rsqrt(jnp.mean(o * o, axis=0, keepdims=True) + NORM_EPS) * subg * (1.0 - lam_init)
            else:
                o = jnp.concatenate([o0, o1], axis=0)
            o_ref[:, block(j)] = o.T.astype(o_ref.dtype)


def _flash(mode, qt, k, vt, n_qblocks, nblk, q0, k0, v0, shared_kv, extra=(), extra_specs=(), lam_init=0.0):
    b, s, _ = k.shape
    tq = tk = min(TOKEN_TILE, s)
    nq = s // tq
    if mode == "win":
        assert WINDOW == tk
        steps, sem = (nq, 2), ("parallel", "arbitrary")
        tile = lambda qi, ki: (qi, jnp.maximum(qi - 1 + ki, 0))
    else:
        steps, sem = (nq * (nq + 1) // 2,), ("arbitrary",)
        tile = lambda t: _triangle_step(t, nq)
    assert n_qblocks % nblk == 0 and q0 % nblk == 0 and (shared_kv or (k0 % nblk == 0 and v0 % nblk == 0))
    wide = nblk * LANES
    if shared_kv:
        k_spec = pl.BlockSpec((None, tk, LANES), lambda i, hb, *t: (i, tile(*t)[1], k0))
        v_spec = pl.BlockSpec((None, LANES, tk), lambda i, hb, *t: (i, v0, tile(*t)[1]))
    else:
        k_spec = pl.BlockSpec((None, tk, wide), lambda i, hb, *t: (i, tile(*t)[1], k0 // nblk + hb))
        v_spec = pl.BlockSpec((None, wide, tk), lambda i, hb, *t: (i, v0 // nblk + hb, tile(*t)[1]))
    in_specs = [pl.BlockSpec((None, wide, tq), lambda i, hb, *t: (i, q0 // nblk + hb, tile(*t)[0])), k_spec, v_spec]
    for spec in extra_specs:
        in_specs.append(spec(tq, tk, tile))
    vrows = LANES if mode == "diff" else HEAD_DIM
    return pl.pallas_call(
        functools.partial(_flash_kernel, mode=mode, tq=tq, tk=tk, nq=nq, nblk=nblk, shared_kv=shared_kv,
                          lam_init=lam_init),
        grid=(b, n_qblocks // nblk) + steps, in_specs=in_specs,
        out_specs=pl.BlockSpec((None, tq, wide), lambda i, hb, *t: (i, tile(*t)[0], hb)),
        out_shape=jax.ShapeDtypeStruct((b, s, n_qblocks * LANES), BF16),
        scratch_shapes=[pltpu.VMEM((2 * nblk, 1, tq), F32), pltpu.VMEM((2 * nblk, 1, tq), F32),
                        pltpu.VMEM((2 * nblk, vrows, tq), F32)],
        compiler_params=_cparams(("parallel", "parallel") + sem),
        name="flash_" + mode,
    )(qt, k, vt, *extra)


def _compress_kernel(xk_ref, xv_ref, w1_by_tok_k, w1_k, pe_k, w2_k, w1_by_tok_v, w1_v, pe_v, w2_v, ok_ref, ov_ref):
    cn = xk_ref.shape[0] // CMP_STRIDE

    def run(x_ref, w1_by_tok, w1, pe, w2, o_ref, transposed):
        pe_term = _dot(pe[...], w1[...], HI)
        ab = jnp.zeros((cn, C_GROUPS * 2 * CMP_HIDDEN), F32)
        for tok in range(CMP_STRIDE):
            rows = x_ref[pl.ds(tok, cn, stride=CMP_STRIDE), :]
            ab = ab + _dot(rows.astype(BF16), w1_by_tok[tok])
        out = 0.0
        for g in range(C_GROUPS):
            first = ab[:, g * 2 * CMP_HIDDEN:(g * 2 + 1) * CMP_HIDDEN]
            second = ab[:, (g * 2 + 1) * CMP_HIDDEN:(g + 1) * 2 * CMP_HIDDEN]
            hid = first + pltpu.roll(second, cn - 1, axis=0) + pe_term
            act = _silu(hid).astype(BF16)
            out = out + (_dot_nt(w2[g], act) if transposed else _dot(act, w2[g]))
        o_ref[...] = out
    run(xk_ref, w1_by_tok_k, w1_k, pe_k, w2_k, ok_ref, False)
    run(xv_ref, w1_by_tok_v, w1_v, pe_v, w2_v, ov_ref, True)


def _compress(cmp_in, weights_k, weights_v):
    b, s, width = cmp_in.shape
    cn = s // CMP_STRIDE
    full = lambda a: pl.BlockSpec(a.shape, lambda i: (0,) * a.ndim)
    return pl.pallas_call(
        _compress_kernel, grid=(b,),
        in_specs=[pl.BlockSpec((None, s, LANES), lambda i: (i, 0, 0)), pl.BlockSpec((None, s, LANES), lambda i: (i, 0, 1))]
        + [full(a) for a in weights_k + weights_v],
        out_specs=[pl.BlockSpec((None, cn, LANES), lambda i: (i, 0, 0)),
                   pl.BlockSpec((None, LANES, cn), lambda i: (i, 0, 0))],
        out_shape=[jax.ShapeDtypeStruct((b, cn, LANES), F32), jax.ShapeDtypeStruct((b, LANES, cn), F32)],
        compiler_params=_cparams(("parallel",)), name="nsa_compress",
    )(cmp_in, cmp_in, *weights_k, *weights_v)


def _cmp_select_kernel(qt_ref, kc_ref, vct_ref, o_ref, sel_ref, *, tq, n_blk, n_cmp):
    qi = pl.program_id(1)
    cn = kc_ref.shape[0]
    kc_hi, kc_lo = _split_bf16(kc_ref[...])
    vct = vct_ref[...].astype(BF16)
    row = lax.broadcasted_iota(jnp.int32, (LANES, 1), 0)
    slot_rows = ((row % HEAD_DIM) < HALF, (row % HEAD_DIM) >= HALF)
    t_row = qi * tq + lax.broadcasted_iota(jnp.int32, (1, tq), 1)
    m_col = lax.broadcasted_iota(jnp.int32, (cn, 1), 0)
    valid = (m_col * CMP_STRIDE + L_CMP - 1 <= t_row) & (m_col < n_cmp)
    any_valid = (t_row >= L_CMP - 1).astype(F32)
    psum = [jnp.zeros((cn, tq), F32) for _ in range(C_GROUPS)]
    for p_blk in range(C_HPG):
        qt = qt_ref[p_blk * LANES:(p_blk + 1) * LANES, :]
        outs = []
        for s in range(C_GROUPS):
            q_hi, q_lo = _split_bf16(jnp.where(slot_rows[s], qt, 0.0))
            sc = jnp.where(valid, _dot(kc_hi, q_hi) + _dot(kc_lo, q_hi) + _dot(kc_hi, q_lo), NEG)
            e = jnp.exp2(sc - jnp.max(sc, axis=0, keepdims=True))
            p = e / jnp.sum(e, axis=0, keepdims=True) * any_valid
            psum[s] = psum[s] + p
            outs.append(_dot(vct[s * HEAD_DIM:(s + 1) * HEAD_DIM, :], p.astype(BF16)))
        o_ref[:, p_blk * LANES:(p_blk + 1) * LANES] = jnp.concatenate(outs, axis=0).T.astype(o_ref.dtype)
    jb = lax.broadcasted_iota(jnp.int32, (n_blk, cn), 0)
    mm = lax.broadcasted_iota(jnp.int32, (n_blk, cn), 1)
    overlap = ((mm * CMP_STRIDE < jb * L_SEL + L_SEL) & (mm * CMP_STRIDE + L_CMP > jb * L_SEL)
               & (mm < n_cmp)).astype(BF16)
    j = lax.broadcasted_iota(jnp.int32, (n_blk, tq), 0)
    cur = (qi * tq + lax.broadcasted_iota(jnp.int32, (n_blk, tq), 1)) // L_SEL
    forced = (j == 0) | (j == cur) | (j == cur - 1)
    n_top = min(N_SEL, n_blk)
    pad_rows = LANES // 4 - n_blk
    parts = []
    for s in range(C_GROUPS):
        p_hi, p_lo = _split_bf16(psum[s])
        imp = _dot(overlap, p_hi) + _dot(overlap, p_lo)
        score = jnp.where(j > cur, NEG, imp + jnp.where(forced, FORCE_BONUS, 0.0))
        rank = jnp.zeros((n_blk, tq), jnp.int32)
        for jp in range(n_blk):
            r = score[jp:jp + 1, :]
            rank = rank + ((r > score) | ((r == score) & (jp < j))).astype(jnp.int32)
        parts.append(jnp.where(rank < n_top, 0.0, -1.0))
        if pad_rows:
            parts.append(jnp.zeros((pad_rows, tq), F32))
    parts.append(jnp.zeros((LANES // 2, tq), F32))
    sel_ref[...] = jnp.concatenate(parts, axis=0).astype(sel_ref.dtype)


def _cmp_select(q_raw_t, kcmp, vcmp_t, n_cmp):
    b, width, s = q_raw_t.shape
    cn = kcmp.shape[1]
    tq = min(TOKEN_TILE, s)
    n_blk = s // L_SEL
    return pl.pallas_call(
        functools.partial(_cmp_select_kernel, tq=tq, n_blk=n_blk, n_cmp=n_cmp), grid=(b, s // tq),
        in_specs=[pl.BlockSpec((None, width, tq), lambda i, j: (i, 0, j)),
                  pl.BlockSpec((None, cn, LANES), lambda i, j: (i, 0, 0)),
                  pl.BlockSpec((None, LANES, cn), lambda i, j: (i, 0, 0))],
        out_specs=[pl.BlockSpec((None, tq, width), lambda i, j: (i, j, 0)),
                   pl.BlockSpec((None, LANES, tq), lambda i, j: (i, 0, j))],
        out_shape=[jax.ShapeDtypeStruct((b, s, width), BF16), jax.ShapeDtypeStruct((b, LANES, s), BF16)],
        compiler_params=_cparams(("parallel", "parallel")), name="nsa_cmp_select",
    )(q_raw_t, kcmp, vcmp_t)


GDN_HEADS = 2
GDN_BATCH = 4
GDN_UNROLL = 2


def _gdn_kernel(q_ref, k_ref, v_ref, wq_ref, wk_ref, wv_ref, gate_ref, alog_ref, dtb_ref, ng_ref, o_ref,
                qs, ks, vs, gs, bs, mc_s, n_s, gt_s):
    s_len = q_ref.shape[0]
    c = DN_CHUNK
    n_chunks = s_len // c
    rows = lax.broadcasted_iota(jnp.int32, (s_len, 1), 0)

    def conv(x, w):
        y = x * w[CONV_WIDTH - 1:CONV_WIDTH, :]
        for back in range(1, CONV_WIDTH):
            xs = jnp.where(rows >= back, pltpu.roll(x, back, axis=0), 0.0)
            y = y + xs * w[CONV_WIDTH - 1 - back:CONV_WIDTH - back, :]
        return _silu(y)

    def l2norm(a):
        return a * lax.rsqrt(jnp.sum(a * a, axis=-1, keepdims=True) + NORM_EPS)

    lanes = [slice(j * LANES, (j + 1) * LANES) for j in range(GDN_HEADS)]
    for j in range(GDN_HEADS):
        qs[j] = l2norm(conv(q_ref[:, lanes[j]], wq_ref[:, lanes[j]])) * (D_DK ** -0.5)
        ks[j] = l2norm(conv(k_ref[:, lanes[j]], wk_ref[:, lanes[j]]))
        vs[j] = conv(v_ref[:, lanes[j]], wv_ref[:, lanes[j]])
        gate = gate_ref[j]
        gs[j] = -jnp.exp(alog_ref[j]) * _softplus(gate[:, 0:1] + dtb_ref[j])
        bs[j] = jax.nn.sigmoid(gate[:, 1:2])

    sc = GDN_BATCH * c
    ii = lax.broadcasted_iota(jnp.int32, (sc, sc), 0)
    jj = lax.broadcasted_iota(jnp.int32, (sc, sc), 1)
    same = (ii // c) == (jj // c)
    causal, strict, upper, eye = same & (ii >= jj), same & (ii > jj), same & (ii <= jj), ii == jj
    chunk_end = same & (jj % c == c - 1)
    ng = ng_ref[...]

    def load(n, j):
        r0 = pl.multiple_of(n * sc, sc)
        return (qs[j, pl.ds(r0, sc), :], ks[j, pl.ds(r0, sc), :], vs[j, pl.ds(r0, sc), :],
                gs[j, pl.ds(r0, sc), :], bs[j, pl.ds(r0, sc), :])

    def prepare(q, k, v, g, beta):
        g_row = jnp.sum(jnp.where(eye, g, 0.0), axis=0, keepdims=True)
        gc_col = jnp.sum(jnp.where(causal, g_row, 0.0), axis=1, keepdims=True)
        gc_row = jnp.sum(jnp.where(upper, g, 0.0), axis=0, keepdims=True)
        g_last = jnp.sum(jnp.where(chunk_end, gc_row, 0.0), axis=1, keepdims=True)
        decay = jnp.where(causal, jnp.exp(jnp.where(causal, gc_col - gc_row, 0.0)), 0.0)
        eg = jnp.exp(gc_col)
        kb = k * beta
        kbf = k.astype(BF16)
        raw = _dot_nt(kb.astype(BF16), kbf)
        raw_qk = _dot_nt(q.astype(BF16), kbf)
        rhs = jnp.concatenate([v * beta, kb * eg], axis=1).astype(BF16)
        kd = k * jnp.exp(g_last - gc_col)
        kd_t = [kd[t * c:(t + 1) * c].T.astype(BF16) for t in range(GDN_BATCH)]
        g_tot = [jnp.exp(g_last[t * c:t * c + 1]) for t in range(GDN_BATCH)]
        return raw, raw_qk, decay, rhs, kd_t, g_tot, q * eg

    def solve(prepared):
        n = len(prepared)
        decay = [p[2] for p in prepared]
        x = [-jnp.where(strict, prepared[i][0] * decay[i], 0.0) for i in range(n)]
        inv = [jnp.where(eye, 1.0, 0.0) + x[i] for i in range(n)]
        xb = [x[i].astype(BF16) for i in range(n)]
        x = [_dot(xb[i], xb[i]) for i in range(n)]
        for _ in range(int(math.log2(c)) - 2):
            xb = [x[i].astype(BF16) for i in range(n)]
            both = [_dot(jnp.concatenate([xb[i], inv[i].astype(BF16)], axis=0), xb[i]) for i in range(n)]
            x = [both[i][:sc] for i in range(n)]
            inv = [inv[i] + both[i][sc:] for i in range(n)]
        last = [_dot(inv[i].astype(BF16), x[i].astype(BF16)) for i in range(n)]
        inv = [inv[i] + last[i] for i in range(n)]
        sol = [_dot(inv[i].astype(BF16), prepared[i][3]).astype(BF16) for i in range(n)]
        qk = [(prepared[i][1] * decay[i]).astype(BF16) for i in range(n)]
        qo = [_dot(qk[i], sol[i]) for i in range(n)]
        mn = [[_dot(prepared[i][4][t], sol[i][t * c:(t + 1) * c]) for t in range(GDN_BATCH)] for i in range(n)]
        return [(qo[i], prepared[i][6], mn[i], prepared[i][5]) for i in range(n)]

    def store(n, j, qo, q_dec, mn, g_tot):
        r0 = pl.multiple_of(n * sc, sc)
        o_ref[pl.ds(r0, sc), lanes[j]] = qo[:, :D_DV]
        qs[j, pl.ds(r0, sc), :] = q_dec - qo[:, D_DV:]
        for t in range(GDN_BATCH):
            n_s[j, n * GDN_BATCH + t] = mn[t][:, :D_DV]
            mc_s[j, n * GDN_BATCH + t] = mn[t][:, D_DV:].astype(BF16)
            gt_s[j, n * GDN_BATCH + t] = jnp.broadcast_to(g_tot[t], (8, LANES))

    def prepare_some(i, carry):
        items = [(i * GDN_UNROLL + u, j) for u in range(GDN_UNROLL) for j in range(GDN_HEADS)]
        results = solve([prepare(*operands) for operands in [load(n, j) for n, j in items]])
        for (n, j), res in zip(items, results):
            store(n, j, *res)
        return carry

    lax.fori_loop(0, n_chunks // (GDN_BATCH * GDN_UNROLL), prepare_some, 0)

    def advance(n, states):
        r0 = pl.multiple_of(n * c, c)
        operands = [(o_ref[pl.ds(r0, c), lanes[j]], qs[j, pl.ds(r0, c), :], gt_s[j, n], mc_s[j, n], n_s[j, n])
                    for j in range(GDN_HEADS)]
        out = []
        for j, (o0, q_eff, g_tot, mc, nn) in enumerate(operands):
            sb = states[j].astype(BF16)
            o = o0 + _dot(q_eff.astype(BF16), sb)
            out.append(states[j] * g_tot[0:1, :] - _dot(mc, sb) + nn)
            operands[j] = o * lax.rsqrt(jnp.mean(o * o, axis=-1, keepdims=True) + NORM_EPS) * ng
        for j in range(GDN_HEADS):
            o_ref[pl.ds(r0, c), lanes[j]] = operands[j]
        return tuple(out)

    lax.fori_loop(0, n_chunks, advance, tuple(jnp.zeros((D_DK, D_DV), F32) for _ in range(GDN_HEADS)))


def _gated_deltanet(qkv, gates, conv_w, a_log, dt_bias, norm_g):
    b, s, _ = qkv.shape
    hp, width = GDN_HEADS, GDN_HEADS * LANES
    groups = D_HEADS // hp
    n_chunks = s // DN_CHUNK
    assert n_chunks % (GDN_BATCH * GDN_UNROLL) == 0
    col = lambda off: pl.BlockSpec((None, s, width), lambda i, h: (i, 0, off + h))
    wcol = lambda off: pl.BlockSpec((CONV_WIDTH, width), lambda i, h: (0, off + h))
    scalar = pl.BlockSpec((hp, 1, 1), lambda i, h: (h, 0, 0))
    return pl.pallas_call(
        _gdn_kernel, grid=(b, groups),
        in_specs=[col(0), col(groups), col(2 * groups), wcol(0), wcol(groups), wcol(2 * groups),
                  pl.BlockSpec((None, hp, s, 2), lambda i, h: (i, h, 0, 0)), scalar, scalar,
                  pl.BlockSpec((1, D_DV), lambda i, h: (0, 0))],
        out_specs=pl.BlockSpec((None, s, width), lambda i, h: (i, 0, h)),
        out_shape=jax.ShapeDtypeStruct((b, s, D_HEADS * D_DV), F32),
        scratch_shapes=[pltpu.VMEM((hp, s, LANES), F32)] * 3 + [pltpu.VMEM((hp, s, 1), F32)] * 2
        + [pltpu.VMEM((hp, n_chunks, D_DK, D_DV), BF16), pltpu.VMEM((hp, n_chunks, D_DK, D_DV), F32),
           pltpu.VMEM((hp, n_chunks, 8, LANES), F32)],
        compiler_params=_cparams(("parallel", "parallel")), name="gated_deltanet",
    )(qkv, qkv, qkv, conv_w, conv_w, conv_w, gates,
      a_log.astype(F32).reshape(D_HEADS, 1, 1), dt_bias.astype(F32).reshape(D_HEADS, 1, 1),
      norm_g.astype(F32).reshape(1, D_DV))


def _out_kernel(*refs, odd, final):
    x_ref, mg_ref, gate_ref, w_ref = refs[:4]
    o_ref = refs[-1]
    rest = list(refs[4:-1])
    fin_ref = rest.pop() if final else None
    half = w_ref.shape[0] // 2
    sg = _silu(gate_ref[...].astype(F32))
    if odd:
        cmp_ref, slc_ref, win_ref, od_ref, small_ref = rest
        lane = lax.broadcasted_iota(jnp.int32, (1, LANES), 1)
        bg = jax.nn.sigmoid(small_ref[...])
        blocks = []
        for p_blk in range(C_HPG):
            sl = slice(p_blk * LANES, (p_blk + 1) * LANES)
            acc = 0.0
            for br, ref in enumerate((cmp_ref, slc_ref, win_ref)):
                ca = 2 * D_HEADS + p_blk * N_BRANCH + br
                cb = 2 * D_HEADS + (p_blk + C_HPG) * N_BRANCH + br
                acc = acc + jnp.where(lane < HEAD_DIM, bg[:, ca:ca + 1], bg[:, cb:cb + 1]) * ref[:, sl]
            blocks.append(acc)
        first = jnp.concatenate(blocks, axis=1)
        second = od_ref[...]
    else:
        first, second = rest[0][...], rest[1][...]
    y = (_dot((first * sg[:, :half]).astype(BF16), w_ref[:half, :])
         + _dot((second * sg[:, half:]).astype(BF16), w_ref[half:, :]))
    out = x_ref[...] + mg_ref[...] * y
    if final:
        out = out * lax.rsqrt(jnp.mean(out * out, axis=-1, keepdims=True) + NORM_EPS) * fin_ref[...]
    o_ref[...] = out


def _out_projection(x, mod_l, gate, w, branches, final_g=None):
    b, s, d = x.shape
    ts = min(TOKEN_TILE, s)
    odd = len(branches) > 2
    row = lambda width: pl.BlockSpec((None, ts, width), lambda i, j: (i, j, 0))
    in_specs = [row(d), pl.BlockSpec((None, None, 1, d), lambda i, j: (i, 2, 0, 0)), row(gate.shape[-1]),
                pl.BlockSpec(w.shape, lambda i, j: (0, 0), pipeline_mode=pl.Buffered(1))]
    in_specs += [row(a.shape[-1]) for a in branches]
    args = [x, mod_l, gate, w, *branches]
    if final_g is not None:
        in_specs.append(pl.BlockSpec((1, d), lambda i, j: (0, 0)))
        args.append(final_g.reshape(1, d))
    return pl.pallas_call(
        functools.partial(_out_kernel, odd=odd, final=final_g is not None), grid=(b, s // ts),
        in_specs=in_specs, out_specs=row(d), out_shape=jax.ShapeDtypeStruct((b, s, d), F32),
        compiler_params=_cparams(("parallel", "parallel")), name="gated_out_proj",
    )(*args)


def _pair_cols(a0, b0):
    a, bb = np.arange(a0, a0 + HEAD_DIM), np.arange(b0, b0 + HEAD_DIM)
    return np.concatenate([a[:HALF], bb[:HALF], a[HALF:], bb[HALF:]])


def _paired_head_order(width):
    pairs = [np.concatenate([np.arange(p * HEAD_DIM, (p + 1) * HEAD_DIM),
                             np.arange((p + C_HPG) * HEAD_DIM, (p + C_HPG + 1) * HEAD_DIM)]) for p in range(C_HPG)]
    return np.concatenate(pairs + [np.arange(C_HEADS * HEAD_DIM, width)])


def _even_layout():
    aq, ak, av = 0, 512, 1024
    bq, bk, bv, bf, gate = 1536, 2048, 2560, 3072, 3080
    zero = gate + 1024
    cols = [_pair_cols(ak + 2 * h * HEAD_DIM, ak + (2 * h + 1) * HEAD_DIM) for h in range(A_HEADS)]
    cols.append(np.arange(bk, bk + 512))
    cols += [np.concatenate([np.repeat(np.arange(bf + 2 * p, bf + 2 * p + 2), BIAS_PIECES),
                             np.full(LANES - 2 * BIAS_PIECES, zero)]) for p in range(B_HEADS // 2)]
    cols.append(np.arange(gate, gate + 1024))
    segs = (("rope", 0, 512), ("bf16", 512, 1024), ("f32", 1024, 1536), ("bf16", 1536, 2560))
    rows = [_pair_cols(aq + 2 * h * HEAD_DIM, aq + (2 * h + 1) * HEAD_DIM) for h in range(A_HEADS)]
    rows += [np.arange(bq, bq + 512), np.arange(av, av + 512), np.arange(bv, bv + 512)]
    tsegs = (("rope", 0, 512), ("bf16", 512, 2048))
    return np.concatenate(cols), segs, np.concatenate(rows), tsegs


def _odd_layout():
    cq, kc, vc, ks, vs, kw, vw, cg = 0, 512, 640, 768, 896, 1024, 1152, 1280
    dq, da, db, gate = 1304, 2840, 2844, 2848
    zero = gate + 1024
    cols = [_pair_cols(ks, ks + HEAD_DIM), _pair_cols(kw, kw + HEAD_DIM), np.arange(kc, kc + 256)]
    small = np.concatenate([np.arange(da, da + 2 * D_HEADS), np.arange(cg, cg + C_HEADS * N_BRANCH)])
    cols.append(np.concatenate([small, np.full(LANES - small.size, zero)]))
    cols += [np.arange(dq, dq + 1536), gate + _paired_head_order(1024)]
    segs = (("rope", 0, 256), ("f32", 256, 512), ("f32", 512, 640), ("f32", 640, 2176), ("bf16", 2176, 3200))
    rows = [_pair_cols(cq + p * HEAD_DIM, cq + (p + C_HPG) * HEAD_DIM) for p in range(C_HPG)]
    rows += [np.arange(vs, vs + LANES), np.arange(vw, vw + LANES)]
    tsegs = (("rope+raw", 0, 512), ("bf16", 512, 768))
    return np.concatenate(cols), segs, np.concatenate(rows), tsegs


def _layout_weights(w, cols, rows, n_query_rows):
    w = jnp.concatenate([w, jnp.zeros((w.shape[0], 1), w.dtype)], axis=1)
    scale = jnp.where(jnp.arange(rows.size) < n_query_rows, QK_SCALE, 1.0).astype(w.dtype)
    return w[:, cols].astype(BF16), (w[:, rows] * scale).T.astype(BF16)


def _compress_weights(pe, w1, w2, for_keys):
    half = L_CMP // 2 * HEAD_DIM
    w1ab = jnp.concatenate([w1[:half], w1[half:]], axis=1)
    w1ab = w1ab.reshape(CMP_STRIDE, HEAD_DIM, 2 * CMP_HIDDEN)
    zeros = jnp.zeros_like(w1ab)
    w1_by_tok = jnp.concatenate([jnp.concatenate([w1ab, zeros], axis=2),
                                 jnp.concatenate([zeros, w1ab], axis=2)], axis=1).astype(BF16)
    w2p = jnp.zeros((C_GROUPS, CMP_HIDDEN, LANES), F32)
    for g in range(C_GROUPS):
        if for_keys:
            w2p = w2p.at[g, :, g * HALF:(g + 1) * HALF].set(w2[:, :HALF])
            w2p = w2p.at[g, :, HEAD_DIM + g * HALF:HEAD_DIM + (g + 1) * HALF].set(w2[:, HALF:])
        else:
            w2p = w2p.at[g, :, g * HEAD_DIM:(g + 1) * HEAD_DIM].set(w2)
    if not for_keys:
        w2p = w2p.transpose(0, 2, 1)
    return [w1_by_tok, w1.astype(F32), pe.astype(F32).reshape(1, L_CMP * HEAD_DIM), w2p.astype(BF16)]


def _even_layer(x, mod_l, rope, layer_idx, g, w_in, b_forget, lq1, lk1, lq2, lk2, subln_g, w_out, final_g):
    cols, segs, rows, tsegs = _even_layout()
    w, wt = _layout_weights(w_in, cols, rows, 2 * A_HEADS * HEAD_DIM + B_HEADS * HEAD_DIM)
    k_a, k_b, forget_logits, gate, qt_a, rest_t = _projection(x, mod_l, g, rope, w, wt, segs, tsegs)
    lam_init = 0.8 - 0.6 * math.exp(-0.3 * layer_idx)
    vec = lambda a: a.astype(F32).reshape(1, -1)
    const = lambda shape: (lambda tq, tk, tile: pl.BlockSpec(shape, lambda i, hb, *t: (0, 0)))
    nb = B_HEADS // 2
    per_step = 2
    oa = _flash("diff", qt_a, k_a, rest_t, A_HEADS, per_step, 0, 0, nb, False,
                extra=[vec(lq1), vec(lk1), vec(lq2), vec(lk2), subln_g.astype(F32).reshape(LANES, 1)],
                extra_specs=[const((1, HEAD_DIM))] * 4 + [const((LANES, 1))], lam_init=lam_init)
    bias = _forget_cumsum(forget_logits, b_forget)
    bias_spec = lambda tq, tk, tile: pl.BlockSpec((None, per_step, tk, LANES),
                                                  lambda i, hb, *t: (i, hb, tile(*t)[1], 0))
    ob = _flash("fox", rest_t, k_b, rest_t, nb, per_step, 0, 0, nb + A_HEADS, False,
                extra=[bias], extra_specs=[bias_spec])
    return _out_projection(x, mod_l, gate, w_out.astype(BF16), [oa, ob], final_g)


def _odd_layer(x, mod_l, rope, g, w_in, pe_k, pe_v, w1_k, w2_k, w1_v, w2_v, conv_w, a_log, dt_bias,
               dn_norm_g, w_out, final_g):
    b, s, _ = x.shape
    cols, segs, rows, tsegs = _odd_layout()
    w, wt = _layout_weights(w_in, cols, rows, C_HEADS * HEAD_DIM)
    k_rot, cmp_in, small, dqkv, gate, q_raw_t, q_rot_t, v_t = _projection(x, mod_l, g, rope, w, wt, segs, tsegs)
    n_cmp = (s - L_CMP) // CMP_STRIDE + 1
    kcmp, vcmp_t = _compress(cmp_in, _compress_weights(pe_k, w1_k, w2_k, True),
                             _compress_weights(pe_v, w1_v, w2_v, False))
    o_cmp, sel = _cmp_select(q_raw_t, kcmp, vcmp_t, n_cmp)
    sel_spec = lambda tq, tk, tile: pl.BlockSpec((None, LANES, tq), lambda i, hb, *t: (i, 0, tile(*t)[0]))
    o_slc = _flash("sel", q_rot_t, k_rot, v_t, C_HPG, C_HPG, 0, 0, 0, True, extra=[sel], extra_specs=[sel_spec])
    o_win = _flash("win", q_rot_t, k_rot, v_t, C_HPG, C_HPG, 0, 1, 1, True)
    gates = small[:, :, :2 * D_HEADS].reshape(b, s, 2, D_HEADS).transpose(0, 3, 1, 2)
    od = _gated_deltanet(dqkv, gates, conv_w.astype(F32), a_log, dt_bias, dn_norm_g)
    rows = _paired_head_order(w_out.shape[0])
    return _out_projection(x, mod_l, gate, w_out[rows].astype(BF16), [o_cmp, o_slc, o_win, od, small], final_g)


def kernel(x, c, positions, norm_g, w_mod, b_mod, w_out, final_norm_g, w_in_even, b_forget, lambda_q1, lambda_k1,
           lambda_q2, lambda_k2, subln_g, w_in_odd, cmp_pe_k, cmp_pe_v, cmp_w1_k, cmp_w2_k, cmp_w1_v, cmp_w2_v,
           conv_w, a_log, dt_bias, dn_norm_g):
    depth = norm_g.shape[0]
    rope = _rope_tables(positions)
    mod = _modulation(c, w_mod, b_mod)
    for l in range(depth):
        final_g = final_norm_g if l == depth - 1 else None
        i = l // 2
        if l % 2 == 0:
            x = _even_layer(x, mod[l], rope, l, norm_g[l], w_in_even[i], b_forget[i], lambda_q1[i],
                            lambda_k1[i], lambda_q2[i], lambda_k2[i], subln_g[i], w_out[l], final_g)
        else:
            x = _odd_layer(x, mod[l], rope, norm_g[l], w_in_odd[i], cmp_pe_k[i], cmp_pe_v[i], cmp_w1_k[i],
                           cmp_w2_k[i], cmp_w1_v[i], cmp_w2_v[i], conv_w[i], a_log[i], dt_bias[i], dn_norm_g[i],
                           w_out[l], final_g)
    return x
```

```python
import functools
import math

import jax
import jax.numpy as jnp
import numpy as np
from jax import lax
from jax.experimental import pallas as pl
from jax.experimental.pallas import tpu as pltpu

F32 = jnp.float32
BF16 = jnp.bfloat16
HI = lax.Precision.HIGHEST

LANES = 128
HEAD_DIM = 64
HALF = HEAD_DIM // 2
ROPE_THETA = 10000.0
NORM_EPS = 1e-6
NEG = -1e30
MASK_BIG = 1e30
LOG2E = math.log2(math.e)
QK_SCALE = HEAD_DIM ** -0.5 * LOG2E
BIAS_PIECES = 3
NORM_ROWS = 16
A_HEADS = 4
B_HEADS = 8
C_HEADS = 8
C_GROUPS = 2
C_HPG = C_HEADS // C_GROUPS
L_CMP = 32
CMP_STRIDE = 16
CMP_HIDDEN = 256
L_SEL = 64
N_SEL = 8
WINDOW = 512
N_BRANCH = 3
FORCE_BONUS = 1e4
D_HEADS = 4
D_DK = 128
D_DV = 128
CONV_WIDTH = 4
DN_CHUNK = 64
TOKEN_TILE = 512
VMEM_LIMIT = 56 * 1024 * 1024


def _cparams(sem):
    return pltpu.CompilerParams(dimension_semantics=sem, vmem_limit_bytes=VMEM_LIMIT)


def _dot(a, b, precision=None):
    return jnp.dot(a, b, precision=precision, preferred_element_type=F32)


def _dot_nt(a, b, precision=None):
    return lax.dot_general(a, b, (((1,), (1,)), ((), ())), precision=precision, preferred_element_type=F32)


def _dot_tn(a, b, precision=None):
    return lax.dot_general(a, b, (((0,), (0,)), ((), ())), precision=precision, preferred_element_type=F32)


def _split_bf16(a):
    hi = a.astype(BF16)
    return hi, (a - hi.astype(F32)).astype(BF16)


def _softplus(z):
    return jnp.maximum(z, 0.0) + jnp.log1p(jnp.exp(-jnp.abs(z)))


def _silu(z):
    return z * jax.nn.sigmoid(z)


def _rope_table_kernel(pos_ref, inv_ref, cos_ref, sin_ref, cos_t_ref, sin_t_ref):
    ang = pos_ref[...].astype(F32) * inv_ref[...]
    lane = lax.broadcasted_iota(jnp.int32, (1, LANES), 1)
    cos = jnp.cos(ang)
    sin = jnp.where(lane < 2 * HALF, -1.0, 1.0) * jnp.sin(ang)
    cos_ref[...] = cos
    sin_ref[...] = sin
    cos_t_ref[...] = cos.T
    sin_t_ref[...] = sin.T


def _rope_tables(positions):
    b, s = positions.shape
    inv = ROPE_THETA ** (-jnp.arange(0, HEAD_DIM, 2, dtype=F32) / HEAD_DIM)
    inv = jnp.tile(inv, 4).reshape(1, LANES)
    tok = jax.ShapeDtypeStruct((b, s, LANES), F32)
    feat = jax.ShapeDtypeStruct((b, LANES, s), F32)
    return pl.pallas_call(
        _rope_table_kernel, grid=(b,),
        in_specs=[pl.BlockSpec((None, s, 1), lambda i: (i, 0, 0)),
                  pl.BlockSpec((1, LANES), lambda i: (0, 0))],
        out_specs=[pl.BlockSpec((None, s, LANES), lambda i: (i, 0, 0))] * 2
        + [pl.BlockSpec((None, LANES, s), lambda i: (i, 0, 0))] * 2,
        out_shape=[tok, tok, feat, feat], compiler_params=_cparams(("parallel",)), name="rope_tables",
    )(positions.reshape(b, s, 1), inv)


def _mod_kernel(c_ref, w_ref, b_ref, o_ref):
    o_ref[...] = _dot(_silu(c_ref[...]), w_ref[...], HI) + b_ref[...]


def _modulation(c, w_mod, b_mod):
    depth, d, n = w_mod.shape
    b = c.shape[0]
    tn = 1024
    mod = pl.pallas_call(
        _mod_kernel, grid=(depth, n // tn),
        in_specs=[pl.BlockSpec((b, d), lambda l, j: (0, 0)),
                  pl.BlockSpec((None, d, tn), lambda l, j: (l, 0, j)),
                  pl.BlockSpec((None, 1, tn), lambda l, j: (l, 0, j))],
        out_specs=pl.BlockSpec((None, b, tn), lambda l, j: (l, 0, j)),
        out_shape=jax.ShapeDtypeStruct((depth, b, n), F32),
        compiler_params=_cparams(("parallel", "parallel")), name="modulation",
    )(c, w_mod, b_mod.reshape(depth, 1, n))
    return mod.reshape(depth, b, 3, 1, d)


def _proj_kernel(x_ref, shift_ref, scale_ref, g_ref, cos_ref, sin_ref, cos_t_ref, sin_t_ref, w_ref, wt_ref,
                 *out_refs, segs, tsegs):
    x = x_ref[...]
    h = x * lax.rsqrt(jnp.mean(x * x, axis=-1, keepdims=True) + NORM_EPS) * g_ref[...]
    h = (h * (1.0 + scale_ref[...]) + shift_ref[...]).astype(BF16)
    outs = list(out_refs)
    step = 4 * LANES
    for kind, c0, c1 in segs:
        o_ref = outs.pop(0)
        for a in range(c0, c1, step):
            e = min(a + step, c1)
            acc = _dot(h, w_ref[:, a:e])
            if kind == "rope":
                cos, sin = cos_ref[...], sin_ref[...]
                for j in range(0, e - a, LANES):
                    blk = acc[:, j:j + LANES]
                    rot = blk * cos + pltpu.roll(blk, 2 * HALF, axis=1) * sin
                    o_ref[:, a - c0 + j:a - c0 + j + LANES] = rot.astype(o_ref.dtype)
            else:
                o_ref[:, a - c0:e - c0] = acc.astype(o_ref.dtype)
    for kind, r0, r1 in tsegs:
        raw_ref = outs.pop(0) if kind == "rope+raw" else None
        o_ref = outs.pop(0)
        for a in range(r0, r1, step):
            e = min(a + step, r1)
            acc = _dot_nt(wt_ref[a:e, :], h)
            if raw_ref is not None:
                raw_ref[a - r0:e - r0, :] = acc
            if kind in ("rope", "rope+raw"):
                cos, sin = cos_t_ref[...], sin_t_ref[...]
                for j in range(0, e - a, LANES):
                    blk = acc[j:j + LANES, :]
                    rot = blk * cos + pltpu.roll(blk, 2 * HALF, axis=0) * sin
                    o_ref[a - r0 + j:a - r0 + j + LANES, :] = rot.astype(o_ref.dtype)
            else:
                o_ref[a - r0:e - r0, :] = acc.astype(o_ref.dtype)


def _projection(x, mod_l, g, rope, w, wt, segs, tsegs):
    b, s, d = x.shape
    ts = min(TOKEN_TILE, s)
    row = lambda width: pl.BlockSpec((None, ts, width), lambda i, j: (i, j, 0))
    col = lambda height: pl.BlockSpec((None, height, ts), lambda i, j: (i, 0, j))
    out_specs, out_shapes = [], []
    for kind, c0, c1 in segs:
        out_specs.append(row(c1 - c0))
        out_shapes.append(jax.ShapeDtypeStruct((b, s, c1 - c0), F32 if kind == "f32" else BF16))
    for kind, r0, r1 in tsegs:
        if kind == "rope+raw":
            out_specs.append(col(r1 - r0))
            out_shapes.append(jax.ShapeDtypeStruct((b, r1 - r0, s), F32))
        out_specs.append(col(r1 - r0))
        out_shapes.append(jax.ShapeDtypeStruct((b, r1 - r0, s), BF16))
    modspec = lambda k: pl.BlockSpec((None, None, 1, d), lambda i, j: (i, k, 0, 0))
    resident = lambda a: pl.BlockSpec(a.shape, lambda i, j: (0, 0), pipeline_mode=pl.Buffered(1))
    return pl.pallas_call(
        functools.partial(_proj_kernel, segs=segs, tsegs=tsegs), grid=(b, s // ts),
        in_specs=[row(d), modspec(0), modspec(1), pl.BlockSpec((1, d), lambda i, j: (0, 0)),
                  row(LANES), row(LANES), col(LANES), col(LANES), resident(w), resident(wt)],
        out_specs=out_specs, out_shape=out_shapes,
        compiler_params=_cparams(("parallel", "parallel")), name="adaln_in_proj",
    )(x, mod_l, mod_l, g.reshape(1, d), *rope, w, wt)


def _cum_kernel(x_ref, bias_ref, o_ref):
    s = x_ref.shape[0]
    ii = lax.broadcasted_iota(jnp.int32, (LANES, LANES), 0)
    jj = lax.broadcasted_iota(jnp.int32, (LANES, LANES), 1)
    lower = (ii >= jj).astype(F32)
    lane = lax.broadcasted_iota(jnp.int32, (1, LANES), 1)
    carry = jnp.zeros((1, LANES), F32)
    for r0 in range(0, s, LANES):
        z = x_ref[r0:r0 + LANES, :] + bias_ref[...]
        logf = jnp.minimum(z, 0.0) - jnp.log1p(jnp.exp(-jnp.abs(z)))
        loc = _dot(lower, logf, HI) + carry
        carry = loc[LANES - 1:LANES, :]
        val = loc * (-LOG2E)
        hi = val.astype(BF16)
        rest = val - hi.astype(F32)
        mid = rest.astype(BF16)
        lo = (rest - mid.astype(F32)).astype(BF16)
        piece = jnp.where(lane % BIAS_PIECES == 0, hi, jnp.where(lane % BIAS_PIECES == 1, mid, lo))
        o_ref[r0:r0 + LANES, :] = jnp.where(lane < 2 * BIAS_PIECES, piece, jnp.zeros_like(piece))


def _forget_cumsum(logits, b_forget):
    b, s, width = logits.shape
    pairs = width // LANES
    per_lane = jnp.repeat(b_forget.astype(F32).reshape(pairs, 2), BIAS_PIECES, axis=1)
    bias = jnp.zeros((pairs, 1, LANES), F32).at[:, 0, :2 * BIAS_PIECES].set(per_lane)
    return pl.pallas_call(
        _cum_kernel, grid=(b, pairs),
        in_specs=[pl.BlockSpec((None, s, LANES), lambda i, p: (i, 0, p)),
                  pl.BlockSpec((None, 1, LANES), lambda i, p: (p, 0, 0))],
        out_specs=pl.BlockSpec((None, None, s, LANES), lambda i, p: (i, p, 0, 0)),
        out_shape=jax.ShapeDtypeStruct((b, pairs, s, LANES), BF16),
        compiler_params=_cparams(("parallel", "parallel")), name="forget_cumsum",
    )(logits, bias)


def _triangle_step(t, nq):
    qi = sum((t >= j * (j + 1) // 2).astype(jnp.int32) for j in range(1, nq))
    return qi, t - qi * (qi + 1) // 2


def _flash_kernel(*refs, mode, tq, tk, nq, nblk, shared_kv, lam_init):
    qt_ref, k_ref, vt_ref = refs[:3]
    m_ref, acc_ref = refs[-2:]
    o_ref = refs[-3]
    extra = refs[3:-3]
    if mode == "win":
        qi, ki = pl.program_id(2), pl.program_id(3)
        kv, last = qi - 1 + ki, ki == pl.num_programs(3) - 1
    else:
        qi, ki = _triangle_step(pl.program_id(2), nq)
        kv, last = ki, ki == qi
    row = lax.broadcasted_iota(jnp.int32, (LANES, 1), 0)
    if mode == "fox":
        slot_rows = (row < HEAD_DIM, row >= HEAD_DIM)
    else:
        slot_rows = ((row % HEAD_DIM) < HALF, (row % HEAD_DIM) >= HALF)
    vrows = acc_ref.shape[1] - NORM_ROWS
    block = lambda j: slice(j * LANES, (j + 1) * LANES)
    half = tq // 2

    @pl.when(ki == 0)
    def _():
        m_ref[...] = jnp.full(m_ref.shape, NEG, F32)
        acc_ref[...] = jnp.zeros(acc_ref.shape, F32)

    def step(kind):
        rr = lax.broadcasted_iota(jnp.int32, (LANES, tq), 0)
        q_extra, k_extra = [None, None], [None, None]
        for s in range(2):
            if mode == "fox":
                q_extra[s] = jnp.where((rr >= BIAS_PIECES * s) & (rr < BIAS_PIECES * (s + 1)), 1.0, 0.0).astype(BF16)
            if mode == "sel":
                cb = (kv * tk + lax.broadcasted_iota(jnp.int32, (tk, LANES), 0)) // L_SEL
                ll = lax.broadcasted_iota(jnp.int32, (tk, LANES), 1)
                k_extra[s] = jnp.where(ll == cb + s * (LANES // 4), MASK_BIG, 0.0).astype(BF16)
                q_extra[s] = extra[0][...]
        lo, hi, everything = slice(0, half), slice(half, tq), slice(0, tq)
        parts = {"full": [(everything, everything, None)],
                 "lower": [(lo, lo, lo), (everything, hi, hi)],
                 "upper": [(everything, lo, lo), (hi, hi, hi)]}[kind]
        tri_k = lax.broadcasted_iota(jnp.int32, (half, half), 0)
        tri_q = lax.broadcasted_iota(jnp.int32, (half, half), 1)
        tri = tri_k <= tri_q if kind == "lower" else tri_k > tri_q
        ones = jnp.ones((NORM_ROWS, tk), BF16)
        sts, vts = [], []
        for j in range(nblk):
            qt = qt_ref[block(j), :]
            k = k_ref[...] if shared_kv else k_ref[:, block(j)]
            base = 0 if shared_kv else j * LANES
            for s in range(2):
                qs = jnp.where(slot_rows[s], qt, jnp.zeros_like(qt))
                if q_extra[s] is not None:
                    k_more = extra[0][j] if mode == "fox" else k_extra[s]
                    k_all, q_all = jnp.concatenate([k, k_more], axis=1), jnp.concatenate([qs, q_extra[s]], axis=0)
                else:
                    k_all, q_all = k, qs
                sts.append([_dot(k_all[krows], q_all[:, lanes]) for krows, lanes, _ in parts])
                v0 = base if vrows == LANES else base + s * vrows
                vts.append(jnp.concatenate([vt_ref[v0:v0 + vrows, :], ones], axis=0))
        for idx in range(2 * nblk):
            for (krows, lanes, tri_rows), st in zip(parts, sts[idx]):
                if tri_rows is not None:
                    if krows == tri_rows:
                        st = jnp.where(tri, st, NEG)
                    elif tri_rows == lo:
                        st = jnp.concatenate([jnp.where(tri, st[:half], NEG), st[half:]], axis=0)
                    else:
                        st = jnp.concatenate([st[:half], jnp.where(tri, st[half:], NEG)], axis=0)
                m_prev = m_ref[idx, :, lanes]
                m_new = jnp.maximum(m_prev, jnp.max(st, axis=0, keepdims=True))
                alpha = jnp.exp2(m_prev - m_new)
                p = jnp.exp2(st - m_new).astype(BF16)
                acc_ref[idx, :, lanes] = alpha * acc_ref[idx, :, lanes] + _dot(vts[idx][:, krows], p)
                m_ref[idx, :, lanes] = m_new

    if mode == "win":
        pl.when((ki == 0) & (kv >= 0))(functools.partial(step, "upper"))
        pl.when(ki == 1)(functools.partial(step, "lower"))
    else:
        pl.when(ki < qi)(functools.partial(step, "full"))
        pl.when(ki == qi)(functools.partial(step, "lower"))

    @pl.when(last)
    def _():
        for j in range(nblk):
            o0 = acc_ref[2 * j, :vrows, :] / acc_ref[2 * j, vrows:vrows + 1, :]
            o1 = acc_ref[2 * j + 1, :vrows, :] / acc_ref[2 * j + 1, vrows:vrows + 1, :]
            if mode == "diff":
                lq1, lk1, lq2, lk2, subg = (r[...] for r in extra)
                lam = (jnp.exp(jnp.sum(lq1 * lk1, axis=1, keepdims=True))
                       - jnp.exp(jnp.sum(lq2 * lk2, axis=1, keepdims=True)) + lam_init)
                o = o0 - lam * o1
                o = o * lax.rsqrt(jnp.mean(o * o, axis=0, keepdims=True) + NORM_EPS) * subg * (1.0 - lam_init)
            else:
                o = jnp.concatenate([o0, o1], axis=0)
            o_ref[:, block(j)] = o.T.astype(o_ref.dtype)


def _flash(mode, qt, k, vt, n_qblocks, nblk, q0, k0, v0, shared_kv, extra=(), extra_specs=(), lam_init=0.0):
    b, s, _ = k.shape
    tq = tk = min(TOKEN_TILE, s)
    nq = s // tq
    if mode == "win":
        assert WINDOW == tk
        steps, sem = (nq, 2), ("parallel", "arbitrary")
        tile = lambda qi, ki: (qi, jnp.maximum(qi - 1 + ki, 0))
    else:
        steps, sem = (nq * (nq + 1) // 2,), ("arbitrary",)
        tile = lambda t: _triangle_step(t, nq)
    assert n_qblocks % nblk == 0 and q0 % nblk == 0 and (shared_kv or (k0 % nblk == 0 and v0 % nblk == 0))
    wide = nblk * LANES
    if shared_kv:
        k_spec = pl.BlockSpec((None, tk, LANES), lambda i, hb, *t: (i, tile(*t)[1], k0))
        v_spec = pl.BlockSpec((None, LANES, tk), lambda i, hb, *t: (i, v0, tile(*t)[1]))
    else:
        k_spec = pl.BlockSpec((None, tk, wide), lambda i, hb, *t: (i, tile(*t)[1], k0 // nblk + hb))
        v_spec = pl.BlockSpec((None, wide, tk), lambda i, hb, *t: (i, v0 // nblk + hb, tile(*t)[1]))
    in_specs = [pl.BlockSpec((None, wide, tq), lambda i, hb, *t: (i, q0 // nblk + hb, tile(*t)[0])), k_spec, v_spec]
    for spec in extra_specs:
        in_specs.append(spec(tq, tk, tile))
    vrows = LANES if mode == "diff" else HEAD_DIM
    return pl.pallas_call(
        functools.partial(_flash_kernel, mode=mode, tq=tq, tk=tk, nq=nq, nblk=nblk, shared_kv=shared_kv,
                          lam_init=lam_init),
        grid=(b, n_qblocks // nblk) + steps, in_specs=in_specs,
        out_specs=pl.BlockSpec((None, tq, wide), lambda i, hb, *t: (i, tile(*t)[0], hb)),
        out_shape=jax.ShapeDtypeStruct((b, s, n_qblocks * LANES), BF16),
        scratch_shapes=[pltpu.VMEM((2 * nblk, 1, tq), F32), pltpu.VMEM((2 * nblk, vrows + NORM_ROWS, tq), F32)],
        compiler_params=_cparams(("parallel", "parallel") + sem),
        name="flash_" + mode,
    )(qt, k, vt, *extra)


def _compress_kernel(xk_ref, xv_ref, w1_by_tok_k, w1_k, pe_k, w2_k, w1_by_tok_v, w1_v, pe_v, w2_v, ok_ref, ov_ref):
    cn = xk_ref.shape[0] // CMP_STRIDE

    def run(x_ref, w1_by_tok, w1, pe, w2, o_ref, transposed):
        pe_term = _dot(pe[...], w1[...], HI)
        ab = jnp.zeros((cn, C_GROUPS * 2 * CMP_HIDDEN), F32)
        for tok in range(CMP_STRIDE):
            rows = x_ref[pl.ds(tok, cn, stride=CMP_STRIDE), :]
            ab = ab + _dot(rows.astype(BF16), w1_by_tok[tok])
        out = 0.0
        for g in range(C_GROUPS):
            first = ab[:, g * 2 * CMP_HIDDEN:(g * 2 + 1) * CMP_HIDDEN]
            second = ab[:, (g * 2 + 1) * CMP_HIDDEN:(g + 1) * 2 * CMP_HIDDEN]
            hid = first + pltpu.roll(second, cn - 1, axis=0) + pe_term
            act = _silu(hid).astype(BF16)
            out = out + (_dot_nt(w2[g], act) if transposed else _dot(act, w2[g]))
        o_ref[...] = out
    run(xk_ref, w1_by_tok_k, w1_k, pe_k, w2_k, ok_ref, False)
    run(xv_ref, w1_by_tok_v, w1_v, pe_v, w2_v, ov_ref, True)


def _compress(cmp_in, weights_k, weights_v):
    b, s, width = cmp_in.shape
    cn = s // CMP_STRIDE
    full = lambda a: pl.BlockSpec(a.shape, lambda i: (0,) * a.ndim)
    return pl.pallas_call(
        _compress_kernel, grid=(b,),
        in_specs=[pl.BlockSpec((None, s, LANES), lambda i: (i, 0, 0)), pl.BlockSpec((None, s, LANES), lambda i: (i, 0, 1))]
        + [full(a) for a in weights_k + weights_v],
        out_specs=[pl.BlockSpec((None, cn, LANES), lambda i: (i, 0, 0)),
                   pl.BlockSpec((None, LANES, cn), lambda i: (i, 0, 0))],
        out_shape=[jax.ShapeDtypeStruct((b, cn, LANES), F32), jax.ShapeDtypeStruct((b, LANES, cn), F32)],
        compiler_params=_cparams(("parallel",)), name="nsa_compress",
    )(cmp_in, cmp_in, *weights_k, *weights_v)


def _cmp_select_kernel(qt_ref, kc_ref, vct_ref, o_ref, sel_ref, *, tq, n_blk, n_cmp):
    qi = pl.program_id(1)
    cn = kc_ref.shape[0]
    kc_hi, kc_lo = _split_bf16(kc_ref[...])
    vct = vct_ref[...].astype(BF16)
    row = lax.broadcasted_iota(jnp.int32, (LANES, 1), 0)
    slot_rows = ((row % HEAD_DIM) < HALF, (row % HEAD_DIM) >= HALF)
    t_row = qi * tq + lax.broadcasted_iota(jnp.int32, (1, tq), 1)
    m_col = lax.broadcasted_iota(jnp.int32, (cn, 1), 0)
    valid = (m_col * CMP_STRIDE + L_CMP - 1 <= t_row) & (m_col < n_cmp)
    any_valid = (t_row >= L_CMP - 1).astype(F32)
    psum = [jnp.zeros((cn, tq), F32) for _ in range(C_GROUPS)]
    for p_blk in range(C_HPG):
        qt = qt_ref[p_blk * LANES:(p_blk + 1) * LANES, :]
        outs = []
        for s in range(C_GROUPS):
            q_hi, q_lo = _split_bf16(jnp.where(slot_rows[s], qt, 0.0))
            sc = jnp.where(valid, _dot(kc_hi, q_hi) + _dot(kc_lo, q_hi) + _dot(kc_hi, q_lo), NEG)
            e = jnp.exp2(sc - jnp.max(sc, axis=0, keepdims=True))
            p = e / jnp.sum(e, axis=0, keepdims=True) * any_valid
            psum[s] = psum[s] + p
            outs.append(_dot(vct[s * HEAD_DIM:(s + 1) * HEAD_DIM, :], p.astype(BF16)))
        o_ref[:, p_blk * LANES:(p_blk + 1) * LANES] = jnp.concatenate(outs, axis=0).T.astype(o_ref.dtype)
    jb = lax.broadcasted_iota(jnp.int32, (n_blk, cn), 0)
    mm = lax.broadcasted_iota(jnp.int32, (n_blk, cn), 1)
    overlap = ((mm * CMP_STRIDE < jb * L_SEL + L_SEL) & (mm * CMP_STRIDE + L_CMP > jb * L_SEL)
               & (mm < n_cmp)).astype(BF16)
    j = lax.broadcasted_iota(jnp.int32, (n_blk, tq), 0)
    cur = (qi * tq + lax.broadcasted_iota(jnp.int32, (n_blk, tq), 1)) // L_SEL
    forced = (j == 0) | (j == cur) | (j == cur - 1)
    n_top = min(N_SEL, n_blk)
    pad_rows = LANES // 4 - n_blk
    parts = []
    for s in range(C_GROUPS):
        p_hi, p_lo = _split_bf16(psum[s])
        imp = _dot(overlap, p_hi) + _dot(overlap, p_lo)
        score = jnp.where(j > cur, NEG, imp + jnp.where(forced, FORCE_BONUS, 0.0))
        rank = jnp.zeros((n_blk, tq), jnp.int32)
        for jp in range(n_blk):
            r = score[jp:jp + 1, :]
            rank = rank + ((r > score) | ((r == score) & (jp < j))).astype(jnp.int32)
        parts.append(jnp.where(rank < n_top, 0.0, -1.0))
        if pad_rows:
            parts.append(jnp.zeros((pad_rows, tq), F32))
    parts.append(jnp.zeros((LANES // 2, tq), F32))
    sel_ref[...] = jnp.concatenate(parts, axis=0).astype(sel_ref.dtype)


def _cmp_select(q_raw_t, kcmp, vcmp_t, n_cmp):
    b, width, s = q_raw_t.shape
    cn = kcmp.shape[1]
    tq = min(TOKEN_TILE, s)
    n_blk = s // L_SEL
    return pl.pallas_call(
        functools.partial(_cmp_select_kernel, tq=tq, n_blk=n_blk, n_cmp=n_cmp), grid=(b, s // tq),
        in_specs=[pl.BlockSpec((None, width, tq), lambda i, j: (i, 0, j)),
                  pl.BlockSpec((None, cn, LANES), lambda i, j: (i, 0, 0)),
                  pl.BlockSpec((None, LANES, cn), lambda i, j: (i, 0, 0))],
        out_specs=[pl.BlockSpec((None, tq, width), lambda i, j: (i, j, 0)),
                   pl.BlockSpec((None, LANES, tq), lambda i, j: (i, 0, j))],
        out_shape=[jax.ShapeDtypeStruct((b, s, width), BF16), jax.ShapeDtypeStruct((b, LANES, s), BF16)],
        compiler_params=_cparams(("parallel", "parallel")), name="nsa_cmp_select",
    )(q_raw_t, kcmp, vcmp_t)


GDN_HEADS = 2
GDN_BATCH = 4
GDN_UNROLL = 2


def _gdn_kernel(q_ref, k_ref, v_ref, wq_ref, wk_ref, wv_ref, gate_ref, alog_ref, dtb_ref, ng_ref, o_ref,
                qs, ks, vs, gs, bs, mc_s, n_s, gt_s):
    s_len = q_ref.shape[0]
    c = DN_CHUNK
    n_chunks = s_len // c
    rows = lax.broadcasted_iota(jnp.int32, (s_len, 1), 0)

    def conv(x, w):
        y = x * w[CONV_WIDTH - 1:CONV_WIDTH, :]
        for back in range(1, CONV_WIDTH):
            xs = jnp.where(rows >= back, pltpu.roll(x, back, axis=0), 0.0)
            y = y + xs * w[CONV_WIDTH - 1 - back:CONV_WIDTH - back, :]
        return _silu(y)

    def l2norm(a):
        return a * lax.rsqrt(jnp.sum(a * a, axis=-1, keepdims=True) + NORM_EPS)

    lanes = [slice(j * LANES, (j + 1) * LANES) for j in range(GDN_HEADS)]
    for j in range(GDN_HEADS):
        qs[j] = l2norm(conv(q_ref[:, lanes[j]], wq_ref[:, lanes[j]])) * (D_DK ** -0.5)
        ks[j] = l2norm(conv(k_ref[:, lanes[j]], wk_ref[:, lanes[j]]))
        vs[j] = conv(v_ref[:, lanes[j]], wv_ref[:, lanes[j]])
        gate = gate_ref[j]
        gs[j] = -jnp.exp(alog_ref[j]) * _softplus(gate[:, 0:1] + dtb_ref[j])
        bs[j] = jax.nn.sigmoid(gate[:, 1:2])

    sc = GDN_BATCH * c
    ii = lax.broadcasted_iota(jnp.int32, (sc, sc), 0)
    jj = lax.broadcasted_iota(jnp.int32, (sc, sc), 1)
    same = (ii // c) == (jj // c)
    causal, strict, upper, eye = same & (ii >= jj), same & (ii > jj), same & (ii <= jj), ii == jj
    chunk_end = same & (jj % c == c - 1)
    ng = ng_ref[...]

    def load(n, j):
        r0 = pl.multiple_of(n * sc, sc)
        return (qs[j, pl.ds(r0, sc), :], ks[j, pl.ds(r0, sc), :], vs[j, pl.ds(r0, sc), :],
                gs[j, pl.ds(r0, sc), :], bs[j, pl.ds(r0, sc), :])

    def prepare(q, k, v, g, beta):
        g_row = jnp.sum(jnp.where(eye, g, 0.0), axis=0, keepdims=True)
        gc_col = jnp.sum(jnp.where(causal, g_row, 0.0), axis=1, keepdims=True)
        gc_row = jnp.sum(jnp.where(upper, g, 0.0), axis=0, keepdims=True)
        g_last = jnp.sum(jnp.where(chunk_end, gc_row, 0.0), axis=1, keepdims=True)
        decay = jnp.where(causal, jnp.exp(jnp.where(causal, gc_col - gc_row, 0.0)), 0.0)
        eg = jnp.exp(gc_col)
        kb = k * beta
        kbf = k.astype(BF16)
        raw = _dot_nt(kb.astype(BF16), kbf)
        raw_qk = _dot_nt(q.astype(BF16), kbf)
        rhs = jnp.concatenate([v * beta, kb * eg], axis=1).astype(BF16)
        kd = k * jnp.exp(g_last - gc_col)
        kd_t = [kd[t * c:(t + 1) * c].T.astype(BF16) for t in range(GDN_BATCH)]
        g_tot = [jnp.exp(g_last[t * c:t * c + 1]) for t in range(GDN_BATCH)]
        return raw, raw_qk, decay, rhs, kd_t, g_tot, q * eg

    def solve(prepared):
        n = len(prepared)
        decay = [p[2] for p in prepared]
        x = [-jnp.where(strict, prepared[i][0] * decay[i], 0.0) for i in range(n)]
        inv = [jnp.where(eye, 1.0, 0.0) + x[i] for i in range(n)]
        xb = [x[i].astype(BF16) for i in range(n)]
        x = [_dot(xb[i], xb[i]) for i in range(n)]
        for _ in range(int(math.log2(c)) - 2):
            xb = [x[i].astype(BF16) for i in range(n)]
            both = [_dot(jnp.concatenate([xb[i], inv[i].astype(BF16)], axis=0), xb[i]) for i in range(n)]
            x = [both[i][:sc] for i in range(n)]
            inv = [inv[i] + both[i][sc:] for i in range(n)]
        last = [_dot(inv[i].astype(BF16), x[i].astype(BF16)) for i in range(n)]
        inv = [inv[i] + last[i] for i in range(n)]
        sol = [_dot(inv[i].astype(BF16), prepared[i][3]).astype(BF16) for i in range(n)]
        qk = [(prepared[i][1] * decay[i]).astype(BF16) for i in range(n)]
        qo = [_dot(qk[i], sol[i]) for i in range(n)]
        mn = [[_dot(prepared[i][4][t], sol[i][t * c:(t + 1) * c]) for t in range(GDN_BATCH)] for i in range(n)]
        return [(qo[i], prepared[i][6], mn[i], prepared[i][5]) for i in range(n)]

    def store(n, j, qo, q_dec, mn, g_tot):
        r0 = pl.multiple_of(n * sc, sc)
        o_ref[pl.ds(r0, sc), lanes[j]] = qo[:, :D_DV]
        q_eff = (q_dec - qo[:, D_DV:]).astype(BF16)
        for t in range(GDN_BATCH):
            n_s[j, n * GDN_BATCH + t] = mn[t][:, :D_DV]
            mc_s[j, n * GDN_BATCH + t, :D_DK, :] = mn[t][:, D_DV:].astype(BF16)
            mc_s[j, n * GDN_BATCH + t, D_DK:, :] = q_eff[t * c:(t + 1) * c]
            gt_s[j, n * GDN_BATCH + t] = jnp.broadcast_to(g_tot[t], (8, LANES))

    def prepare_some(i, carry):
        items = [(i * GDN_UNROLL + u, j) for u in range(GDN_UNROLL) for j in range(GDN_HEADS)]
        results = solve([prepare(*operands) for operands in [load(n, j) for n, j in items]])
        for (n, j), res in zip(items, results):
            store(n, j, *res)
        return carry

    lax.fori_loop(0, n_chunks // (GDN_BATCH * GDN_UNROLL), prepare_some, 0)

    def advance(n, states):
        r0 = pl.multiple_of(n * c, c)
        operands = [(o_ref[pl.ds(r0, c), lanes[j]], gt_s[j, n], mc_s[j, n], n_s[j, n]) for j in range(GDN_HEADS)]
        prods = [_dot(operands[j][2], states[j].astype(BF16)) for j in range(GDN_HEADS)]
        out = [states[j] * operands[j][1][0:1, :] - prods[j][:D_DK] + operands[j][3] for j in range(GDN_HEADS)]
        for j in range(GDN_HEADS):
            o_ref[pl.ds(r0, c), lanes[j]] = operands[j][0] + prods[j][D_DK:]
        return tuple(out)

    lax.fori_loop(0, n_chunks, advance, tuple(jnp.zeros((D_DK, D_DV), F32) for _ in range(GDN_HEADS)))
    for j in range(GDN_HEADS):
        o = o_ref[:, lanes[j]]
        o_ref[:, lanes[j]] = o * lax.rsqrt(jnp.mean(o * o, axis=-1, keepdims=True) + NORM_EPS) * ng


def _gated_deltanet(qkv, gates, conv_w, a_log, dt_bias, norm_g):
    b, s, _ = qkv.shape
    hp, width = GDN_HEADS, GDN_HEADS * LANES
    groups = D_HEADS // hp
    n_chunks = s // DN_CHUNK
    assert n_chunks % (GDN_BATCH * GDN_UNROLL) == 0
    col = lambda off: pl.BlockSpec((None, s, width), lambda i, h: (i, 0, off + h))
    wcol = lambda off: pl.BlockSpec((CONV_WIDTH, width), lambda i, h: (0, off + h))
    scalar = pl.BlockSpec((hp, 1, 1), lambda i, h: (h, 0, 0))
    return pl.pallas_call(
        _gdn_kernel, grid=(b, groups),
        in_specs=[col(0), col(groups), col(2 * groups), wcol(0), wcol(groups), wcol(2 * groups),
                  pl.BlockSpec((None, hp, s, 2), lambda i, h: (i, h, 0, 0)), scalar, scalar,
                  pl.BlockSpec((1, D_DV), lambda i, h: (0, 0))],
        out_specs=pl.BlockSpec((None, s, width), lambda i, h: (i, 0, h)),
        out_shape=jax.ShapeDtypeStruct((b, s, D_HEADS * D_DV), F32),
        scratch_shapes=[pltpu.VMEM((hp, s, LANES), F32)] * 3 + [pltpu.VMEM((hp, s, 1), F32)] * 2
        + [pltpu.VMEM((hp, n_chunks, D_DK + DN_CHUNK, D_DV), BF16), pltpu.VMEM((hp, n_chunks, D_DK, D_DV), F32),
           pltpu.VMEM((hp, n_chunks, 8, LANES), F32)],
        compiler_params=_cparams(("parallel", "parallel")), name="gated_deltanet",
    )(qkv, qkv, qkv, conv_w, conv_w, conv_w, gates,
      a_log.astype(F32).reshape(D_HEADS, 1, 1), dt_bias.astype(F32).reshape(D_HEADS, 1, 1),
      norm_g.astype(F32).reshape(1, D_DV))


def _out_kernel(*refs, odd, final):
    x_ref, mg_ref, gate_ref, w_ref = refs[:4]
    o_ref = refs[-1]
    rest = list(refs[4:-1])
    fin_ref = rest.pop() if final else None
    half = w_ref.shape[0] // 2
    sg = _silu(gate_ref[...].astype(F32))
    if odd:
        cmp_ref, slc_ref, win_ref, od_ref, small_ref = rest
        lane = lax.broadcasted_iota(jnp.int32, (1, LANES), 1)
        bg = jax.nn.sigmoid(small_ref[...])
        blocks = []
        for p_blk in range(C_HPG):
            sl = slice(p_blk * LANES, (p_blk + 1) * LANES)
            acc = 0.0
            for br, ref in enumerate((cmp_ref, slc_ref, win_ref)):
                ca = 2 * D_HEADS + p_blk * N_BRANCH + br
                cb = 2 * D_HEADS + (p_blk + C_HPG) * N_BRANCH + br
                acc = acc + jnp.where(lane < HEAD_DIM, bg[:, ca:ca + 1], bg[:, cb:cb + 1]) * ref[:, sl]
            blocks.append(acc)
        first = jnp.concatenate(blocks, axis=1)
        second = od_ref[...]
    else:
        first, second = rest[0][...], rest[1][...]
    y = (_dot((first * sg[:, :half]).astype(BF16), w_ref[:half, :])
         + _dot((second * sg[:, half:]).astype(BF16), w_ref[half:, :]))
    out = x_ref[...] + mg_ref[...] * y
    if final:
        out = out * lax.rsqrt(jnp.mean(out * out, axis=-1, keepdims=True) + NORM_EPS) * fin_ref[...]
    o_ref[...] = out


def _out_projection(x, mod_l, gate, w, branches, final_g=None):
    b, s, d = x.shape
    ts = min(TOKEN_TILE, s)
    odd = len(branches) > 2
    row = lambda width: pl.BlockSpec((None, ts, width), lambda i, j: (i, j, 0))
    in_specs = [row(d), pl.BlockSpec((None, None, 1, d), lambda i, j: (i, 2, 0, 0)), row(gate.shape[-1]),
                pl.BlockSpec(w.shape, lambda i, j: (0, 0), pipeline_mode=pl.Buffered(1))]
    in_specs += [row(a.shape[-1]) for a in branches]
    args = [x, mod_l, gate, w, *branches]
    if final_g is not None:
        in_specs.append(pl.BlockSpec((1, d), lambda i, j: (0, 0)))
        args.append(final_g.reshape(1, d))
    return pl.pallas_call(
        functools.partial(_out_kernel, odd=odd, final=final_g is not None), grid=(b, s // ts),
        in_specs=in_specs, out_specs=row(d), out_shape=jax.ShapeDtypeStruct((b, s, d), F32),
        compiler_params=_cparams(("parallel", "parallel")), name="gated_out_proj",
    )(*args)


def _pair_cols(a0, b0):
    a, bb = np.arange(a0, a0 + HEAD_DIM), np.arange(b0, b0 + HEAD_DIM)
    return np.concatenate([a[:HALF], bb[:HALF], a[HALF:], bb[HALF:]])


def _paired_head_order(width):
    pairs = [np.concatenate([np.arange(p * HEAD_DIM, (p + 1) * HEAD_DIM),
                             np.arange((p + C_HPG) * HEAD_DIM, (p + C_HPG + 1) * HEAD_DIM)]) for p in range(C_HPG)]
    return np.concatenate(pairs + [np.arange(C_HEADS * HEAD_DIM, width)])


def _even_layout():
    aq, ak, av = 0, 512, 1024
    bq, bk, bv, bf, gate = 1536, 2048, 2560, 3072, 3080
    zero = gate + 1024
    cols = [_pair_cols(ak + 2 * h * HEAD_DIM, ak + (2 * h + 1) * HEAD_DIM) for h in range(A_HEADS)]
    cols.append(np.arange(bk, bk + 512))
    cols += [np.concatenate([np.repeat(np.arange(bf + 2 * p, bf + 2 * p + 2), BIAS_PIECES),
                             np.full(LANES - 2 * BIAS_PIECES, zero)]) for p in range(B_HEADS // 2)]
    cols.append(np.arange(gate, gate + 1024))
    segs = (("rope", 0, 512), ("bf16", 512, 1024), ("f32", 1024, 1536), ("bf16", 1536, 2560))
    rows = [_pair_cols(aq + 2 * h * HEAD_DIM, aq + (2 * h + 1) * HEAD_DIM) for h in range(A_HEADS)]
    rows += [np.arange(bq, bq + 512), np.arange(av, av + 512), np.arange(bv, bv + 512)]
    tsegs = (("rope", 0, 512), ("bf16", 512, 2048))
    return np.concatenate(cols), segs, np.concatenate(rows), tsegs


def _odd_layout():
    cq, kc, vc, ks, vs, kw, vw, cg = 0, 512, 640, 768, 896, 1024, 1152, 1280
    dq, da, db, gate = 1304, 2840, 2844, 2848
    zero = gate + 1024
    cols = [_pair_cols(ks, ks + HEAD_DIM), _pair_cols(kw, kw + HEAD_DIM), np.arange(kc, kc + 256)]
    small = np.concatenate([np.arange(da, da + 2 * D_HEADS), np.arange(cg, cg + C_HEADS * N_BRANCH)])
    cols.append(np.concatenate([small, np.full(LANES - small.size, zero)]))
    cols += [np.arange(dq, dq + 1536), gate + _paired_head_order(1024)]
    segs = (("rope", 0, 256), ("f32", 256, 512), ("f32", 512, 640), ("f32", 640, 2176), ("bf16", 2176, 3200))
    rows = [_pair_cols(cq + p * HEAD_DIM, cq + (p + C_HPG) * HEAD_DIM) for p in range(C_HPG)]
    rows += [np.arange(vs, vs + LANES), np.arange(vw, vw + LANES)]
    tsegs = (("rope+raw", 0, 512), ("bf16", 512, 768))
    return np.concatenate(cols), segs, np.concatenate(rows), tsegs


def _layout_weights(w, cols, rows, n_query_rows):
    w = jnp.concatenate([w, jnp.zeros((w.shape[0], 1), w.dtype)], axis=1)
    scale = jnp.where(jnp.arange(rows.size) < n_query_rows, QK_SCALE, 1.0).astype(w.dtype)
    return w[:, cols].astype(BF16), (w[:, rows] * scale).T.astype(BF16)


def _compress_weights(pe, w1, w2, for_keys):
    half = L_CMP // 2 * HEAD_DIM
    w1ab = jnp.concatenate([w1[:half], w1[half:]], axis=1)
    w1ab = w1ab.reshape(CMP_STRIDE, HEAD_DIM, 2 * CMP_HIDDEN)
    zeros = jnp.zeros_like(w1ab)
    w1_by_tok = jnp.concatenate([jnp.concatenate([w1ab, zeros], axis=2),
                                 jnp.concatenate([zeros, w1ab], axis=2)], axis=1).astype(BF16)
    w2p = jnp.zeros((C_GROUPS, CMP_HIDDEN, LANES), F32)
    for g in range(C_GROUPS):
        if for_keys:
            w2p = w2p.at[g, :, g * HALF:(g + 1) * HALF].set(w2[:, :HALF])
            w2p = w2p.at[g, :, HEAD_DIM + g * HALF:HEAD_DIM + (g + 1) * HALF].set(w2[:, HALF:])
        else:
            w2p = w2p.at[g, :, g * HEAD_DIM:(g + 1) * HEAD_DIM].set(w2)
    if not for_keys:
        w2p = w2p.transpose(0, 2, 1)
    return [w1_by_tok, w1.astype(F32), pe.astype(F32).reshape(1, L_CMP * HEAD_DIM), w2p.astype(BF16)]


def _even_layer(x, mod_l, rope, layer_idx, g, w_in, b_forget, lq1, lk1, lq2, lk2, subln_g, w_out, final_g):
    cols, segs, rows, tsegs = _even_layout()
    w, wt = _layout_weights(w_in, cols, rows, 2 * A_HEADS * HEAD_DIM + B_HEADS * HEAD_DIM)
    k_a, k_b, forget_logits, gate, qt_a, rest_t = _projection(x, mod_l, g, rope, w, wt, segs, tsegs)
    lam_init = 0.8 - 0.6 * math.exp(-0.3 * layer_idx)
    vec = lambda a: a.astype(F32).reshape(1, -1)
    const = lambda shape: (lambda tq, tk, tile: pl.BlockSpec(shape, lambda i, hb, *t: (0, 0)))
    nb = B_HEADS // 2
    per_step = 2
    oa = _flash("diff", qt_a, k_a, rest_t, A_HEADS, per_step, 0, 0, nb, False,
                extra=[vec(lq1), vec(lk1), vec(lq2), vec(lk2), subln_g.astype(F32).reshape(LANES, 1)],
                extra_specs=[const((1, HEAD_DIM))] * 4 + [const((LANES, 1))], lam_init=lam_init)
    bias = _forget_cumsum(forget_logits, b_forget)
    bias_spec = lambda tq, tk, tile: pl.BlockSpec((None, per_step, tk, LANES),
                                                  lambda i, hb, *t: (i, hb, tile(*t)[1], 0))
    ob = _flash("fox", rest_t, k_b, rest_t, nb, per_step, 0, 0, nb + A_HEADS, False,
                extra=[bias], extra_specs=[bias_spec])
    return _out_projection(x, mod_l, gate, w_out.astype(BF16), [oa, ob], final_g)


def _odd_layer(x, mod_l, rope, g, w_in, pe_k, pe_v, w1_k, w2_k, w1_v, w2_v, conv_w, a_log, dt_bias,
               dn_norm_g, w_out, final_g):
    b, s, _ = x.shape
    cols, segs, rows, tsegs = _odd_layout()
    w, wt = _layout_weights(w_in, cols, rows, C_HEADS * HEAD_DIM)
    k_rot, cmp_in, small, dqkv, gate, q_raw_t, q_rot_t, v_t = _projection(x, mod_l, g, rope, w, wt, segs, tsegs)
    n_cmp = (s - L_CMP) // CMP_STRIDE + 1
    kcmp, vcmp_t = _compress(cmp_in, _compress_weights(pe_k, w1_k, w2_k, True),
                             _compress_weights(pe_v, w1_v, w2_v, False))
    o_cmp, sel = _cmp_select(q_raw_t, kcmp, vcmp_t, n_cmp)
    sel_spec = lambda tq, tk, tile: pl.BlockSpec((None, LANES, tq), lambda i, hb, *t: (i, 0, tile(*t)[0]))
    o_slc = _flash("sel", q_rot_t, k_rot, v_t, C_HPG, C_HPG, 0, 0, 0, True, extra=[sel], extra_specs=[sel_spec])
    o_win = _flash("win", q_rot_t, k_rot, v_t, C_HPG, C_HPG, 0, 1, 1, True)
    gates = small[:, :, :2 * D_HEADS].reshape(b, s, 2, D_HEADS).transpose(0, 3, 1, 2)
    od = _gated_deltanet(dqkv, gates, conv_w.astype(F32), a_log, dt_bias, dn_norm_g)
    rows = _paired_head_order(w_out.shape[0])
    return _out_projection(x, mod_l, gate, w_out[rows].astype(BF16), [o_cmp, o_slc, o_win, od, small], final_g)


def kernel(x, c, positions, norm_g, w_mod, b_mod, w_out, final_norm_g, w_in_even, b_forget, lambda_q1, lambda_k1,
           lambda_q2, lambda_k2, subln_g, w_in_odd, cmp_pe_k, cmp_pe_v, cmp_w1_k, cmp_w2_k, cmp_w1_v, cmp_w2_v,
           conv_w, a_log, dt_bias, dn_norm_g):
    depth = norm_g.shape[0]
    rope = _rope_tables(positions)
    mod = _modulation(c, w_mod, b_mod)
    for l in range(depth):
        final_g = final_norm_g if l == depth - 1 else None
        i = l // 2
        if l % 2 == 0:
            x = _even_layer(x, mod[l], rope, l, norm_g[l], w_in_even[i], b_forget[i], lambda_q1[i],
                            lambda_k1[i], lambda_q2[i], lambda_k2[i], subln_g[i], w_out[l], final_g)
        else:
            x = _odd_layer(x, mod[l], rope, norm_g[l], w_in_odd[i], cmp_pe_k[i], cmp_pe_v[i], cmp_w1_k[i],
                           cmp_w2_k[i], cmp_w1_v[i], cmp_w2_v[i], conv_w[i], a_log[i], dt_bias[i], dn_norm_g[i],
                           w_out[l], final_g)
    return x
```

```python
import functools
import math

import jax
import jax.numpy as jnp
import numpy as np
from jax import lax
from jax.experimental import pallas as pl
from jax.experimental.pallas import tpu as pltpu

F32 = jnp.float32
BF16 = jnp.bfloat16
HI = lax.Precision.HIGHEST

LANES = 128
HEAD_DIM = 64
HALF = HEAD_DIM // 2
ROPE_THETA = 10000.0
NORM_EPS = 1e-6
NEG = -1e30
MASK_BIG = 1e30
LOG2E = math.log2(math.e)
QK_SCALE = HEAD_DIM ** -0.5 * LOG2E
BIAS_PIECES = 3
NORM_ROWS = 16
A_HEADS = 4
B_HEADS = 8
C_HEADS = 8
C_GROUPS = 2
C_HPG = C_HEADS // C_GROUPS
L_CMP = 32
CMP_STRIDE = 16
CMP_HIDDEN = 256
L_SEL = 64
N_SEL = 8
WINDOW = 512
N_BRANCH = 3
FORCE_BONUS = 1e4
D_HEADS = 4
D_DK = 128
D_DV = 128
CONV_WIDTH = 4
DN_CHUNK = 64
TOKEN_TILE = 512
VMEM_LIMIT = 56 * 1024 * 1024


def _cparams(sem):
    return pltpu.CompilerParams(dimension_semantics=sem, vmem_limit_bytes=VMEM_LIMIT)


def _dot(a, b, precision=None):
    return jnp.dot(a, b, precision=precision, preferred_element_type=F32)


def _dot_nt(a, b, precision=None):
    return lax.dot_general(a, b, (((1,), (1,)), ((), ())), precision=precision, preferred_element_type=F32)


def _dot_tn(a, b, precision=None):
    return lax.dot_general(a, b, (((0,), (0,)), ((), ())), precision=precision, preferred_element_type=F32)


def _split_bf16(a):
    hi = a.astype(BF16)
    return hi, (a - hi.astype(F32)).astype(BF16)


def _softplus(z):
    return jnp.maximum(z, 0.0) + jnp.log1p(jnp.exp(-jnp.abs(z)))


def _silu(z):
    return z * jax.nn.sigmoid(z)


def _rope_table_kernel(pos_ref, inv_ref, cos_ref, sin_ref, cos_t_ref, sin_t_ref):
    ang = pos_ref[...].astype(F32) * inv_ref[...]
    lane = lax.broadcasted_iota(jnp.int32, (1, LANES), 1)
    cos = jnp.cos(ang)
    sin = jnp.where(lane < 2 * HALF, -1.0, 1.0) * jnp.sin(ang)
    cos_ref[...] = cos
    sin_ref[...] = sin
    cos_t_ref[...] = cos.T
    sin_t_ref[...] = sin.T


def _rope_tables(positions):
    b, s = positions.shape
    inv = ROPE_THETA ** (-jnp.arange(0, HEAD_DIM, 2, dtype=F32) / HEAD_DIM)
    inv = jnp.tile(inv, 4).reshape(1, LANES)
    tok = jax.ShapeDtypeStruct((b, s, LANES), F32)
    feat = jax.ShapeDtypeStruct((b, LANES, s), F32)
    return pl.pallas_call(
        _rope_table_kernel, grid=(b,),
        in_specs=[pl.BlockSpec((None, s, 1), lambda i: (i, 0, 0)),
                  pl.BlockSpec((1, LANES), lambda i: (0, 0))],
        out_specs=[pl.BlockSpec((None, s, LANES), lambda i: (i, 0, 0))] * 2
        + [pl.BlockSpec((None, LANES, s), lambda i: (i, 0, 0))] * 2,
        out_shape=[tok, tok, feat, feat], compiler_params=_cparams(("parallel",)), name="rope_tables",
    )(positions.reshape(b, s, 1), inv)


def _mod_kernel(c_ref, w_ref, b_ref, o_ref):
    o_ref[...] = _dot(_silu(c_ref[...]), w_ref[...], HI) + b_ref[...]


def _modulation(c, w_mod, b_mod):
    depth, d, n = w_mod.shape
    b = c.shape[0]
    tn = 1024
    mod = pl.pallas_call(
        _mod_kernel, grid=(depth, n // tn),
        in_specs=[pl.BlockSpec((b, d), lambda l, j: (0, 0)),
                  pl.BlockSpec((None, d, tn), lambda l, j: (l, 0, j)),
                  pl.BlockSpec((None, 1, tn), lambda l, j: (l, 0, j))],
        out_specs=pl.BlockSpec((None, b, tn), lambda l, j: (l, 0, j)),
        out_shape=jax.ShapeDtypeStruct((depth, b, n), F32),
        compiler_params=_cparams(("parallel", "parallel")), name="modulation",
    )(c, w_mod, b_mod.reshape(depth, 1, n))
    return mod.reshape(depth, b, 3, 1, d)


def _proj_kernel(x_ref, shift_ref, scale_ref, g_ref, cos_ref, sin_ref, cos_t_ref, sin_t_ref, w_ref, wt_ref,
                 *out_refs, segs, tsegs):
    x = x_ref[...]
    h = x * lax.rsqrt(jnp.mean(x * x, axis=-1, keepdims=True) + NORM_EPS) * g_ref[...]
    h = (h * (1.0 + scale_ref[...]) + shift_ref[...]).astype(BF16)
    outs = list(out_refs)
    step = 4 * LANES
    for kind, c0, c1 in segs:
        o_ref = outs.pop(0)
        for a in range(c0, c1, step):
            e = min(a + step, c1)
            acc = _dot(h, w_ref[:, a:e])
            if kind == "rope":
                cos, sin = cos_ref[...], sin_ref[...]
                for j in range(0, e - a, LANES):
                    blk = acc[:, j:j + LANES]
                    rot = blk * cos + pltpu.roll(blk, 2 * HALF, axis=1) * sin
                    o_ref[:, a - c0 + j:a - c0 + j + LANES] = rot.astype(o_ref.dtype)
            else:
                o_ref[:, a - c0:e - c0] = acc.astype(o_ref.dtype)
    for kind, r0, r1 in tsegs:
        raw_ref = outs.pop(0) if kind == "rope+raw" else None
        o_ref = outs.pop(0)
        for a in range(r0, r1, step):
            e = min(a + step, r1)
            acc = _dot_nt(wt_ref[a:e, :], h)
            if raw_ref is not None:
                raw_ref[a - r0:e - r0, :] = acc
            if kind in ("rope", "rope+raw"):
                cos, sin = cos_t_ref[...], sin_t_ref[...]
                for j in range(0, e - a, LANES):
                    blk = acc[j:j + LANES, :]
                    rot = blk * cos + pltpu.roll(blk, 2 * HALF, axis=0) * sin
                    o_ref[a - r0 + j:a - r0 + j + LANES, :] = rot.astype(o_ref.dtype)
            else:
                o_ref[a - r0:e - r0, :] = acc.astype(o_ref.dtype)


def _projection(x, mod_l, g, rope, w, wt, segs, tsegs):
    b, s, d = x.shape
    ts = min(TOKEN_TILE, s)
    row = lambda width: pl.BlockSpec((None, ts, width), lambda i, j: (i, j, 0))
    col = lambda height: pl.BlockSpec((None, height, ts), lambda i, j: (i, 0, j))
    out_specs, out_shapes = [], []
    for kind, c0, c1 in segs:
        out_specs.append(row(c1 - c0))
        out_shapes.append(jax.ShapeDtypeStruct((b, s, c1 - c0), F32 if kind == "f32" else BF16))
    for kind, r0, r1 in tsegs:
        if kind == "rope+raw":
            out_specs.append(col(r1 - r0))
            out_shapes.append(jax.ShapeDtypeStruct((b, r1 - r0, s), F32))
        out_specs.append(col(r1 - r0))
        out_shapes.append(jax.ShapeDtypeStruct((b, r1 - r0, s), BF16))
    modspec = lambda k: pl.BlockSpec((None, None, 1, d), lambda i, j: (i, k, 0, 0))
    resident = lambda a: pl.BlockSpec(a.shape, lambda i, j: (0, 0), pipeline_mode=pl.Buffered(1))
    return pl.pallas_call(
        functools.partial(_proj_kernel, segs=segs, tsegs=tsegs), grid=(b, s // ts),
        in_specs=[row(d), modspec(0), modspec(1), pl.BlockSpec((1, d), lambda i, j: (0, 0)),
                  row(LANES), row(LANES), col(LANES), col(LANES), resident(w), resident(wt)],
        out_specs=out_specs, out_shape=out_shapes,
        compiler_params=_cparams(("parallel", "parallel")), name="adaln_in_proj",
    )(x, mod_l, mod_l, g.reshape(1, d), *rope, w, wt)


def _cum_kernel(x_ref, bias_ref, o_ref):
    s = x_ref.shape[0]
    ii = lax.broadcasted_iota(jnp.int32, (LANES, LANES), 0)
    jj = lax.broadcasted_iota(jnp.int32, (LANES, LANES), 1)
    lower = (ii >= jj).astype(F32)
    lane = lax.broadcasted_iota(jnp.int32, (1, LANES), 1)
    carry = jnp.zeros((1, LANES), F32)
    for r0 in range(0, s, LANES):
        z = x_ref[r0:r0 + LANES, :] + bias_ref[...]
        logf = jnp.minimum(z, 0.0) - jnp.log1p(jnp.exp(-jnp.abs(z)))
        loc = _dot(lower, logf, HI) + carry
        carry = loc[LANES - 1:LANES, :]
        val = loc * (-LOG2E)
        hi = val.astype(BF16)
        rest = val - hi.astype(F32)
        mid = rest.astype(BF16)
        lo = (rest - mid.astype(F32)).astype(BF16)
        piece = jnp.where(lane % BIAS_PIECES == 0, hi, jnp.where(lane % BIAS_PIECES == 1, mid, lo))
        o_ref[r0:r0 + LANES, :] = jnp.where(lane < B_HEADS * BIAS_PIECES, piece, jnp.zeros_like(piece))


def _forget_cumsum(logits, b_forget):
    b, s, _ = logits.shape
    bias = jnp.zeros((1, LANES), F32).at[0, :B_HEADS * BIAS_PIECES].set(jnp.repeat(b_forget.astype(F32), BIAS_PIECES))
    return pl.pallas_call(
        _cum_kernel, grid=(b,),
        in_specs=[pl.BlockSpec((None, s, LANES), lambda i: (i, 0, 0)), pl.BlockSpec((1, LANES), lambda i: (0, 0))],
        out_specs=pl.BlockSpec((None, s, LANES), lambda i: (i, 0, 0)),
        out_shape=jax.ShapeDtypeStruct((b, s, LANES), BF16),
        compiler_params=_cparams(("parallel",)), name="forget_cumsum",
    )(logits, bias)


def _triangle_step(t, nq):
    qi = sum((t >= j * (j + 1) // 2).astype(jnp.int32) for j in range(1, nq))
    return qi, t - qi * (qi + 1) // 2


def _flash_kernel(*refs, mode, tq, tk, nq, nblk, shared_kv, lam_init):
    qt_ref, k_ref, vt_ref = refs[:3]
    m_ref, acc_ref = refs[-2:]
    o_ref = refs[-3]
    extra = refs[3:-3]
    if mode == "win":
        qi, ki = pl.program_id(2), pl.program_id(3)
        kv, last = qi - 1 + ki, ki == pl.num_programs(3) - 1
    else:
        qi, ki = _triangle_step(pl.program_id(2), nq)
        kv, last = ki, ki == qi
    row = lax.broadcasted_iota(jnp.int32, (LANES, 1), 0)
    if mode == "fox":
        slot_rows = (row < HEAD_DIM, row >= HEAD_DIM)
    else:
        slot_rows = ((row % HEAD_DIM) < HALF, (row % HEAD_DIM) >= HALF)
    vrows = acc_ref.shape[1] - NORM_ROWS
    block = lambda j: slice(j * LANES, (j + 1) * LANES)
    half = tq // 2

    @pl.when(ki == 0)
    def _():
        m_ref[...] = jnp.full(m_ref.shape, NEG, F32)
        acc_ref[...] = jnp.zeros(acc_ref.shape, F32)

    def step(kind):
        rr = lax.broadcasted_iota(jnp.int32, (LANES, tq), 0)
        k_extra = [None, None]
        for s in range(2):
            if mode == "sel":
                cb = (kv * tk + lax.broadcasted_iota(jnp.int32, (tk, LANES), 0)) // L_SEL
                ll = lax.broadcasted_iota(jnp.int32, (tk, LANES), 1)
                k_extra[s] = jnp.where(ll == cb + s * (LANES // 4), MASK_BIG, 0.0).astype(BF16)

        def q_extra(j, s):
            if mode == "sel":
                return extra[0][...]
            if mode == "fox":
                first = ((pl.program_id(1) * nblk + j) * 2 + s) * BIAS_PIECES
                return jnp.where((rr >= first) & (rr < first + BIAS_PIECES), 1.0, 0.0).astype(BF16)
            return None
        lo, hi, everything = slice(0, half), slice(half, tq), slice(0, tq)
        parts = {"full": [(everything, everything, None)],
                 "lower": [(lo, lo, lo), (everything, hi, hi)],
                 "upper": [(everything, lo, lo), (hi, hi, hi)]}[kind]
        tri_k = lax.broadcasted_iota(jnp.int32, (half, half), 0)
        tri_q = lax.broadcasted_iota(jnp.int32, (half, half), 1)
        tri = tri_k <= tri_q if kind == "lower" else tri_k > tri_q
        ones = jnp.ones((NORM_ROWS, tk), BF16)
        sts, vts = [], []
        for j in range(nblk):
            qt = qt_ref[block(j), :]
            k = k_ref[...] if shared_kv else k_ref[:, block(j)]
            base = 0 if shared_kv else j * LANES
            for s in range(2):
                qs = jnp.where(slot_rows[s], qt, jnp.zeros_like(qt))
                q_more = q_extra(j, s)
                if q_more is not None:
                    k_more = extra[0][...] if mode == "fox" else k_extra[s]
                    k_all, q_all = jnp.concatenate([k, k_more], axis=1), jnp.concatenate([qs, q_more], axis=0)
                else:
                    k_all, q_all = k, qs
                sts.append([_dot(k_all[krows], q_all[:, lanes]) for krows, lanes, _ in parts])
                v0 = base if vrows == LANES else base + s * vrows
                vts.append(jnp.concatenate([vt_ref[v0:v0 + vrows, :], ones], axis=0))
        for idx in range(2 * nblk):
            for (krows, lanes, tri_rows), st in zip(parts, sts[idx]):
                if tri_rows is not None:
                    if krows == tri_rows:
                        st = jnp.where(tri, st, NEG)
                    elif tri_rows == lo:
                        st = jnp.concatenate([jnp.where(tri, st[:half], NEG), st[half:]], axis=0)
                    else:
                        st = jnp.concatenate([st[:half], jnp.where(tri, st[half:], NEG)], axis=0)
                m_prev = m_ref[idx, :, lanes]
                m_new = jnp.maximum(m_prev, jnp.max(st, axis=0, keepdims=True))
                alpha = jnp.exp2(m_prev - m_new)
                p = jnp.exp2(st - m_new).astype(BF16)
                acc_ref[idx, :, lanes] = alpha * acc_ref[idx, :, lanes] + _dot(vts[idx][:, krows], p)
                m_ref[idx, :, lanes] = m_new

    if mode == "win":
        pl.when((ki == 0) & (kv >= 0))(functools.partial(step, "upper"))
        pl.when(ki == 1)(functools.partial(step, "lower"))
    else:
        pl.when(ki < qi)(functools.partial(step, "full"))
        pl.when(ki == qi)(functools.partial(step, "lower"))

    @pl.when(last)
    def _():
        for j in range(nblk):
            o0 = acc_ref[2 * j, :vrows, :] / acc_ref[2 * j, vrows:vrows + 1, :]
            o1 = acc_ref[2 * j + 1, :vrows, :] / acc_ref[2 * j + 1, vrows:vrows + 1, :]
            if mode == "diff":
                lq1, lk1, lq2, lk2, subg = (r[...] for r in extra)
                lam = (jnp.exp(jnp.sum(lq1 * lk1, axis=1, keepdims=True))
                       - jnp.exp(jnp.sum(lq2 * lk2, axis=1, keepdims=True)) + lam_init)
                o = o0 - lam * o1
                o = o * lax.rsqrt(jnp.mean(o * o, axis=0, keepdims=True) + NORM_EPS) * subg * (1.0 - lam_init)
            else:
                o = jnp.concatenate([o0, o1], axis=0)
            o_ref[:, block(j)] = o.T.astype(o_ref.dtype)


def _flash(mode, qt, k, vt, n_qblocks, nblk, q0, k0, v0, shared_kv, extra=(), extra_specs=(), lam_init=0.0):
    b, s, _ = k.shape
    tq = tk = min(TOKEN_TILE, s)
    nq = s // tq
    if mode == "win":
        assert WINDOW == tk
        steps, sem = (nq, 2), ("parallel", "arbitrary")
        tile = lambda qi, ki: (qi, jnp.maximum(qi - 1 + ki, 0))
    else:
        steps, sem = (nq * (nq + 1) // 2,), ("arbitrary",)
        tile = lambda t: _triangle_step(t, nq)
    assert n_qblocks % nblk == 0 and q0 % nblk == 0 and (shared_kv or (k0 % nblk == 0 and v0 % nblk == 0))
    wide = nblk * LANES
    if shared_kv:
        k_spec = pl.BlockSpec((None, tk, LANES), lambda i, hb, *t: (i, tile(*t)[1], k0))
        v_spec = pl.BlockSpec((None, LANES, tk), lambda i, hb, *t: (i, v0, tile(*t)[1]))
    else:
        k_spec = pl.BlockSpec((None, tk, wide), lambda i, hb, *t: (i, tile(*t)[1], k0 // nblk + hb))
        v_spec = pl.BlockSpec((None, wide, tk), lambda i, hb, *t: (i, v0 // nblk + hb, tile(*t)[1]))
    in_specs = [pl.BlockSpec((None, wide, tq), lambda i, hb, *t: (i, q0 // nblk + hb, tile(*t)[0])), k_spec, v_spec]
    for spec in extra_specs:
        in_specs.append(spec(tq, tk, tile))
    vrows = LANES if mode == "diff" else HEAD_DIM
    return pl.pallas_call(
        functools.partial(_flash_kernel, mode=mode, tq=tq, tk=tk, nq=nq, nblk=nblk, shared_kv=shared_kv,
                          lam_init=lam_init),
        grid=(b, n_qblocks // nblk) + steps, in_specs=in_specs,
        out_specs=pl.BlockSpec((None, tq, wide), lambda i, hb, *t: (i, tile(*t)[0], hb)),
        out_shape=jax.ShapeDtypeStruct((b, s, n_qblocks * LANES), BF16),
        scratch_shapes=[pltpu.VMEM((2 * nblk, 1, tq), F32), pltpu.VMEM((2 * nblk, vrows + NORM_ROWS, tq), F32)],
        compiler_params=_cparams(("parallel", "parallel") + sem),
        name="flash_" + mode,
    )(qt, k, vt, *extra)


def _compress_kernel(xk_ref, xv_ref, w1_by_tok_k, w1_k, pe_k, w2_k, w1_by_tok_v, w1_v, pe_v, w2_v, ok_ref, ov_ref):
    cn = xk_ref.shape[0] // CMP_STRIDE

    def run(x_ref, w1_by_tok, w1, pe, w2, o_ref, transposed):
        pe_term = _dot(pe[...], w1[...], HI)
        ab = jnp.zeros((cn, C_GROUPS * 2 * CMP_HIDDEN), F32)
        for tok in range(CMP_STRIDE):
            rows = x_ref[pl.ds(tok, cn, stride=CMP_STRIDE), :]
            ab = ab + _dot(rows.astype(BF16), w1_by_tok[tok])
        out = 0.0
        for g in range(C_GROUPS):
            first = ab[:, g * 2 * CMP_HIDDEN:(g * 2 + 1) * CMP_HIDDEN]
            second = ab[:, (g * 2 + 1) * CMP_HIDDEN:(g + 1) * 2 * CMP_HIDDEN]
            hid = first + pltpu.roll(second, cn - 1, axis=0) + pe_term
            act = _silu(hid).astype(BF16)
            out = out + (_dot_nt(w2[g], act) if transposed else _dot(act, w2[g]))
        o_ref[...] = out
    run(xk_ref, w1_by_tok_k, w1_k, pe_k, w2_k, ok_ref, False)
    run(xv_ref, w1_by_tok_v, w1_v, pe_v, w2_v, ov_ref, True)


def _compress(cmp_in, weights_k, weights_v):
    b, s, width = cmp_in.shape
    cn = s // CMP_STRIDE
    full = lambda a: pl.BlockSpec(a.shape, lambda i: (0,) * a.ndim)
    return pl.pallas_call(
        _compress_kernel, grid=(b,),
        in_specs=[pl.BlockSpec((None, s, LANES), lambda i: (i, 0, 0)), pl.BlockSpec((None, s, LANES), lambda i: (i, 0, 1))]
        + [full(a) for a in weights_k + weights_v],
        out_specs=[pl.BlockSpec((None, cn, LANES), lambda i: (i, 0, 0)),
                   pl.BlockSpec((None, LANES, cn), lambda i: (i, 0, 0))],
        out_shape=[jax.ShapeDtypeStruct((b, cn, LANES), F32), jax.ShapeDtypeStruct((b, LANES, cn), F32)],
        compiler_params=_cparams(("parallel",)), name="nsa_compress",
    )(cmp_in, cmp_in, *weights_k, *weights_v)


def _cmp_select_kernel(qt_ref, kc_ref, vct_ref, o_ref, sel_ref, *, tq, n_blk, n_cmp):
    qi = pl.program_id(1)
    cn = kc_ref.shape[0]
    kc_hi, kc_lo = _split_bf16(kc_ref[...])
    vct = vct_ref[...].astype(BF16)
    row = lax.broadcasted_iota(jnp.int32, (LANES, 1), 0)
    slot_rows = ((row % HEAD_DIM) < HALF, (row % HEAD_DIM) >= HALF)
    t_row = qi * tq + lax.broadcasted_iota(jnp.int32, (1, tq), 1)
    m_col = lax.broadcasted_iota(jnp.int32, (cn, 1), 0)
    valid = (m_col * CMP_STRIDE + L_CMP - 1 <= t_row) & (m_col < n_cmp)
    any_valid = (t_row >= L_CMP - 1).astype(F32)
    psum = [jnp.zeros((cn, tq), F32) for _ in range(C_GROUPS)]
    for p_blk in range(C_HPG):
        qt = qt_ref[p_blk * LANES:(p_blk + 1) * LANES, :]
        outs = []
        for s in range(C_GROUPS):
            q_hi, q_lo = _split_bf16(jnp.where(slot_rows[s], qt, 0.0))
            sc = jnp.where(valid, _dot(kc_hi, q_hi) + _dot(kc_lo, q_hi) + _dot(kc_hi, q_lo), NEG)
            e = jnp.exp2(sc - jnp.max(sc, axis=0, keepdims=True))
            p = e / jnp.sum(e, axis=0, keepdims=True) * any_valid
            psum[s] = psum[s] + p
            outs.append(_dot(vct[s * HEAD_DIM:(s + 1) * HEAD_DIM, :], p.astype(BF16)))
        o_ref[:, p_blk * LANES:(p_blk + 1) * LANES] = jnp.concatenate(outs, axis=0).T.astype(o_ref.dtype)
    jb = lax.broadcasted_iota(jnp.int32, (n_blk, cn), 0)
    mm = lax.broadcasted_iota(jnp.int32, (n_blk, cn), 1)
    overlap = ((mm * CMP_STRIDE < jb * L_SEL + L_SEL) & (mm * CMP_STRIDE + L_CMP > jb * L_SEL)
               & (mm < n_cmp)).astype(BF16)
    j = lax.broadcasted_iota(jnp.int32, (n_blk, tq), 0)
    cur = (qi * tq + lax.broadcasted_iota(jnp.int32, (n_blk, tq), 1)) // L_SEL
    forced = (j == 0) | (j == cur) | (j == cur - 1)
    n_top = min(N_SEL, n_blk)
    pad_rows = LANES // 4 - n_blk
    parts = []
    for s in range(C_GROUPS):
        p_hi, p_lo = _split_bf16(psum[s])
        imp = _dot(overlap, p_hi) + _dot(overlap, p_lo)
        score = jnp.where(j > cur, NEG, imp + jnp.where(forced, FORCE_BONUS, 0.0))
        rank = jnp.zeros((n_blk, tq), jnp.int32)
        for jp in range(n_blk):
            r = score[jp:jp + 1, :]
            rank = rank + ((r > score) | ((r == score) & (jp < j))).astype(jnp.int32)
        parts.append(jnp.where(rank < n_top, 0.0, -1.0))
        if pad_rows:
            parts.append(jnp.zeros((pad_rows, tq), F32))
    parts.append(jnp.zeros((LANES // 2, tq), F32))
    sel_ref[...] = jnp.concatenate(parts, axis=0).astype(sel_ref.dtype)


def _cmp_select(q_raw_t, kcmp, vcmp_t, n_cmp):
    b, width, s = q_raw_t.shape
    cn = kcmp.shape[1]
    tq = min(TOKEN_TILE, s)
    n_blk = s // L_SEL
    return pl.pallas_call(
        functools.partial(_cmp_select_kernel, tq=tq, n_blk=n_blk, n_cmp=n_cmp), grid=(b, s // tq),
        in_specs=[pl.BlockSpec((None, width, tq), lambda i, j: (i, 0, j)),
                  pl.BlockSpec((None, cn, LANES), lambda i, j: (i, 0, 0)),
                  pl.BlockSpec((None, LANES, cn), lambda i, j: (i, 0, 0))],
        out_specs=[pl.BlockSpec((None, tq, width), lambda i, j: (i, j, 0)),
                   pl.BlockSpec((None, LANES, tq), lambda i, j: (i, 0, j))],
        out_shape=[jax.ShapeDtypeStruct((b, s, width), BF16), jax.ShapeDtypeStruct((b, LANES, s), BF16)],
        compiler_params=_cparams(("parallel", "parallel")), name="nsa_cmp_select",
    )(q_raw_t, kcmp, vcmp_t)


GDN_HEADS = 2
GDN_BATCH = 2
GDN_UNROLL = 8


def _gdn_kernel(q_ref, k_ref, v_ref, wq_ref, wk_ref, wv_ref, gate_ref, alog_ref, dtb_ref, ng_ref, o_ref,
                qs, ks, vs, gs, bs, mc_s, n_s, gt_s):
    s_len = q_ref.shape[0]
    c = DN_CHUNK
    n_chunks = s_len // c
    rows = lax.broadcasted_iota(jnp.int32, (s_len, 1), 0)

    def conv(x, w):
        y = x * w[CONV_WIDTH - 1:CONV_WIDTH, :]
        for back in range(1, CONV_WIDTH):
            xs = jnp.where(rows >= back, pltpu.roll(x, back, axis=0), 0.0)
            y = y + xs * w[CONV_WIDTH - 1 - back:CONV_WIDTH - back, :]
        return _silu(y)

    def l2norm(a):
        return a * lax.rsqrt(jnp.sum(a * a, axis=-1, keepdims=True) + NORM_EPS)

    lanes = [slice(j * LANES, (j + 1) * LANES) for j in range(GDN_HEADS)]
    for j in range(GDN_HEADS):
        qs[j] = l2norm(conv(q_ref[:, lanes[j]], wq_ref[:, lanes[j]])) * (D_DK ** -0.5)
        ks[j] = l2norm(conv(k_ref[:, lanes[j]], wk_ref[:, lanes[j]]))
        vs[j] = conv(v_ref[:, lanes[j]], wv_ref[:, lanes[j]])
        gate = gate_ref[j]
        gs[j] = -jnp.exp(alog_ref[j]) * _softplus(gate[:, 0:1] + dtb_ref[j])
        bs[j] = jax.nn.sigmoid(gate[:, 1:2])

    sc = GDN_BATCH * c
    ii = lax.broadcasted_iota(jnp.int32, (sc, sc), 0)
    jj = lax.broadcasted_iota(jnp.int32, (sc, sc), 1)
    same = (ii // c) == (jj // c)
    causal, strict, upper, eye = same & (ii >= jj), same & (ii > jj), same & (ii <= jj), ii == jj
    chunk_end = same & (jj % c == c - 1)
    ng = ng_ref[...]

    def load(n, j):
        r0 = pl.multiple_of(n * sc, sc)
        return (qs[j, pl.ds(r0, sc), :], ks[j, pl.ds(r0, sc), :], vs[j, pl.ds(r0, sc), :],
                gs[j, pl.ds(r0, sc), :], bs[j, pl.ds(r0, sc), :])

    def prepare(q, k, v, g, beta):
        g_row = jnp.sum(jnp.where(eye, g, 0.0), axis=0, keepdims=True)
        gc_col = jnp.sum(jnp.where(causal, g_row, 0.0), axis=1, keepdims=True)
        gc_row = jnp.sum(jnp.where(upper, g, 0.0), axis=0, keepdims=True)
        g_last = jnp.sum(jnp.where(chunk_end, gc_row, 0.0), axis=1, keepdims=True)
        decay = jnp.where(causal, jnp.exp(jnp.where(causal, gc_col - gc_row, 0.0)), 0.0)
        eg = jnp.exp(gc_col)
        kb = k * beta
        kbf = k.astype(BF16)
        raw = _dot_nt(kb.astype(BF16), kbf)
        raw_qk = _dot_nt(q.astype(BF16), kbf)
        rhs = jnp.concatenate([v * beta, kb * eg], axis=1).astype(BF16)
        kd = k * jnp.exp(g_last - gc_col)
        kd_t = [kd[t * c:(t + 1) * c].T.astype(BF16) for t in range(GDN_BATCH)]
        g_tot = [jnp.exp(g_last[t * c:t * c + 1]) for t in range(GDN_BATCH)]
        return raw, raw_qk, decay, rhs, kd_t, g_tot, q * eg

    def solve(prepared):
        n = len(prepared)
        decay = [p[2] for p in prepared]
        x = [-jnp.where(strict, prepared[i][0] * decay[i], 0.0) for i in range(n)]
        inv = [jnp.where(eye, 1.0, 0.0) + x[i] for i in range(n)]
        xb = [x[i].astype(BF16) for i in range(n)]
        x = [_dot(xb[i], xb[i]) for i in range(n)]
        for _ in range(int(math.log2(c)) - 2):
            xb = [x[i].astype(BF16) for i in range(n)]
            both = [_dot(jnp.concatenate([xb[i], inv[i].astype(BF16)], axis=0), xb[i]) for i in range(n)]
            x = [both[i][:sc] for i in range(n)]
            inv = [inv[i] + both[i][sc:] for i in range(n)]
        last = [_dot(inv[i].astype(BF16), x[i].astype(BF16)) for i in range(n)]
        inv = [inv[i] + last[i] for i in range(n)]
        sol = [_dot(inv[i].astype(BF16), prepared[i][3]).astype(BF16) for i in range(n)]
        qk = [(prepared[i][1] * decay[i]).astype(BF16) for i in range(n)]
        qo = [_dot(qk[i], sol[i]) for i in range(n)]
        mn = [[_dot(prepared[i][4][t], sol[i][t * c:(t + 1) * c]) for t in range(GDN_BATCH)] for i in range(n)]
        return [(qo[i], prepared[i][6], mn[i], prepared[i][5]) for i in range(n)]

    def store(n, j, qo, q_dec, mn, g_tot):
        r0 = pl.multiple_of(n * sc, sc)
        o_ref[pl.ds(r0, sc), lanes[j]] = qo[:, :D_DV]
        q_eff = (q_dec - qo[:, D_DV:]).astype(BF16)
        for t in range(GDN_BATCH):
            n_s[j, n * GDN_BATCH + t] = mn[t][:, :D_DV]
            mc_s[j, n * GDN_BATCH + t, :D_DK, :] = mn[t][:, D_DV:].astype(BF16)
            mc_s[j, n * GDN_BATCH + t, D_DK:, :] = q_eff[t * c:(t + 1) * c]
            gt_s[j, n * GDN_BATCH + t] = jnp.broadcast_to(g_tot[t], (8, LANES))

    def prepare_some(i, carry):
        items = [(i * GDN_UNROLL + u, j) for u in range(GDN_UNROLL) for j in range(GDN_HEADS)]
        results = solve([prepare(*operands) for operands in [load(n, j) for n, j in items]])
        for (n, j), res in zip(items, results):
            store(n, j, *res)
        return carry

    lax.fori_loop(0, n_chunks // (GDN_BATCH * GDN_UNROLL), prepare_some, 0)

    def advance(n, states):
        r0 = pl.multiple_of(n * c, c)
        operands = [(o_ref[pl.ds(r0, c), lanes[j]], gt_s[j, n], mc_s[j, n], n_s[j, n]) for j in range(GDN_HEADS)]
        prods = [_dot(operands[j][2], states[j].astype(BF16)) for j in range(GDN_HEADS)]
        out = [states[j] * operands[j][1][0:1, :] - prods[j][:D_DK] + operands[j][3] for j in range(GDN_HEADS)]
        for j in range(GDN_HEADS):
            o_ref[pl.ds(r0, c), lanes[j]] = operands[j][0] + prods[j][D_DK:]
        return tuple(out)

    lax.fori_loop(0, n_chunks, advance, tuple(jnp.zeros((D_DK, D_DV), F32) for _ in range(GDN_HEADS)))
    for j in range(GDN_HEADS):
        o = o_ref[:, lanes[j]]
        o_ref[:, lanes[j]] = o * lax.rsqrt(jnp.mean(o * o, axis=-1, keepdims=True) + NORM_EPS) * ng


def _gated_deltanet(qkv, gates, conv_w, a_log, dt_bias, norm_g):
    b, s, _ = qkv.shape
    hp, width = GDN_HEADS, GDN_HEADS * LANES
    groups = D_HEADS // hp
    n_chunks = s // DN_CHUNK
    assert n_chunks % (GDN_BATCH * GDN_UNROLL) == 0
    col = lambda off: pl.BlockSpec((None, s, width), lambda i, h: (i, 0, off + h))
    wcol = lambda off: pl.BlockSpec((CONV_WIDTH, width), lambda i, h: (0, off + h))
    scalar = pl.BlockSpec((hp, 1, 1), lambda i, h: (h, 0, 0))
    return pl.pallas_call(
        _gdn_kernel, grid=(b, groups),
        in_specs=[col(0), col(groups), col(2 * groups), wcol(0), wcol(groups), wcol(2 * groups),
                  pl.BlockSpec((None, hp, s, 2), lambda i, h: (i, h, 0, 0)), scalar, scalar,
                  pl.BlockSpec((1, D_DV), lambda i, h: (0, 0))],
        out_specs=pl.BlockSpec((None, s, width), lambda i, h: (i, 0, h)),
        out_shape=jax.ShapeDtypeStruct((b, s, D_HEADS * D_DV), F32),
        scratch_shapes=[pltpu.VMEM((hp, s, LANES), F32)] * 3 + [pltpu.VMEM((hp, s, 1), F32)] * 2
        + [pltpu.VMEM((hp, n_chunks, D_DK + DN_CHUNK, D_DV), BF16), pltpu.VMEM((hp, n_chunks, D_DK, D_DV), F32),
           pltpu.VMEM((hp, n_chunks, 8, LANES), F32)],
        compiler_params=_cparams(("parallel", "parallel")), name="gated_deltanet",
    )(qkv, qkv, qkv, conv_w, conv_w, conv_w, gates,
      a_log.astype(F32).reshape(D_HEADS, 1, 1), dt_bias.astype(F32).reshape(D_HEADS, 1, 1),
      norm_g.astype(F32).reshape(1, D_DV))


def _out_kernel(*refs, odd, final):
    x_ref, mg_ref, gate_ref, w_ref = refs[:4]
    o_ref = refs[-1]
    rest = list(refs[4:-1])
    fin_ref = rest.pop() if final else None
    half = w_ref.shape[0] // 2
    sg = _silu(gate_ref[...].astype(F32))
    if odd:
        cmp_ref, slc_ref, win_ref, od_ref, small_ref = rest
        lane = lax.broadcasted_iota(jnp.int32, (1, LANES), 1)
        bg = jax.nn.sigmoid(small_ref[...])
        blocks = []
        for p_blk in range(C_HPG):
            sl = slice(p_blk * LANES, (p_blk + 1) * LANES)
            acc = 0.0
            for br, ref in enumerate((cmp_ref, slc_ref, win_ref)):
                ca = 2 * D_HEADS + p_blk * N_BRANCH + br
                cb = 2 * D_HEADS + (p_blk + C_HPG) * N_BRANCH + br
                acc = acc + jnp.where(lane < HEAD_DIM, bg[:, ca:ca + 1], bg[:, cb:cb + 1]) * ref[:, sl]
            blocks.append(acc)
        first = jnp.concatenate(blocks, axis=1)
        second = od_ref[...]
    else:
        first, second = rest[0][...], rest[1][...]
    y = (_dot((first * sg[:, :half]).astype(BF16), w_ref[:half, :])
         + _dot((second * sg[:, half:]).astype(BF16), w_ref[half:, :]))
    out = x_ref[...] + mg_ref[...] * y
    if final:
        out = out * lax.rsqrt(jnp.mean(out * out, axis=-1, keepdims=True) + NORM_EPS) * fin_ref[...]
    o_ref[...] = out


def _out_projection(x, mod_l, gate, w, branches, final_g=None):
    b, s, d = x.shape
    ts = min(TOKEN_TILE, s)
    odd = len(branches) > 2
    row = lambda width: pl.BlockSpec((None, ts, width), lambda i, j: (i, j, 0))
    in_specs = [row(d), pl.BlockSpec((None, None, 1, d), lambda i, j: (i, 2, 0, 0)), row(gate.shape[-1]),
                pl.BlockSpec(w.shape, lambda i, j: (0, 0), pipeline_mode=pl.Buffered(1))]
    in_specs += [row(a.shape[-1]) for a in branches]
    args = [x, mod_l, gate, w, *branches]
    if final_g is not None:
        in_specs.append(pl.BlockSpec((1, d), lambda i, j: (0, 0)))
        args.append(final_g.reshape(1, d))
    return pl.pallas_call(
        functools.partial(_out_kernel, odd=odd, final=final_g is not None), grid=(b, s // ts),
        in_specs=in_specs, out_specs=row(d), out_shape=jax.ShapeDtypeStruct((b, s, d), F32),
        compiler_params=_cparams(("parallel", "parallel")), name="gated_out_proj",
    )(*args)


def _pair_cols(a0, b0):
    a, bb = np.arange(a0, a0 + HEAD_DIM), np.arange(b0, b0 + HEAD_DIM)
    return np.concatenate([a[:HALF], bb[:HALF], a[HALF:], bb[HALF:]])


def _paired_head_order(width):
    pairs = [np.concatenate([np.arange(p * HEAD_DIM, (p + 1) * HEAD_DIM),
                             np.arange((p + C_HPG) * HEAD_DIM, (p + C_HPG + 1) * HEAD_DIM)]) for p in range(C_HPG)]
    return np.concatenate(pairs + [np.arange(C_HEADS * HEAD_DIM, width)])


def _even_layout():
    aq, ak, av = 0, 512, 1024
    bq, bk, bv, bf, gate = 1536, 2048, 2560, 3072, 3080
    zero = gate + 1024
    cols = [_pair_cols(ak + 2 * h * HEAD_DIM, ak + (2 * h + 1) * HEAD_DIM) for h in range(A_HEADS)]
    cols.append(np.arange(bk, bk + 512))
    cols.append(np.concatenate([np.repeat(np.arange(bf, bf + B_HEADS), BIAS_PIECES),
                                np.full(LANES - B_HEADS * BIAS_PIECES, zero)]))
    cols.append(np.arange(gate, gate + 1024))
    segs = (("rope", 0, 512), ("bf16", 512, 1024), ("f32", 1024, 1152), ("bf16", 1152, 2176))
    rows = [_pair_cols(aq + 2 * h * HEAD_DIM, aq + (2 * h + 1) * HEAD_DIM) for h in range(A_HEADS)]
    rows += [np.arange(bq, bq + 512), np.arange(av, av + 512), np.arange(bv, bv + 512)]
    tsegs = (("rope", 0, 512), ("bf16", 512, 2048))
    return np.concatenate(cols), segs, np.concatenate(rows), tsegs


def _odd_layout():
    cq, kc, vc, ks, vs, kw, vw, cg = 0, 512, 640, 768, 896, 1024, 1152, 1280
    dq, da, db, gate = 1304, 2840, 2844, 2848
    zero = gate + 1024
    cols = [_pair_cols(ks, ks + HEAD_DIM), _pair_cols(kw, kw + HEAD_DIM), np.arange(kc, kc + 256)]
    small = np.concatenate([np.arange(da, da + 2 * D_HEADS), np.arange(cg, cg + C_HEADS * N_BRANCH)])
    cols.append(np.concatenate([small, np.full(LANES - small.size, zero)]))
    cols += [np.arange(dq, dq + 1536), gate + _paired_head_order(1024)]
    segs = (("rope", 0, 256), ("f32", 256, 512), ("f32", 512, 640), ("f32", 640, 2176), ("bf16", 2176, 3200))
    rows = [_pair_cols(cq + p * HEAD_DIM, cq + (p + C_HPG) * HEAD_DIM) for p in range(C_HPG)]
    rows += [np.arange(vs, vs + LANES), np.arange(vw, vw + LANES)]
    tsegs = (("rope+raw", 0, 512), ("bf16", 512, 768))
    return np.concatenate(cols), segs, np.concatenate(rows), tsegs


def _layout_weights(w, cols, rows, n_query_rows):
    w = jnp.concatenate([w, jnp.zeros((w.shape[0], 1), w.dtype)], axis=1)
    scale = jnp.where(jnp.arange(rows.size) < n_query_rows, QK_SCALE, 1.0).astype(w.dtype)
    return w[:, cols].astype(BF16), (w[:, rows] * scale).T.astype(BF16)


def _compress_weights(pe, w1, w2, for_keys):
    half = L_CMP // 2 * HEAD_DIM
    w1ab = jnp.concatenate([w1[:half], w1[half:]], axis=1)
    w1ab = w1ab.reshape(CMP_STRIDE, HEAD_DIM, 2 * CMP_HIDDEN)
    zeros = jnp.zeros_like(w1ab)
    w1_by_tok = jnp.concatenate([jnp.concatenate([w1ab, zeros], axis=2),
                                 jnp.concatenate([zeros, w1ab], axis=2)], axis=1).astype(BF16)
    w2p = jnp.zeros((C_GROUPS, CMP_HIDDEN, LANES), F32)
    for g in range(C_GROUPS):
        if for_keys:
            w2p = w2p.at[g, :, g * HALF:(g + 1) * HALF].set(w2[:, :HALF])
            w2p = w2p.at[g, :, HEAD_DIM + g * HALF:HEAD_DIM + (g + 1) * HALF].set(w2[:, HALF:])
        else:
            w2p = w2p.at[g, :, g * HEAD_DIM:(g + 1) * HEAD_DIM].set(w2)
    if not for_keys:
        w2p = w2p.transpose(0, 2, 1)
    return [w1_by_tok, w1.astype(F32), pe.astype(F32).reshape(1, L_CMP * HEAD_DIM), w2p.astype(BF16)]


def _even_layer(x, mod_l, rope, layer_idx, g, w_in, b_forget, lq1, lk1, lq2, lk2, subln_g, w_out, final_g):
    cols, segs, rows, tsegs = _even_layout()
    w, wt = _layout_weights(w_in, cols, rows, 2 * A_HEADS * HEAD_DIM + B_HEADS * HEAD_DIM)
    k_a, k_b, forget_logits, gate, qt_a, rest_t = _projection(x, mod_l, g, rope, w, wt, segs, tsegs)
    lam_init = 0.8 - 0.6 * math.exp(-0.3 * layer_idx)
    vec = lambda a: a.astype(F32).reshape(1, -1)
    const = lambda shape: (lambda tq, tk, tile: pl.BlockSpec(shape, lambda i, hb, *t: (0, 0)))
    nb = B_HEADS // 2
    per_step = 2
    oa = _flash("diff", qt_a, k_a, rest_t, A_HEADS, per_step, 0, 0, nb, False,
                extra=[vec(lq1), vec(lk1), vec(lq2), vec(lk2), subln_g.astype(F32).reshape(LANES, 1)],
                extra_specs=[const((1, HEAD_DIM))] * 4 + [const((LANES, 1))], lam_init=lam_init)
    bias = _forget_cumsum(forget_logits, b_forget)
    bias_spec = lambda tq, tk, tile: pl.BlockSpec((None, tk, LANES), lambda i, hb, *t: (i, tile(*t)[1], 0))
    ob = _flash("fox", rest_t, k_b, rest_t, nb, per_step, 0, 0, nb + A_HEADS, False,
                extra=[bias], extra_specs=[bias_spec])
    return _out_projection(x, mod_l, gate, w_out.astype(BF16), [oa, ob], final_g)


def _odd_layer(x, mod_l, rope, g, w_in, pe_k, pe_v, w1_k, w2_k, w1_v, w2_v, conv_w, a_log, dt_bias,
               dn_norm_g, w_out, final_g):
    b, s, _ = x.shape
    cols, segs, rows, tsegs = _odd_layout()
    w, wt = _layout_weights(w_in, cols, rows, C_HEADS * HEAD_DIM)
    k_rot, cmp_in, small, dqkv, gate, q_raw_t, q_rot_t, v_t = _projection(x, mod_l, g, rope, w, wt, segs, tsegs)
    n_cmp = (s - L_CMP) // CMP_STRIDE + 1
    kcmp, vcmp_t = _compress(cmp_in, _compress_weights(pe_k, w1_k, w2_k, True),
                             _compress_weights(pe_v, w1_v, w2_v, False))
    o_cmp, sel = _cmp_select(q_raw_t, kcmp, vcmp_t, n_cmp)
    sel_spec = lambda tq, tk, tile: pl.BlockSpec((None, LANES, tq), lambda i, hb, *t: (i, 0, tile(*t)[0]))
    o_slc = _flash("sel", q_rot_t, k_rot, v_t, C_HPG, C_HPG, 0, 0, 0, True, extra=[sel], extra_specs=[sel_spec])
    o_win = _flash("win", q_rot_t, k_rot, v_t, C_HPG, C_HPG, 0, 1, 1, True)
    gates = small[:, :, :2 * D_HEADS].reshape(b, s, 2, D_HEADS).transpose(0, 3, 1, 2)
    od = _gated_deltanet(dqkv, gates, conv_w.astype(F32), a_log, dt_bias, dn_norm_g)
    rows = _paired_head_order(w_out.shape[0])
    return _out_projection(x, mod_l, gate, w_out[rows].astype(BF16), [o_cmp, o_slc, o_win, od, small], final_g)


def kernel(x, c, positions, norm_g, w_mod, b_mod, w_out, final_norm_g, w_in_even, b_forget, lambda_q1, lambda_k1,
           lambda_q2, lambda_k2, subln_g, w_in_odd, cmp_pe_k, cmp_pe_v, cmp_w1_k, cmp_w2_k, cmp_w1_v, cmp_w2_v,
           conv_w, a_log, dt_bias, dn_norm_g):
    depth = norm_g.shape[0]
    rope = _rope_tables(positions)
    mod = _modulation(c, w_mod, b_mod)
    for l in range(depth):
        final_g = final_norm_g if l == depth - 1 else None
        i = l // 2
        if l % 2 == 0:
            x = _even_layer(x, mod[l], rope, l, norm_g[l], w_in_even[i], b_forget[i], lambda_q1[i],
                            lambda_k1[i], lambda_q2[i], lambda_k2[i], subln_g[i], w_out[l], final_g)
        else:
            x = _odd_layer(x, mod[l], rope, norm_g[l], w_in_odd[i], cmp_pe_k[i], cmp_pe_v[i], cmp_w1_k[i],
                           cmp_w2_k[i], cmp_w1_v[i], cmp_w2_v[i], conv_w[i], a_log[i], dt_bias[i], dn_norm_g[i],
                           w_out[l], final_g)
    return x
```

```python
import functools
import math

import jax
import jax.numpy as jnp
import numpy as np
from jax import lax
from jax.experimental import pallas as pl
from jax.experimental.pallas import tpu as pltpu

F32 = jnp.float32
BF16 = jnp.bfloat16
HI = lax.Precision.HIGHEST

LANES = 128
HEAD_DIM = 64
HALF = HEAD_DIM // 2
ROPE_THETA = 10000.0
NORM_EPS = 1e-6
NEG = -1e30
MASK_BIG = 1e30
LOG2E = math.log2(math.e)
QK_SCALE = HEAD_DIM ** -0.5 * LOG2E
BIAS_PIECES = 3
NORM_ROWS = 16
A_HEADS = 4
B_HEADS = 8
C_HEADS = 8
C_GROUPS = 2
C_HPG = C_HEADS // C_GROUPS
L_CMP = 32
CMP_STRIDE = 16
CMP_HIDDEN = 256
L_SEL = 64
N_SEL = 8
WINDOW = 512
N_BRANCH = 3
FORCE_BONUS = 1e4
D_HEADS = 4
D_DK = 128
D_DV = 128
CONV_WIDTH = 4
DN_CHUNK = 64
TOKEN_TILE = 512
VMEM_LIMIT = 56 * 1024 * 1024


def _cparams(sem):
    return pltpu.CompilerParams(dimension_semantics=sem, vmem_limit_bytes=VMEM_LIMIT)


def _dot(a, b, precision=None):
    return jnp.dot(a, b, precision=precision, preferred_element_type=F32)


def _dot_nt(a, b, precision=None):
    return lax.dot_general(a, b, (((1,), (1,)), ((), ())), precision=precision, preferred_element_type=F32)


def _dot_tn(a, b, precision=None):
    return lax.dot_general(a, b, (((0,), (0,)), ((), ())), precision=precision, preferred_element_type=F32)


def _split_bf16(a):
    hi = a.astype(BF16)
    return hi, (a - hi.astype(F32)).astype(BF16)


def _softplus(z):
    return jnp.maximum(z, 0.0) + jnp.log1p(jnp.exp(-jnp.abs(z)))


def _silu(z):
    return z * jax.nn.sigmoid(z)


def _rope_table_kernel(pos_ref, inv_ref, cos_ref, sin_ref, cos_t_ref, sin_t_ref):
    ang = pos_ref[...].astype(F32) * inv_ref[...]
    lane = lax.broadcasted_iota(jnp.int32, (1, LANES), 1)
    cos = jnp.cos(ang)
    sin = jnp.where(lane < 2 * HALF, -1.0, 1.0) * jnp.sin(ang)
    cos_ref[...] = cos
    sin_ref[...] = sin
    cos_t_ref[...] = cos.T
    sin_t_ref[...] = sin.T


def _rope_tables(positions):
    b, s = positions.shape
    inv = ROPE_THETA ** (-jnp.arange(0, HEAD_DIM, 2, dtype=F32) / HEAD_DIM)
    inv = jnp.tile(inv, 4).reshape(1, LANES)
    tok = jax.ShapeDtypeStruct((b, s, LANES), F32)
    feat = jax.ShapeDtypeStruct((b, LANES, s), F32)
    return pl.pallas_call(
        _rope_table_kernel, grid=(b,),
        in_specs=[pl.BlockSpec((None, s, 1), lambda i: (i, 0, 0)),
                  pl.BlockSpec((1, LANES), lambda i: (0, 0))],
        out_specs=[pl.BlockSpec((None, s, LANES), lambda i: (i, 0, 0))] * 2
        + [pl.BlockSpec((None, LANES, s), lambda i: (i, 0, 0))] * 2,
        out_shape=[tok, tok, feat, feat], compiler_params=_cparams(("parallel",)), name="rope_tables",
    )(positions.reshape(b, s, 1), inv)


def _mod_kernel(c_ref, w_ref, b_ref, o_ref):
    o_ref[...] = _dot(_silu(c_ref[...]), w_ref[...], HI) + b_ref[...]


def _modulation(c, w_mod, b_mod):
    depth, d, n = w_mod.shape
    b = c.shape[0]
    tn = 1024
    mod = pl.pallas_call(
        _mod_kernel, grid=(depth, n // tn),
        in_specs=[pl.BlockSpec((b, d), lambda l, j: (0, 0)),
                  pl.BlockSpec((None, d, tn), lambda l, j: (l, 0, j)),
                  pl.BlockSpec((None, 1, tn), lambda l, j: (l, 0, j))],
        out_specs=pl.BlockSpec((None, b, tn), lambda l, j: (l, 0, j)),
        out_shape=jax.ShapeDtypeStruct((depth, b, n), F32),
        compiler_params=_cparams(("parallel", "parallel")), name="modulation",
    )(c, w_mod, b_mod.reshape(depth, 1, n))
    return mod.reshape(depth, b, 3, 1, d)


def _proj_kernel(x_ref, shift_ref, scale_ref, g_ref, cos_ref, sin_ref, cos_t_ref, sin_t_ref, w_ref, wt_ref,
                 *rest, segs, tsegs, deltanet):
    x = x_ref[...]
    h = x * lax.rsqrt(jnp.mean(x * x, axis=-1, keepdims=True) + NORM_EPS) * g_ref[...]
    h = (h * (1.0 + scale_ref[...]) + shift_ref[...]).astype(BF16)
    if deltanet:
        conv_ref, gparam_ref, halo_ref = rest[0], rest[1], rest[-1]
        outs = list(rest[2:-1])

        @pl.when(pl.program_id(1) == 0)
        def _():
            halo_ref[...] = jnp.zeros(halo_ref.shape, F32)
    else:
        outs = list(rest)
    ts = x.shape[0]
    step = 4 * LANES
    conv_off = 0
    for kind, c0, c1 in segs:
        o_ref = outs.pop(0)
        for a in range(c0, c1, step):
            e = min(a + step, c1)
            acc = _dot(h, w_ref[:, a:e])
            if kind == "rope":
                cos, sin = cos_ref[...], sin_ref[...]
                for j in range(0, e - a, LANES):
                    blk = acc[:, j:j + LANES]
                    rot = blk * cos + pltpu.roll(blk, 2 * HALF, axis=1) * sin
                    o_ref[:, a - c0 + j:a - c0 + j + LANES] = rot.astype(o_ref.dtype)
            elif kind == "dn_gates":
                lane = lax.broadcasted_iota(jnp.int32, (1, LANES), 1)
                log_decay = -jnp.exp(gparam_ref[0:1, :]) * _softplus(acc + gparam_ref[1:2, :])
                o_ref[...] = jnp.where(lane < D_HEADS, log_decay,
                                       jnp.where(lane < 2 * D_HEADS, jax.nn.sigmoid(acc), acc))
            elif kind in ("dn_q", "dn_k", "dn_v"):
                cols = slice(conv_off, conv_off + (e - a))
                conv_off += e - a
                wc = conv_ref[:, cols]
                ext = jnp.concatenate([halo_ref[:, cols], acc], axis=0)
                halo_ref[:, cols] = acc[ts - 8:, :]
                y = acc * wc[CONV_WIDTH - 1:CONV_WIDTH, :]
                for back in range(1, CONV_WIDTH):
                    y = y + ext[8 - back:8 - back + ts, :] * wc[CONV_WIDTH - 1 - back:CONV_WIDTH - back, :]
                y = _silu(y)
                for j in range(0, e - a, LANES):
                    blk = y[:, j:j + LANES]
                    if kind != "dn_v":
                        blk = blk * lax.rsqrt(jnp.sum(blk * blk, axis=-1, keepdims=True) + NORM_EPS)
                    if kind == "dn_q":
                        blk = blk * (D_DK ** -0.5)
                    o_ref[:, a - c0 + j:a - c0 + j + LANES] = blk
            else:
                o_ref[:, a - c0:e - c0] = acc.astype(o_ref.dtype)
    for kind, r0, r1 in tsegs:
        raw_ref = outs.pop(0) if kind == "rope+raw" else None
        o_ref = outs.pop(0)
        for a in range(r0, r1, step):
            e = min(a + step, r1)
            acc = _dot_nt(wt_ref[a:e, :], h)
            if raw_ref is not None:
                raw_ref[a - r0:e - r0, :] = acc
            if kind in ("rope", "rope+raw"):
                cos, sin = cos_t_ref[...], sin_t_ref[...]
                for j in range(0, e - a, LANES):
                    blk = acc[j:j + LANES, :]
                    rot = blk * cos + pltpu.roll(blk, 2 * HALF, axis=0) * sin
                    o_ref[a - r0 + j:a - r0 + j + LANES, :] = rot.astype(o_ref.dtype)
            else:
                o_ref[a - r0:e - r0, :] = acc.astype(o_ref.dtype)


def _projection(x, mod_l, g, rope, w, wt, segs, tsegs, deltanet=None):
    b, s, d = x.shape
    ts = min(TOKEN_TILE, s)
    row = lambda width: pl.BlockSpec((None, ts, width), lambda i, j: (i, j, 0))
    col = lambda height: pl.BlockSpec((None, height, ts), lambda i, j: (i, 0, j))
    out_specs, out_shapes = [], []
    for kind, c0, c1 in segs:
        out_specs.append(row(c1 - c0))
        out_shapes.append(jax.ShapeDtypeStruct((b, s, c1 - c0), BF16 if kind in ("rope", "bf16") else F32))
    for kind, r0, r1 in tsegs:
        if kind == "rope+raw":
            out_specs.append(col(r1 - r0))
            out_shapes.append(jax.ShapeDtypeStruct((b, r1 - r0, s), F32))
        out_specs.append(col(r1 - r0))
        out_shapes.append(jax.ShapeDtypeStruct((b, r1 - r0, s), BF16))
    modspec = lambda k: pl.BlockSpec((None, None, 1, d), lambda i, j: (i, k, 0, 0))
    resident = lambda a: pl.BlockSpec(a.shape, lambda i, j: (0, 0), pipeline_mode=pl.Buffered(1))
    in_specs = [row(d), modspec(0), modspec(1), pl.BlockSpec((1, d), lambda i, j: (0, 0)),
                row(LANES), row(LANES), col(LANES), col(LANES), resident(w), resident(wt)]
    args = [x, mod_l, mod_l, g.reshape(1, d), *rope, w, wt]
    scratch = []
    if deltanet is not None:
        in_specs += [pl.BlockSpec(a.shape, lambda i, j: (0, 0)) for a in deltanet]
        args += list(deltanet)
        scratch = [pltpu.VMEM((8, deltanet[0].shape[1]), F32)]
    sem = ("parallel", "arbitrary") if deltanet is not None else ("parallel", "parallel")
    return pl.pallas_call(
        functools.partial(_proj_kernel, segs=segs, tsegs=tsegs, deltanet=deltanet is not None), grid=(b, s // ts),
        in_specs=in_specs, out_specs=out_specs, out_shape=out_shapes, scratch_shapes=scratch,
        compiler_params=_cparams(sem), name="adaln_in_proj",
    )(*args)


def _cum_kernel(x_ref, bias_ref, o_ref):
    s = x_ref.shape[0]
    ii = lax.broadcasted_iota(jnp.int32, (LANES, LANES), 0)
    jj = lax.broadcasted_iota(jnp.int32, (LANES, LANES), 1)
    lower = (ii >= jj).astype(F32)
    lane = lax.broadcasted_iota(jnp.int32, (1, LANES), 1)
    carry = jnp.zeros((1, LANES), F32)
    for r0 in range(0, s, LANES):
        z = x_ref[r0:r0 + LANES, :] + bias_ref[...]
        logf = jnp.minimum(z, 0.0) - jnp.log1p(jnp.exp(-jnp.abs(z)))
        loc = _dot(lower, logf, HI) + carry
        carry = loc[LANES - 1:LANES, :]
        val = loc * (-LOG2E)
        hi = val.astype(BF16)
        rest = val - hi.astype(F32)
        mid = rest.astype(BF16)
        lo = (rest - mid.astype(F32)).astype(BF16)
        piece = jnp.where(lane % BIAS_PIECES == 0, hi, jnp.where(lane % BIAS_PIECES == 1, mid, lo))
        o_ref[r0:r0 + LANES, :] = jnp.where(lane < B_HEADS * BIAS_PIECES, piece, jnp.zeros_like(piece))


def _forget_cumsum(logits, b_forget):
    b, s, _ = logits.shape
    bias = jnp.zeros((1, LANES), F32).at[0, :B_HEADS * BIAS_PIECES].set(jnp.repeat(b_forget.astype(F32), BIAS_PIECES))
    return pl.pallas_call(
        _cum_kernel, grid=(b,),
        in_specs=[pl.BlockSpec((None, s, LANES), lambda i: (i, 0, 0)), pl.BlockSpec((1, LANES), lambda i: (0, 0))],
        out_specs=pl.BlockSpec((None, s, LANES), lambda i: (i, 0, 0)),
        out_shape=jax.ShapeDtypeStruct((b, s, LANES), BF16),
        compiler_params=_cparams(("parallel",)), name="forget_cumsum",
    )(logits, bias)


def _triangle_step(t, nq):
    qi = sum((t >= j * (j + 1) // 2).astype(jnp.int32) for j in range(1, nq))
    return qi, t - qi * (qi + 1) // 2


def _flash_kernel(*refs, mode, tq, tk, nq, nblk, shared_kv, lam_init):
    qt_ref, k_ref, vt_ref = refs[:3]
    m_ref, acc_ref = refs[-2:]
    o_ref = refs[-3]
    extra = refs[3:-3]
    if mode == "win":
        qi, ki = pl.program_id(2), pl.program_id(3)
        kv, last = qi - 1 + ki, ki == pl.num_programs(3) - 1
    else:
        qi, ki = _triangle_step(pl.program_id(2), nq)
        kv, last = ki, ki == qi
    row = lax.broadcasted_iota(jnp.int32, (LANES, 1), 0)
    if mode == "fox":
        slot_rows = (row < HEAD_DIM, row >= HEAD_DIM)
    else:
        slot_rows = ((row % HEAD_DIM) < HALF, (row % HEAD_DIM) >= HALF)
    vrows = acc_ref.shape[1] - NORM_ROWS
    block = lambda j: slice(j * LANES, (j + 1) * LANES)
    half = tq // 2

    @pl.when(ki == 0)
    def _():
        m_ref[...] = jnp.full(m_ref.shape, NEG, F32)
        acc_ref[...] = jnp.zeros(acc_ref.shape, F32)

    def step(kind):
        rr = lax.broadcasted_iota(jnp.int32, (LANES, tq), 0)
        k_extra = [None, None]
        for s in range(2):
            if mode == "sel":
                cb = (kv * tk + lax.broadcasted_iota(jnp.int32, (tk, LANES), 0)) // L_SEL
                ll = lax.broadcasted_iota(jnp.int32, (tk, LANES), 1)
                k_extra[s] = jnp.where(ll == cb + s * (LANES // 4), MASK_BIG, 0.0).astype(BF16)

        def q_extra(j, s):
            if mode == "sel":
                return extra[0][...]
            if mode == "fox":
                first = ((pl.program_id(1) * nblk + j) * 2 + s) * BIAS_PIECES
                return jnp.where((rr >= first) & (rr < first + BIAS_PIECES), 1.0, 0.0).astype(BF16)
            return None
        lo, hi, everything = slice(0, half), slice(half, tq), slice(0, tq)
        parts = {"full": [(everything, everything, None)],
                 "lower": [(lo, lo, lo), (everything, hi, hi)],
                 "upper": [(everything, lo, lo), (hi, hi, hi)]}[kind]
        tri_k = lax.broadcasted_iota(jnp.int32, (half, half), 0)
        tri_q = lax.broadcasted_iota(jnp.int32, (half, half), 1)
        tri = tri_k <= tri_q if kind == "lower" else tri_k > tri_q
        ones = jnp.ones((NORM_ROWS, tk), BF16)
        sts, vts = [], []
        for j in range(nblk):
            qt = qt_ref[block(j), :]
            k = k_ref[...] if shared_kv else k_ref[:, block(j)]
            base = 0 if shared_kv else j * LANES
            for s in range(2):
                qs = jnp.where(slot_rows[s], qt, jnp.zeros_like(qt))
                q_more = q_extra(j, s)
                if q_more is not None:
                    k_more = extra[0][...] if mode == "fox" else k_extra[s]
                    k_all, q_all = jnp.concatenate([k, k_more], axis=1), jnp.concatenate([qs, q_more], axis=0)
                else:
                    k_all, q_all = k, qs
                sts.append([_dot(k_all[krows], q_all[:, lanes]) for krows, lanes, _ in parts])
                v0 = base if vrows == LANES else base + s * vrows
                vts.append(jnp.concatenate([vt_ref[v0:v0 + vrows, :], ones], axis=0))
        for idx in range(2 * nblk):
            for (krows, lanes, tri_rows), st in zip(parts, sts[idx]):
                if tri_rows is not None:
                    if krows == tri_rows:
                        st = jnp.where(tri, st, NEG)
                    elif tri_rows == lo:
                        st = jnp.concatenate([jnp.where(tri, st[:half], NEG), st[half:]], axis=0)
                    else:
                        st = jnp.concatenate([st[:half], jnp.where(tri, st[half:], NEG)], axis=0)
                m_prev = m_ref[idx, :, lanes]
                m_new = jnp.maximum(m_prev, jnp.max(st, axis=0, keepdims=True))
                alpha = jnp.exp2(m_prev - m_new)
                p = jnp.exp2(st - m_new).astype(BF16)
                acc_ref[idx, :, lanes] = alpha * acc_ref[idx, :, lanes] + _dot(vts[idx][:, krows], p)
                m_ref[idx, :, lanes] = m_new

    if mode == "win":
        pl.when((ki == 0) & (kv >= 0))(functools.partial(step, "upper"))
        pl.when(ki == 1)(functools.partial(step, "lower"))
    else:
        pl.when(ki < qi)(functools.partial(step, "full"))
        pl.when(ki == qi)(functools.partial(step, "lower"))

    @pl.when(last)
    def _():
        for j in range(nblk):
            o0 = acc_ref[2 * j, :vrows, :] / acc_ref[2 * j, vrows:vrows + 1, :]
            o1 = acc_ref[2 * j + 1, :vrows, :] / acc_ref[2 * j + 1, vrows:vrows + 1, :]
            if mode == "diff":
                lq1, lk1, lq2, lk2, subg = (r[...] for r in extra)
                lam = (jnp.exp(jnp.sum(lq1 * lk1, axis=1, keepdims=True))
                       - jnp.exp(jnp.sum(lq2 * lk2, axis=1, keepdims=True)) + lam_init)
                o = o0 - lam * o1
                o = o * lax.rsqrt(jnp.mean(o * o, axis=0, keepdims=True) + NORM_EPS) * subg * (1.0 - lam_init)
            else:
                o = jnp.concatenate([o0, o1], axis=0)
            o_ref[:, block(j)] = o.T.astype(o_ref.dtype)


def _flash(mode, qt, k, vt, n_qblocks, nblk, q0, k0, v0, shared_kv, extra=(), extra_specs=(), lam_init=0.0):
    b, s, _ = k.shape
    tq = tk = min(TOKEN_TILE, s)
    nq = s // tq
    if mode == "win":
        assert WINDOW == tk
        steps, sem = (nq, 2), ("parallel", "arbitrary")
        tile = lambda qi, ki: (qi, jnp.maximum(qi - 1 + ki, 0))
    else:
        steps, sem = (nq * (nq + 1) // 2,), ("arbitrary",)
        tile = lambda t: _triangle_step(t, nq)
    assert n_qblocks % nblk == 0 and q0 % nblk == 0 and (shared_kv or (k0 % nblk == 0 and v0 % nblk == 0))
    wide = nblk * LANES
    if shared_kv:
        k_spec = pl.BlockSpec((None, tk, LANES), lambda i, hb, *t: (i, tile(*t)[1], k0))
        v_spec = pl.BlockSpec((None, LANES, tk), lambda i, hb, *t: (i, v0, tile(*t)[1]))
    else:
        k_spec = pl.BlockSpec((None, tk, wide), lambda i, hb, *t: (i, tile(*t)[1], k0 // nblk + hb))
        v_spec = pl.BlockSpec((None, wide, tk), lambda i, hb, *t: (i, v0 // nblk + hb, tile(*t)[1]))
    in_specs = [pl.BlockSpec((None, wide, tq), lambda i, hb, *t: (i, q0 // nblk + hb, tile(*t)[0])), k_spec, v_spec]
    for spec in extra_specs:
        in_specs.append(spec(tq, tk, tile))
    vrows = LANES if mode == "diff" else HEAD_DIM
    return pl.pallas_call(
        functools.partial(_flash_kernel, mode=mode, tq=tq, tk=tk, nq=nq, nblk=nblk, shared_kv=shared_kv,
                          lam_init=lam_init),
        grid=(b, n_qblocks // nblk) + steps, in_specs=in_specs,
        out_specs=pl.BlockSpec((None, tq, wide), lambda i, hb, *t: (i, tile(*t)[0], hb)),
        out_shape=jax.ShapeDtypeStruct((b, s, n_qblocks * LANES), BF16),
        scratch_shapes=[pltpu.VMEM((2 * nblk, 1, tq), F32), pltpu.VMEM((2 * nblk, vrows + NORM_ROWS, tq), F32)],
        compiler_params=_cparams(("parallel", "parallel") + sem),
        name="flash_" + mode,
    )(qt, k, vt, *extra)


def _compress_kernel(xk_ref, xv_ref, w1_by_tok_k, w1_k, pe_k, w2_k, w1_by_tok_v, w1_v, pe_v, w2_v, ok_ref, ov_ref):
    cn = xk_ref.shape[0] // CMP_STRIDE

    def run(x_ref, w1_by_tok, w1, pe, w2, o_ref, transposed):
        pe_term = _dot(pe[...], w1[...], HI)
        ab = jnp.zeros((cn, C_GROUPS * 2 * CMP_HIDDEN), F32)
        for tok in range(CMP_STRIDE):
            rows = x_ref[pl.ds(tok, cn, stride=CMP_STRIDE), :]
            ab = ab + _dot(rows.astype(BF16), w1_by_tok[tok])
        out = 0.0
        for g in range(C_GROUPS):
            first = ab[:, g * 2 * CMP_HIDDEN:(g * 2 + 1) * CMP_HIDDEN]
            second = ab[:, (g * 2 + 1) * CMP_HIDDEN:(g + 1) * 2 * CMP_HIDDEN]
            hid = first + pltpu.roll(second, cn - 1, axis=0) + pe_term
            act = _silu(hid).astype(BF16)
            out = out + (_dot_nt(w2[g], act) if transposed else _dot(act, w2[g]))
        o_ref[...] = out
    run(xk_ref, w1_by_tok_k, w1_k, pe_k, w2_k, ok_ref, False)
    run(xv_ref, w1_by_tok_v, w1_v, pe_v, w2_v, ov_ref, True)


def _compress(cmp_in, weights_k, weights_v):
    b, s, width = cmp_in.shape
    cn = s // CMP_STRIDE
    full = lambda a: pl.BlockSpec(a.shape, lambda i: (0,) * a.ndim)
    return pl.pallas_call(
        _compress_kernel, grid=(b,),
        in_specs=[pl.BlockSpec((None, s, LANES), lambda i: (i, 0, 0)), pl.BlockSpec((None, s, LANES), lambda i: (i, 0, 1))]
        + [full(a) for a in weights_k + weights_v],
        out_specs=[pl.BlockSpec((None, cn, LANES), lambda i: (i, 0, 0)),
                   pl.BlockSpec((None, LANES, cn), lambda i: (i, 0, 0))],
        out_shape=[jax.ShapeDtypeStruct((b, cn, LANES), F32), jax.ShapeDtypeStruct((b, LANES, cn), F32)],
        compiler_params=_cparams(("parallel",)), name="nsa_compress",
    )(cmp_in, cmp_in, *weights_k, *weights_v)


def _cmp_select_kernel(qt_ref, kc_ref, vct_ref, o_ref, sel_ref, *, tq, n_blk, n_cmp):
    qi = pl.program_id(1)
    cn = kc_ref.shape[0]
    kc_hi, kc_lo = _split_bf16(kc_ref[...])
    vct = vct_ref[...].astype(BF16)
    row = lax.broadcasted_iota(jnp.int32, (LANES, 1), 0)
    slot_rows = ((row % HEAD_DIM) < HALF, (row % HEAD_DIM) >= HALF)
    t_row = qi * tq + lax.broadcasted_iota(jnp.int32, (1, tq), 1)
    m_col = lax.broadcasted_iota(jnp.int32, (cn, 1), 0)
    valid = (m_col * CMP_STRIDE + L_CMP - 1 <= t_row) & (m_col < n_cmp)
    any_valid = (t_row >= L_CMP - 1).astype(F32)
    psum = [jnp.zeros((cn, tq), F32) for _ in range(C_GROUPS)]
    for p_blk in range(C_HPG):
        qt = qt_ref[p_blk * LANES:(p_blk + 1) * LANES, :]
        outs = []
        for s in range(C_GROUPS):
            q_hi, q_lo = _split_bf16(jnp.where(slot_rows[s], qt, 0.0))
            sc = jnp.where(valid, _dot(kc_hi, q_hi) + _dot(kc_lo, q_hi) + _dot(kc_hi, q_lo), NEG)
            e = jnp.exp2(sc - jnp.max(sc, axis=0, keepdims=True))
            p = e / jnp.sum(e, axis=0, keepdims=True) * any_valid
            psum[s] = psum[s] + p
            outs.append(_dot(vct[s * HEAD_DIM:(s + 1) * HEAD_DIM, :], p.astype(BF16)))
        o_ref[:, p_blk * LANES:(p_blk + 1) * LANES] = jnp.concatenate(outs, axis=0).T.astype(o_ref.dtype)
    jb = lax.broadcasted_iota(jnp.int32, (n_blk, cn), 0)
    mm = lax.broadcasted_iota(jnp.int32, (n_blk, cn), 1)
    overlap = ((mm * CMP_STRIDE < jb * L_SEL + L_SEL) & (mm * CMP_STRIDE + L_CMP > jb * L_SEL)
               & (mm < n_cmp)).astype(BF16)
    j = lax.broadcasted_iota(jnp.int32, (n_blk, tq), 0)
    cur = (qi * tq + lax.broadcasted_iota(jnp.int32, (n_blk, tq), 1)) // L_SEL
    forced = (j == 0) | (j == cur) | (j == cur - 1)
    n_top = min(N_SEL, n_blk)
    pad_rows = LANES // 4 - n_blk
    parts = []
    for s in range(C_GROUPS):
        p_hi, p_lo = _split_bf16(psum[s])
        imp = _dot(overlap, p_hi) + _dot(overlap, p_lo)
        score = jnp.where(j > cur, NEG, imp + jnp.where(forced, FORCE_BONUS, 0.0))
        rank = jnp.zeros((n_blk, tq), jnp.int32)
        for jp in range(n_blk):
            r = score[jp:jp + 1, :]
            rank = rank + ((r > score) | ((r == score) & (jp < j))).astype(jnp.int32)
        parts.append(jnp.where(rank < n_top, 0.0, -1.0))
        if pad_rows:
            parts.append(jnp.zeros((pad_rows, tq), F32))
    parts.append(jnp.zeros((LANES // 2, tq), F32))
    sel_ref[...] = jnp.concatenate(parts, axis=0).astype(sel_ref.dtype)


def _cmp_select(q_raw_t, kcmp, vcmp_t, n_cmp):
    b, width, s = q_raw_t.shape
    cn = kcmp.shape[1]
    tq = min(TOKEN_TILE, s)
    n_blk = s // L_SEL
    return pl.pallas_call(
        functools.partial(_cmp_select_kernel, tq=tq, n_blk=n_blk, n_cmp=n_cmp), grid=(b, s // tq),
        in_specs=[pl.BlockSpec((None, width, tq), lambda i, j: (i, 0, j)),
                  pl.BlockSpec((None, cn, LANES), lambda i, j: (i, 0, 0)),
                  pl.BlockSpec((None, LANES, cn), lambda i, j: (i, 0, 0))],
        out_specs=[pl.BlockSpec((None, tq, width), lambda i, j: (i, j, 0)),
                   pl.BlockSpec((None, LANES, tq), lambda i, j: (i, 0, j))],
        out_shape=[jax.ShapeDtypeStruct((b, s, width), BF16), jax.ShapeDtypeStruct((b, LANES, s), BF16)],
        compiler_params=_cparams(("parallel", "parallel")), name="nsa_cmp_select",
    )(q_raw_t, kcmp, vcmp_t)


GDN_HEADS = 2
GDN_BATCH = 2
GDN_UNROLL = 8


def _gdn_kernel(q_ref, k_ref, v_ref, gate_ref, ng_ref, o_ref, mc_s, n_s, gt_s):
    s_len = q_ref.shape[0]
    c = DN_CHUNK
    n_chunks = s_len // c
    lanes = [slice(j * LANES, (j + 1) * LANES) for j in range(GDN_HEADS)]

    sc = GDN_BATCH * c
    ii = lax.broadcasted_iota(jnp.int32, (sc, sc), 0)
    jj = lax.broadcasted_iota(jnp.int32, (sc, sc), 1)
    same = (ii // c) == (jj // c)
    causal, strict, upper, eye = same & (ii >= jj), same & (ii > jj), same & (ii <= jj), ii == jj
    chunk_end = same & (jj % c == c - 1)
    ng = ng_ref[...]

    def load(n, j):
        r0 = pl.multiple_of(n * sc, sc)
        return (q_ref[pl.ds(r0, sc), lanes[j]], k_ref[pl.ds(r0, sc), lanes[j]], v_ref[pl.ds(r0, sc), lanes[j]],
                gate_ref[j, pl.ds(r0, sc), 0:1], gate_ref[j, pl.ds(r0, sc), 1:2])

    def prepare(q, k, v, g, beta):
        g_row = jnp.sum(jnp.where(eye, g, 0.0), axis=0, keepdims=True)
        gc_col = jnp.sum(jnp.where(causal, g_row, 0.0), axis=1, keepdims=True)
        gc_row = jnp.sum(jnp.where(upper, g, 0.0), axis=0, keepdims=True)
        g_last = jnp.sum(jnp.where(chunk_end, gc_row, 0.0), axis=1, keepdims=True)
        decay = jnp.where(causal, jnp.exp(jnp.where(causal, gc_col - gc_row, 0.0)), 0.0)
        eg = jnp.exp(gc_col)
        kb = k * beta
        kbf = k.astype(BF16)
        raw = _dot_nt(kb.astype(BF16), kbf)
        raw_qk = _dot_nt(q.astype(BF16), kbf)
        rhs = jnp.concatenate([v * beta, kb * eg], axis=1).astype(BF16)
        kd = k * jnp.exp(g_last - gc_col)
        kd_t = [kd[t * c:(t + 1) * c].T.astype(BF16) for t in range(GDN_BATCH)]
        g_tot = [jnp.exp(g_last[t * c:t * c + 1]) for t in range(GDN_BATCH)]
        return raw, raw_qk, decay, rhs, kd_t, g_tot, q * eg

    def solve(prepared):
        n = len(prepared)
        decay = [p[2] for p in prepared]
        x = [-jnp.where(strict, prepared[i][0] * decay[i], 0.0) for i in range(n)]
        inv = [jnp.where(eye, 1.0, 0.0) + x[i] for i in range(n)]
        xb = [x[i].astype(BF16) for i in range(n)]
        x = [_dot(xb[i], xb[i]) for i in range(n)]
        for _ in range(int(math.log2(c)) - 2):
            xb = [x[i].astype(BF16) for i in range(n)]
            both = [_dot(jnp.concatenate([xb[i], inv[i].astype(BF16)], axis=0), xb[i]) for i in range(n)]
            x = [both[i][:sc] for i in range(n)]
            inv = [inv[i] + both[i][sc:] for i in range(n)]
        last = [_dot(inv[i].astype(BF16), x[i].astype(BF16)) for i in range(n)]
        inv = [inv[i] + last[i] for i in range(n)]
        sol = [_dot(inv[i].astype(BF16), prepared[i][3]).astype(BF16) for i in range(n)]
        qk = [(prepared[i][1] * decay[i]).astype(BF16) for i in range(n)]
        qo = [_dot(qk[i], sol[i]) for i in range(n)]
        mn = [[_dot(prepared[i][4][t], sol[i][t * c:(t + 1) * c]) for t in range(GDN_BATCH)] for i in range(n)]
        return [(qo[i], prepared[i][6], mn[i], prepared[i][5]) for i in range(n)]

    def store(n, j, qo, q_dec, mn, g_tot):
        r0 = pl.multiple_of(n * sc, sc)
        o_ref[pl.ds(r0, sc), lanes[j]] = qo[:, :D_DV]
        q_eff = (q_dec - qo[:, D_DV:]).astype(BF16)
        for t in range(GDN_BATCH):
            n_s[j, n * GDN_BATCH + t] = mn[t][:, :D_DV]
            mc_s[j, n * GDN_BATCH + t, :D_DK, :] = mn[t][:, D_DV:].astype(BF16)
            mc_s[j, n * GDN_BATCH + t, D_DK:, :] = q_eff[t * c:(t + 1) * c]
            gt_s[j, n * GDN_BATCH + t] = jnp.broadcast_to(g_tot[t], (8, LANES))

    def prepare_some(i, carry):
        items = [(i * GDN_UNROLL + u, j) for u in range(GDN_UNROLL) for j in range(GDN_HEADS)]
        results = solve([prepare(*operands) for operands in [load(n, j) for n, j in items]])
        for (n, j), res in zip(items, results):
            store(n, j, *res)
        return carry

    lax.fori_loop(0, n_chunks // (GDN_BATCH * GDN_UNROLL), prepare_some, 0)

    def advance(n, states):
        r0 = pl.multiple_of(n * c, c)
        operands = [(o_ref[pl.ds(r0, c), lanes[j]], gt_s[j, n], mc_s[j, n], n_s[j, n]) for j in range(GDN_HEADS)]
        prods = [_dot(operands[j][2], states[j].astype(BF16)) for j in range(GDN_HEADS)]
        out = [states[j] * operands[j][1][0:1, :] - prods[j][:D_DK] + operands[j][3] for j in range(GDN_HEADS)]
        for j in range(GDN_HEADS):
            o_ref[pl.ds(r0, c), lanes[j]] = operands[j][0] + prods[j][D_DK:]
        return tuple(out)

    lax.fori_loop(0, n_chunks, advance, tuple(jnp.zeros((D_DK, D_DV), F32) for _ in range(GDN_HEADS)))
    for j in range(GDN_HEADS):
        o = o_ref[:, lanes[j]]
        o_ref[:, lanes[j]] = o * lax.rsqrt(jnp.mean(o * o, axis=-1, keepdims=True) + NORM_EPS) * ng


def _gated_deltanet(q, k, v, gates, norm_g):
    b, s, _ = q.shape
    hp, width = GDN_HEADS, GDN_HEADS * LANES
    groups = D_HEADS // hp
    n_chunks = s // DN_CHUNK
    assert n_chunks % (GDN_BATCH * GDN_UNROLL) == 0
    col = pl.BlockSpec((None, s, width), lambda i, h: (i, 0, h))
    return pl.pallas_call(
        _gdn_kernel, grid=(b, groups),
        in_specs=[col, col, col, pl.BlockSpec((None, hp, s, 2), lambda i, h: (i, h, 0, 0)),
                  pl.BlockSpec((1, D_DV), lambda i, h: (0, 0))],
        out_specs=col,
        out_shape=jax.ShapeDtypeStruct((b, s, D_HEADS * D_DV), F32),
        scratch_shapes=[pltpu.VMEM((hp, n_chunks, D_DK + DN_CHUNK, D_DV), BF16),
                        pltpu.VMEM((hp, n_chunks, D_DK, D_DV), F32), pltpu.VMEM((hp, n_chunks, 8, LANES), F32)],
        compiler_params=_cparams(("parallel", "parallel")), name="gated_deltanet",
    )(q, k, v, gates, norm_g.astype(F32).reshape(1, D_DV))


def _out_kernel(*refs, odd, final):
    x_ref, mg_ref, gate_ref, w_ref = refs[:4]
    o_ref = refs[-1]
    rest = list(refs[4:-1])
    fin_ref = rest.pop() if final else None
    half = w_ref.shape[0] // 2
    sg = _silu(gate_ref[...].astype(F32))
    if odd:
        cmp_ref, slc_ref, win_ref, od_ref, small_ref = rest
        lane = lax.broadcasted_iota(jnp.int32, (1, LANES), 1)
        bg = jax.nn.sigmoid(small_ref[...])
        blocks = []
        for p_blk in range(C_HPG):
            sl = slice(p_blk * LANES, (p_blk + 1) * LANES)
            acc = 0.0
            for br, ref in enumerate((cmp_ref, slc_ref, win_ref)):
                ca = 2 * D_HEADS + p_blk * N_BRANCH + br
                cb = 2 * D_HEADS + (p_blk + C_HPG) * N_BRANCH + br
                acc = acc + jnp.where(lane < HEAD_DIM, bg[:, ca:ca + 1], bg[:, cb:cb + 1]) * ref[:, sl]
            blocks.append(acc)
        first = jnp.concatenate(blocks, axis=1)
        second = od_ref[...]
    else:
        first, second = rest[0][...], rest[1][...]
    y = (_dot((first * sg[:, :half]).astype(BF16), w_ref[:half, :])
         + _dot((second * sg[:, half:]).astype(BF16), w_ref[half:, :]))
    out = x_ref[...] + mg_ref[...] * y
    if final:
        out = out * lax.rsqrt(jnp.mean(out * out, axis=-1, keepdims=True) + NORM_EPS) * fin_ref[...]
    o_ref[...] = out


def _out_projection(x, mod_l, gate, w, branches, final_g=None):
    b, s, d = x.shape
    ts = min(TOKEN_TILE, s)
    odd = len(branches) > 2
    row = lambda width: pl.BlockSpec((None, ts, width), lambda i, j: (i, j, 0))
    in_specs = [row(d), pl.BlockSpec((None, None, 1, d), lambda i, j: (i, 2, 0, 0)), row(gate.shape[-1]),
                pl.BlockSpec(w.shape, lambda i, j: (0, 0), pipeline_mode=pl.Buffered(1))]
    in_specs += [row(a.shape[-1]) for a in branches]
    args = [x, mod_l, gate, w, *branches]
    if final_g is not None:
        in_specs.append(pl.BlockSpec((1, d), lambda i, j: (0, 0)))
        args.append(final_g.reshape(1, d))
    return pl.pallas_call(
        functools.partial(_out_kernel, odd=odd, final=final_g is not None), grid=(b, s // ts),
        in_specs=in_specs, out_specs=row(d), out_shape=jax.ShapeDtypeStruct((b, s, d), F32),
        compiler_params=_cparams(("parallel", "parallel")), name="gated_out_proj",
    )(*args)


def _pair_cols(a0, b0):
    a, bb = np.arange(a0, a0 + HEAD_DIM), np.arange(b0, b0 + HEAD_DIM)
    return np.concatenate([a[:HALF], bb[:HALF], a[HALF:], bb[HALF:]])


def _paired_head_order(width):
    pairs = [np.concatenate([np.arange(p * HEAD_DIM, (p + 1) * HEAD_DIM),
                             np.arange((p + C_HPG) * HEAD_DIM, (p + C_HPG + 1) * HEAD_DIM)]) for p in range(C_HPG)]
    return np.concatenate(pairs + [np.arange(C_HEADS * HEAD_DIM, width)])


def _even_layout():
    aq, ak, av = 0, 512, 1024
    bq, bk, bv, bf, gate = 1536, 2048, 2560, 3072, 3080
    zero = gate + 1024
    cols = [_pair_cols(ak + 2 * h * HEAD_DIM, ak + (2 * h + 1) * HEAD_DIM) for h in range(A_HEADS)]
    cols.append(np.arange(bk, bk + 512))
    cols.append(np.concatenate([np.repeat(np.arange(bf, bf + B_HEADS), BIAS_PIECES),
                                np.full(LANES - B_HEADS * BIAS_PIECES, zero)]))
    cols.append(np.arange(gate, gate + 1024))
    segs = (("rope", 0, 512), ("bf16", 512, 1024), ("f32", 1024, 1152), ("bf16", 1152, 2176))
    rows = [_pair_cols(aq + 2 * h * HEAD_DIM, aq + (2 * h + 1) * HEAD_DIM) for h in range(A_HEADS)]
    rows += [np.arange(bq, bq + 512), np.arange(av, av + 512), np.arange(bv, bv + 512)]
    tsegs = (("rope", 0, 512), ("bf16", 512, 2048))
    return np.concatenate(cols), segs, np.concatenate(rows), tsegs


def _odd_layout():
    cq, kc, vc, ks, vs, kw, vw, cg = 0, 512, 640, 768, 896, 1024, 1152, 1280
    dq, da, db, gate = 1304, 2840, 2844, 2848
    zero = gate + 1024
    cols = [_pair_cols(ks, ks + HEAD_DIM), _pair_cols(kw, kw + HEAD_DIM), np.arange(kc, kc + 256)]
    small = np.concatenate([np.arange(da, da + 2 * D_HEADS), np.arange(cg, cg + C_HEADS * N_BRANCH)])
    cols.append(np.concatenate([small, np.full(LANES - small.size, zero)]))
    cols += [np.arange(dq, dq + 1536), gate + _paired_head_order(1024)]
    segs = (("rope", 0, 256), ("f32", 256, 512), ("dn_gates", 512, 640), ("dn_q", 640, 1152), ("dn_k", 1152, 1664),
            ("dn_v", 1664, 2176), ("bf16", 2176, 3200))
    rows = [_pair_cols(cq + p * HEAD_DIM, cq + (p + C_HPG) * HEAD_DIM) for p in range(C_HPG)]
    rows += [np.arange(vs, vs + LANES), np.arange(vw, vw + LANES)]
    tsegs = (("rope+raw", 0, 512), ("bf16", 512, 768))
    return np.concatenate(cols), segs, np.concatenate(rows), tsegs


def _layout_weights(w, cols, rows, n_query_rows):
    w = jnp.concatenate([w, jnp.zeros((w.shape[0], 1), w.dtype)], axis=1)
    scale = jnp.where(jnp.arange(rows.size) < n_query_rows, QK_SCALE, 1.0).astype(w.dtype)
    return w[:, cols].astype(BF16), (w[:, rows] * scale).T.astype(BF16)


def _compress_weights(pe, w1, w2, for_keys):
    half = L_CMP // 2 * HEAD_DIM
    w1ab = jnp.concatenate([w1[:half], w1[half:]], axis=1)
    w1ab = w1ab.reshape(CMP_STRIDE, HEAD_DIM, 2 * CMP_HIDDEN)
    zeros = jnp.zeros_like(w1ab)
    w1_by_tok = jnp.concatenate([jnp.concatenate([w1ab, zeros], axis=2),
                                 jnp.concatenate([zeros, w1ab], axis=2)], axis=1).astype(BF16)
    w2p = jnp.zeros((C_GROUPS, CMP_HIDDEN, LANES), F32)
    for g in range(C_GROUPS):
        if for_keys:
            w2p = w2p.at[g, :, g * HALF:(g + 1) * HALF].set(w2[:, :HALF])
            w2p = w2p.at[g, :, HEAD_DIM + g * HALF:HEAD_DIM + (g + 1) * HALF].set(w2[:, HALF:])
        else:
            w2p = w2p.at[g, :, g * HEAD_DIM:(g + 1) * HEAD_DIM].set(w2)
    if not for_keys:
        w2p = w2p.transpose(0, 2, 1)
    return [w1_by_tok, w1.astype(F32), pe.astype(F32).reshape(1, L_CMP * HEAD_DIM), w2p.astype(BF16)]


def _even_layer(x, mod_l, rope, layer_idx, g, w_in, b_forget, lq1, lk1, lq2, lk2, subln_g, w_out, final_g):
    cols, segs, rows, tsegs = _even_layout()
    w, wt = _layout_weights(w_in, cols, rows, 2 * A_HEADS * HEAD_DIM + B_HEADS * HEAD_DIM)
    k_a, k_b, forget_logits, gate, qt_a, rest_t = _projection(x, mod_l, g, rope, w, wt, segs, tsegs)
    lam_init = 0.8 - 0.6 * math.exp(-0.3 * layer_idx)
    vec = lambda a: a.astype(F32).reshape(1, -1)
    const = lambda shape: (lambda tq, tk, tile: pl.BlockSpec(shape, lambda i, hb, *t: (0, 0)))
    nb = B_HEADS // 2
    per_step = 2
    oa = _flash("diff", qt_a, k_a, rest_t, A_HEADS, per_step, 0, 0, nb, False,
                extra=[vec(lq1), vec(lk1), vec(lq2), vec(lk2), subln_g.astype(F32).reshape(LANES, 1)],
                extra_specs=[const((1, HEAD_DIM))] * 4 + [const((LANES, 1))], lam_init=lam_init)
    bias = _forget_cumsum(forget_logits, b_forget)
    bias_spec = lambda tq, tk, tile: pl.BlockSpec((None, tk, LANES), lambda i, hb, *t: (i, tile(*t)[1], 0))
    ob = _flash("fox", rest_t, k_b, rest_t, nb, per_step, 0, 0, nb + A_HEADS, False,
                extra=[bias], extra_specs=[bias_spec])
    return _out_projection(x, mod_l, gate, w_out.astype(BF16), [oa, ob], final_g)


def _odd_layer(x, mod_l, rope, g, w_in, pe_k, pe_v, w1_k, w2_k, w1_v, w2_v, conv_w, a_log, dt_bias,
               dn_norm_g, w_out, final_g):
    b, s, _ = x.shape
    cols, segs, rows, tsegs = _odd_layout()
    w, wt = _layout_weights(w_in, cols, rows, C_HEADS * HEAD_DIM)
    gate_params = jnp.zeros((2, LANES), F32).at[:, :D_HEADS].set(jnp.stack([a_log, dt_bias]).astype(F32))
    k_rot, cmp_in, small, dn_q, dn_k, dn_v, gate, q_raw_t, q_rot_t, v_t = _projection(
        x, mod_l, g, rope, w, wt, segs, tsegs, deltanet=(conv_w.astype(F32), gate_params))
    n_cmp = (s - L_CMP) // CMP_STRIDE + 1
    kcmp, vcmp_t = _compress(cmp_in, _compress_weights(pe_k, w1_k, w2_k, True),
                             _compress_weights(pe_v, w1_v, w2_v, False))
    o_cmp, sel = _cmp_select(q_raw_t, kcmp, vcmp_t, n_cmp)
    sel_spec = lambda tq, tk, tile: pl.BlockSpec((None, LANES, tq), lambda i, hb, *t: (i, 0, tile(*t)[0]))
    o_slc = _flash("sel", q_rot_t, k_rot, v_t, C_HPG, C_HPG, 0, 0, 0, True, extra=[sel], extra_specs=[sel_spec])
    o_win = _flash("win", q_rot_t, k_rot, v_t, C_HPG, C_HPG, 0, 1, 1, True)
    gates = small[:, :, :2 * D_HEADS].reshape(b, s, 2, D_HEADS).transpose(0, 3, 1, 2)
    od = _gated_deltanet(dn_q, dn_k, dn_v, gates, dn_norm_g)
    rows = _paired_head_order(w_out.shape[0])
    return _out_projection(x, mod_l, gate, w_out[rows].astype(BF16), [o_cmp, o_slc, o_win, od, small], final_g)


def kernel(x, c, positions, norm_g, w_mod, b_mod, w_out, final_norm_g, w_in_even, b_forget, lambda_q1, lambda_k1,
           lambda_q2, lambda_k2, subln_g, w_in_odd, cmp_pe_k, cmp_pe_v, cmp_w1_k, cmp_w2_k, cmp_w1_v, cmp_w2_v,
           conv_w, a_log, dt_bias, dn_norm_g):
    depth = norm_g.shape[0]
    rope = _rope_tables(positions)
    mod = _modulation(c, w_mod, b_mod)
    for l in range(depth):
        final_g = final_norm_g if l == depth - 1 else None
        i = l // 2
        if l % 2 == 0:
            x = _even_layer(x, mod[l], rope, l, norm_g[l], w_in_even[i], b_forget[i], lambda_q1[i],
                            lambda_k1[i], lambda_q2[i], lambda_k2[i], subln_g[i], w_out[l], final_g)
        else:
            x = _odd_layer(x, mod[l], rope, norm_g[l], w_in_odd[i], cmp_pe_k[i], cmp_pe_v[i], cmp_w1_k[i],
                           cmp_w2_k[i], cmp_w1_v[i], cmp_w2_v[i], conv_w[i], a_log[i], dt_bias[i], dn_norm_g[i],
                           w_out[l], final_g)
    return x
```

```python
import functools
import math

import jax
import jax.numpy as jnp
import numpy as np
from jax import lax
from jax.experimental import pallas as pl
from jax.experimental.pallas import tpu as pltpu

F32 = jnp.float32
BF16 = jnp.bfloat16
HI = lax.Precision.HIGHEST

LANES = 128
HEAD_DIM = 64
HALF = HEAD_DIM // 2
ROPE_THETA = 10000.0
NORM_EPS = 1e-6
NEG = -1e30
MASK_BIG = 1e30
LOG2E = math.log2(math.e)
QK_SCALE = HEAD_DIM ** -0.5 * LOG2E
BIAS_PIECES = 3
NORM_ROWS = 16
A_HEADS = 4
B_HEADS = 8
C_HEADS = 8
C_GROUPS = 2
C_HPG = C_HEADS // C_GROUPS
L_CMP = 32
CMP_STRIDE = 16
CMP_HIDDEN = 256
L_SEL = 64
N_SEL = 8
WINDOW = 512
N_BRANCH = 3
FORCE_BONUS = 1e4
D_HEADS = 4
D_DK = 128
D_DV = 128
CONV_WIDTH = 4
DN_CHUNK = 64
TOKEN_TILE = 512
VMEM_LIMIT = 56 * 1024 * 1024


def _cparams(sem):
    return pltpu.CompilerParams(dimension_semantics=sem, vmem_limit_bytes=VMEM_LIMIT)


def _dot(a, b, precision=None):
    return jnp.dot(a, b, precision=precision, preferred_element_type=F32)


def _dot_nt(a, b, precision=None):
    return lax.dot_general(a, b, (((1,), (1,)), ((), ())), precision=precision, preferred_element_type=F32)


def _dot_tn(a, b, precision=None):
    return lax.dot_general(a, b, (((0,), (0,)), ((), ())), precision=precision, preferred_element_type=F32)


def _split_bf16(a):
    hi = a.astype(BF16)
    return hi, (a - hi.astype(F32)).astype(BF16)


def _softplus(z):
    return jnp.maximum(z, 0.0) + jnp.log1p(jnp.exp(-jnp.abs(z)))


def _silu(z):
    return z * jax.nn.sigmoid(z)


def _rope_table_kernel(pos_ref, inv_ref, cos_ref, sin_ref, cos_t_ref, sin_t_ref):
    ang = pos_ref[...].astype(F32) * inv_ref[...]
    lane = lax.broadcasted_iota(jnp.int32, (1, LANES), 1)
    cos = jnp.cos(ang)
    sin = jnp.where(lane < 2 * HALF, -1.0, 1.0) * jnp.sin(ang)
    cos_ref[...] = cos
    sin_ref[...] = sin
    cos_t_ref[...] = cos.T
    sin_t_ref[...] = sin.T


def _rope_tables(positions):
    b, s = positions.shape
    inv = ROPE_THETA ** (-jnp.arange(0, HEAD_DIM, 2, dtype=F32) / HEAD_DIM)
    inv = jnp.tile(inv, 4).reshape(1, LANES)
    tok = jax.ShapeDtypeStruct((b, s, LANES), F32)
    feat = jax.ShapeDtypeStruct((b, LANES, s), F32)
    return pl.pallas_call(
        _rope_table_kernel, grid=(b,),
        in_specs=[pl.BlockSpec((None, s, 1), lambda i: (i, 0, 0)),
                  pl.BlockSpec((1, LANES), lambda i: (0, 0))],
        out_specs=[pl.BlockSpec((None, s, LANES), lambda i: (i, 0, 0))] * 2
        + [pl.BlockSpec((None, LANES, s), lambda i: (i, 0, 0))] * 2,
        out_shape=[tok, tok, feat, feat], compiler_params=_cparams(("parallel",)), name="rope_tables",
    )(positions.reshape(b, s, 1), inv)


def _mod_kernel(c_ref, w_ref, b_ref, o_ref):
    o_ref[...] = _dot(_silu(c_ref[...]), w_ref[...], HI) + b_ref[...]


def _modulation(c, w_mod, b_mod):
    depth, d, n = w_mod.shape
    b = c.shape[0]
    tn = 1024
    mod = pl.pallas_call(
        _mod_kernel, grid=(depth, n // tn),
        in_specs=[pl.BlockSpec((b, d), lambda l, j: (0, 0)),
                  pl.BlockSpec((None, d, tn), lambda l, j: (l, 0, j)),
                  pl.BlockSpec((None, 1, tn), lambda l, j: (l, 0, j))],
        out_specs=pl.BlockSpec((None, b, tn), lambda l, j: (l, 0, j)),
        out_shape=jax.ShapeDtypeStruct((depth, b, n), F32),
        compiler_params=_cparams(("parallel", "parallel")), name="modulation",
    )(c, w_mod, b_mod.reshape(depth, 1, n))
    return mod.reshape(depth, b, 3, 1, d)


def _proj_kernel(x_ref, shift_ref, scale_ref, g_ref, cos_ref, sin_ref, cos_t_ref, sin_t_ref, w_ref, wt_ref,
                 *rest, segs, tsegs, deltanet):
    x = x_ref[...]
    h = x * lax.rsqrt(jnp.mean(x * x, axis=-1, keepdims=True) + NORM_EPS) * g_ref[...]
    h = (h * (1.0 + scale_ref[...]) + shift_ref[...]).astype(BF16)
    if deltanet:
        conv_ref, gparam_ref, halo_ref = rest[0], rest[1], rest[-1]
        outs = list(rest[2:-1])

        @pl.when(pl.program_id(1) == 0)
        def _():
            halo_ref[...] = jnp.zeros(halo_ref.shape, F32)
    else:
        outs = list(rest)
    ts = x.shape[0]
    step = 4 * LANES
    conv_off = 0
    work = []

    def tok_item(kind, c0, a, e, o_ref, cols):
        def epilogue(acc):
            if kind == "rope":
                cos, sin = cos_ref[...], sin_ref[...]
                for j in range(0, e - a, LANES):
                    blk = acc[:, j:j + LANES]
                    rot = blk * cos + pltpu.roll(blk, 2 * HALF, axis=1) * sin
                    o_ref[:, a - c0 + j:a - c0 + j + LANES] = rot.astype(o_ref.dtype)
            elif kind == "dn_gates":
                lane = lax.broadcasted_iota(jnp.int32, (1, LANES), 1)
                log_decay = -jnp.exp(gparam_ref[0:1, :]) * _softplus(acc + gparam_ref[1:2, :])
                o_ref[...] = jnp.where(lane < D_HEADS, log_decay,
                                       jnp.where(lane < 2 * D_HEADS, jax.nn.sigmoid(acc), acc))
            elif kind in ("dn_q", "dn_k", "dn_v"):
                wc = conv_ref[:, cols]
                ext = jnp.concatenate([halo_ref[:, cols], acc], axis=0)
                halo_ref[:, cols] = acc[ts - 8:, :]
                y = acc * wc[CONV_WIDTH - 1:CONV_WIDTH, :]
                for back in range(1, CONV_WIDTH):
                    y = y + ext[8 - back:8 - back + ts, :] * wc[CONV_WIDTH - 1 - back:CONV_WIDTH - back, :]
                y = _silu(y)
                for j in range(0, e - a, LANES):
                    blk = y[:, j:j + LANES]
                    if kind != "dn_v":
                        blk = blk * lax.rsqrt(jnp.sum(blk * blk, axis=-1, keepdims=True) + NORM_EPS)
                    if kind == "dn_q":
                        blk = blk * (D_DK ** -0.5)
                    o_ref[:, a - c0 + j:a - c0 + j + LANES] = blk
            else:
                o_ref[:, a - c0:e - c0] = acc.astype(o_ref.dtype)
        return (lambda: _dot(h, w_ref[:, a:e])), epilogue

    def feat_item(kind, r0, a, e, o_ref, raw_ref):
        def epilogue(acc):
            if raw_ref is not None:
                raw_ref[a - r0:e - r0, :] = acc
            if kind in ("rope", "rope+raw"):
                cos, sin = cos_t_ref[...], sin_t_ref[...]
                for j in range(0, e - a, LANES):
                    blk = acc[j:j + LANES, :]
                    rot = blk * cos + pltpu.roll(blk, 2 * HALF, axis=0) * sin
                    o_ref[a - r0 + j:a - r0 + j + LANES, :] = rot.astype(o_ref.dtype)
            else:
                o_ref[a - r0:e - r0, :] = acc.astype(o_ref.dtype)
        return (lambda: _dot_nt(wt_ref[a:e, :], h)), epilogue

    for kind, c0, c1 in segs:
        o_ref = outs.pop(0)
        for a in range(c0, c1, step):
            e = min(a + step, c1)
            cols = None
            if kind in ("dn_q", "dn_k", "dn_v"):
                cols = slice(conv_off, conv_off + (e - a))
                conv_off += e - a
            work.append(tok_item(kind, c0, a, e, o_ref, cols))
    for kind, r0, r1 in tsegs:
        raw_ref = outs.pop(0) if kind == "rope+raw" else None
        o_ref = outs.pop(0)
        for a in range(r0, r1, step):
            work.append(feat_item(kind, r0, a, min(a + step, r1), o_ref, raw_ref))
    acc = work[0][0]()
    for i, (_, epilogue) in enumerate(work):
        nxt = work[i + 1][0]() if i + 1 < len(work) else None
        epilogue(acc)
        acc = nxt


def _projection(x, mod_l, g, rope, w, wt, segs, tsegs, deltanet=None):
    b, s, d = x.shape
    ts = min(TOKEN_TILE, s)
    row = lambda width: pl.BlockSpec((None, ts, width), lambda i, j: (i, j, 0))
    col = lambda height: pl.BlockSpec((None, height, ts), lambda i, j: (i, 0, j))
    out_specs, out_shapes = [], []
    for kind, c0, c1 in segs:
        out_specs.append(row(c1 - c0))
        out_shapes.append(jax.ShapeDtypeStruct((b, s, c1 - c0), BF16 if kind in ("rope", "bf16") else F32))
    for kind, r0, r1 in tsegs:
        if kind == "rope+raw":
            out_specs.append(col(r1 - r0))
            out_shapes.append(jax.ShapeDtypeStruct((b, r1 - r0, s), F32))
        out_specs.append(col(r1 - r0))
        out_shapes.append(jax.ShapeDtypeStruct((b, r1 - r0, s), BF16))
    modspec = lambda k: pl.BlockSpec((None, None, 1, d), lambda i, j: (i, k, 0, 0))
    resident = lambda a: pl.BlockSpec(a.shape, lambda i, j: (0, 0), pipeline_mode=pl.Buffered(1))
    in_specs = [row(d), modspec(0), modspec(1), pl.BlockSpec((1, d), lambda i, j: (0, 0)),
                row(LANES), row(LANES), col(LANES), col(LANES), resident(w), resident(wt)]
    args = [x, mod_l, mod_l, g.reshape(1, d), *rope, w, wt]
    scratch = []
    if deltanet is not None:
        in_specs += [pl.BlockSpec(a.shape, lambda i, j: (0, 0)) for a in deltanet]
        args += list(deltanet)
        scratch = [pltpu.VMEM((8, deltanet[0].shape[1]), F32)]
    sem = ("parallel", "arbitrary") if deltanet is not None else ("parallel", "parallel")
    return pl.pallas_call(
        functools.partial(_proj_kernel, segs=segs, tsegs=tsegs, deltanet=deltanet is not None), grid=(b, s // ts),
        in_specs=in_specs, out_specs=out_specs, out_shape=out_shapes, scratch_shapes=scratch,
        compiler_params=_cparams(sem), name="adaln_in_proj",
    )(*args)


def _cum_kernel(x_ref, bias_ref, o_ref):
    s = x_ref.shape[0]
    ii = lax.broadcasted_iota(jnp.int32, (LANES, LANES), 0)
    jj = lax.broadcasted_iota(jnp.int32, (LANES, LANES), 1)
    lower = (ii >= jj).astype(F32)
    lane = lax.broadcasted_iota(jnp.int32, (1, LANES), 1)
    carry = jnp.zeros((1, LANES), F32)
    for r0 in range(0, s, LANES):
        z = x_ref[r0:r0 + LANES, :] + bias_ref[...]
        logf = jnp.minimum(z, 0.0) - jnp.log1p(jnp.exp(-jnp.abs(z)))
        loc = _dot(lower, logf, HI) + carry
        carry = loc[LANES - 1:LANES, :]
        val = loc * (-LOG2E)
        hi = val.astype(BF16)
        rest = val - hi.astype(F32)
        mid = rest.astype(BF16)
        lo = (rest - mid.astype(F32)).astype(BF16)
        piece = jnp.where(lane % BIAS_PIECES == 0, hi, jnp.where(lane % BIAS_PIECES == 1, mid, lo))
        o_ref[r0:r0 + LANES, :] = jnp.where(lane < B_HEADS * BIAS_PIECES, piece, jnp.zeros_like(piece))


def _forget_cumsum(logits, b_forget):
    b, s, _ = logits.shape
    bias = jnp.zeros((1, LANES), F32).at[0, :B_HEADS * BIAS_PIECES].set(jnp.repeat(b_forget.astype(F32), BIAS_PIECES))
    return pl.pallas_call(
        _cum_kernel, grid=(b,),
        in_specs=[pl.BlockSpec((None, s, LANES), lambda i: (i, 0, 0)), pl.BlockSpec((1, LANES), lambda i: (0, 0))],
        out_specs=pl.BlockSpec((None, s, LANES), lambda i: (i, 0, 0)),
        out_shape=jax.ShapeDtypeStruct((b, s, LANES), BF16),
        compiler_params=_cparams(("parallel",)), name="forget_cumsum",
    )(logits, bias)


def _triangle_step(t, nq):
    qi = sum((t >= j * (j + 1) // 2).astype(jnp.int32) for j in range(1, nq))
    return qi, t - qi * (qi + 1) // 2


def _flash_kernel(*refs, mode, tq, tk, nq, nblk, shared_kv, lam_init):
    qt_ref, k_ref, vt_ref = refs[:3]
    m_ref, acc_ref = refs[-2:]
    o_ref = refs[-3]
    extra = refs[3:-3]
    if mode == "win":
        qi, ki = pl.program_id(2), pl.program_id(3)
        kv, last = qi - 1 + ki, ki == pl.num_programs(3) - 1
    else:
        qi, ki = _triangle_step(pl.program_id(2), nq)
        kv, last = ki, ki == qi
    row = lax.broadcasted_iota(jnp.int32, (LANES, 1), 0)
    if mode == "fox":
        slot_rows = (row < HEAD_DIM, row >= HEAD_DIM)
    else:
        slot_rows = ((row % HEAD_DIM) < HALF, (row % HEAD_DIM) >= HALF)
    vrows = acc_ref.shape[1] - NORM_ROWS
    block = lambda j: slice(j * LANES, (j + 1) * LANES)
    half = tq // 2

    @pl.when(ki == 0)
    def _():
        m_ref[...] = jnp.full(m_ref.shape, NEG, F32)
        acc_ref[...] = jnp.zeros(acc_ref.shape, F32)

    def step(kind):
        rr = lax.broadcasted_iota(jnp.int32, (LANES, tq), 0)
        k_extra = [None, None]
        for s in range(2):
            if mode == "sel":
                cb = (kv * tk + lax.broadcasted_iota(jnp.int32, (tk, LANES), 0)) // L_SEL
                ll = lax.broadcasted_iota(jnp.int32, (tk, LANES), 1)
                k_extra[s] = jnp.where(ll == cb + s * (LANES // 4), MASK_BIG, 0.0).astype(BF16)

        def q_extra(j, s):
            if mode == "sel":
                return extra[0][...]
            if mode == "fox":
                first = ((pl.program_id(1) * nblk + j) * 2 + s) * BIAS_PIECES
                return jnp.where((rr >= first) & (rr < first + BIAS_PIECES), 1.0, 0.0).astype(BF16)
            return None
        lo, hi, everything = slice(0, half), slice(half, tq), slice(0, tq)
        parts = {"full": [(everything, everything, None)],
                 "lower": [(lo, lo, lo), (everything, hi, hi)],
                 "upper": [(everything, lo, lo), (hi, hi, hi)]}[kind]
        tri_k = lax.broadcasted_iota(jnp.int32, (half, half), 0)
        tri_q = lax.broadcasted_iota(jnp.int32, (half, half), 1)
        tri = tri_k <= tri_q if kind == "lower" else tri_k > tri_q
        ones = jnp.ones((NORM_ROWS, tk), BF16)
        sts, vts = [], []
        for j in range(nblk):
            qt = qt_ref[block(j), :]
            k = k_ref[...] if shared_kv else k_ref[:, block(j)]
            base = 0 if shared_kv else j * LANES
            for s in range(2):
                qs = jnp.where(slot_rows[s], qt, jnp.zeros_like(qt))
                q_more = q_extra(j, s)
                if q_more is not None:
                    k_more = extra[0][...] if mode == "fox" else k_extra[s]
                    k_all, q_all = jnp.concatenate([k, k_more], axis=1), jnp.concatenate([qs, q_more], axis=0)
                else:
                    k_all, q_all = k, qs
                sts.append([_dot(k_all[krows], q_all[:, lanes]) for krows, lanes, _ in parts])
                v0 = base if vrows == LANES else base + s * vrows
                vts.append(jnp.concatenate([vt_ref[v0:v0 + vrows, :], ones], axis=0))
        for idx in range(2 * nblk):
            for (krows, lanes, tri_rows), st in zip(parts, sts[idx]):
                if tri_rows is not None:
                    if krows == tri_rows:
                        st = jnp.where(tri, st, NEG)
                    elif tri_rows == lo:
                        st = jnp.concatenate([jnp.where(tri, st[:half], NEG), st[half:]], axis=0)
                    else:
                        st = jnp.concatenate([st[:half], jnp.where(tri, st[half:], NEG)], axis=0)
                m_prev = m_ref[idx, :, lanes]
                m_new = jnp.maximum(m_prev, jnp.max(st, axis=0, keepdims=True))
                alpha = jnp.exp2(m_prev - m_new)
                p = jnp.exp2(st - m_new).astype(BF16)
                acc_ref[idx, :, lanes] = alpha * acc_ref[idx, :, lanes] + _dot(vts[idx][:, krows], p)
                m_ref[idx, :, lanes] = m_new

    if mode == "win":
        pl.when((ki == 0) & (kv >= 0))(functools.partial(step, "upper"))
        pl.when(ki == 1)(functools.partial(step, "lower"))
    else:
        pl.when(ki < qi)(functools.partial(step, "full"))
        pl.when(ki == qi)(functools.partial(step, "lower"))

    @pl.when(last)
    def _():
        for j in range(nblk):
            o0 = acc_ref[2 * j, :vrows, :] / acc_ref[2 * j, vrows:vrows + 1, :]
            o1 = acc_ref[2 * j + 1, :vrows, :] / acc_ref[2 * j + 1, vrows:vrows + 1, :]
            if mode == "diff":
                lq1, lk1, lq2, lk2, subg = (r[...] for r in extra)
                lam = (jnp.exp(jnp.sum(lq1 * lk1, axis=1, keepdims=True))
                       - jnp.exp(jnp.sum(lq2 * lk2, axis=1, keepdims=True)) + lam_init)
                o = o0 - lam * o1
                o = o * lax.rsqrt(jnp.mean(o * o, axis=0, keepdims=True) + NORM_EPS) * subg * (1.0 - lam_init)
            else:
                o = jnp.concatenate([o0, o1], axis=0)
            o_ref[:, block(j)] = o.T.astype(o_ref.dtype)


def _flash(mode, qt, k, vt, n_qblocks, nblk, q0, k0, v0, shared_kv, extra=(), extra_specs=(), lam_init=0.0):
    b, s, _ = k.shape
    tq = tk = min(TOKEN_TILE, s)
    nq = s // tq
    if mode == "win":
        assert WINDOW == tk
        steps, sem = (nq, 2), ("parallel", "arbitrary")
        tile = lambda qi, ki: (qi, jnp.maximum(qi - 1 + ki, 0))
    else:
        steps, sem = (nq * (nq + 1) // 2,), ("arbitrary",)
        tile = lambda t: _triangle_step(t, nq)
    assert n_qblocks % nblk == 0 and q0 % nblk == 0 and (shared_kv or (k0 % nblk == 0 and v0 % nblk == 0))
    wide = nblk * LANES
    if shared_kv:
        k_spec = pl.BlockSpec((None, tk, LANES), lambda i, hb, *t: (i, tile(*t)[1], k0))
        v_spec = pl.BlockSpec((None, LANES, tk), lambda i, hb, *t: (i, v0, tile(*t)[1]))
    else:
        k_spec = pl.BlockSpec((None, tk, wide), lambda i, hb, *t: (i, tile(*t)[1], k0 // nblk + hb))
        v_spec = pl.BlockSpec((None, wide, tk), lambda i, hb, *t: (i, v0 // nblk + hb, tile(*t)[1]))
    in_specs = [pl.BlockSpec((None, wide, tq), lambda i, hb, *t: (i, q0 // nblk + hb, tile(*t)[0])), k_spec, v_spec]
    for spec in extra_specs:
        in_specs.append(spec(tq, tk, tile))
    vrows = LANES if mode == "diff" else HEAD_DIM
    return pl.pallas_call(
        functools.partial(_flash_kernel, mode=mode, tq=tq, tk=tk, nq=nq, nblk=nblk, shared_kv=shared_kv,
                          lam_init=lam_init),
        grid=(b, n_qblocks // nblk) + steps, in_specs=in_specs,
        out_specs=pl.BlockSpec((None, tq, wide), lambda i, hb, *t: (i, tile(*t)[0], hb)),
        out_shape=jax.ShapeDtypeStruct((b, s, n_qblocks * LANES), BF16),
        scratch_shapes=[pltpu.VMEM((2 * nblk, 1, tq), F32), pltpu.VMEM((2 * nblk, vrows + NORM_ROWS, tq), F32)],
        compiler_params=_cparams(("parallel", "parallel") + sem),
        name="flash_" + mode,
    )(qt, k, vt, *extra)


def _compress_kernel(xk_ref, xv_ref, w1_by_tok_k, w1_k, pe_k, w2_k, w1_by_tok_v, w1_v, pe_v, w2_v, ok_ref, ov_ref):
    cn = xk_ref.shape[0] // CMP_STRIDE

    def run(x_ref, w1_by_tok, w1, pe, w2, o_ref, transposed):
        pe_term = _dot(pe[...], w1[...], HI)
        ab = jnp.zeros((cn, C_GROUPS * 2 * CMP_HIDDEN), F32)
        for tok in range(CMP_STRIDE):
            rows = x_ref[pl.ds(tok, cn, stride=CMP_STRIDE), :]
            ab = ab + _dot(rows.astype(BF16), w1_by_tok[tok])
        out = 0.0
        for g in range(C_GROUPS):
            first = ab[:, g * 2 * CMP_HIDDEN:(g * 2 + 1) * CMP_HIDDEN]
            second = ab[:, (g * 2 + 1) * CMP_HIDDEN:(g + 1) * 2 * CMP_HIDDEN]
            hid = first + pltpu.roll(second, cn - 1, axis=0) + pe_term
            act = _silu(hid).astype(BF16)
            out = out + (_dot_nt(w2[g], act) if transposed else _dot(act, w2[g]))
        o_ref[...] = out
    run(xk_ref, w1_by_tok_k, w1_k, pe_k, w2_k, ok_ref, False)
    run(xv_ref, w1_by_tok_v, w1_v, pe_v, w2_v, ov_ref, True)


def _compress(cmp_in, weights_k, weights_v):
    b, s, width = cmp_in.shape
    cn = s // CMP_STRIDE
    full = lambda a: pl.BlockSpec(a.shape, lambda i: (0,) * a.ndim)
    return pl.pallas_call(
        _compress_kernel, grid=(b,),
        in_specs=[pl.BlockSpec((None, s, LANES), lambda i: (i, 0, 0)), pl.BlockSpec((None, s, LANES), lambda i: (i, 0, 1))]
        + [full(a) for a in weights_k + weights_v],
        out_specs=[pl.BlockSpec((None, cn, LANES), lambda i: (i, 0, 0)),
                   pl.BlockSpec((None, LANES, cn), lambda i: (i, 0, 0))],
        out_shape=[jax.ShapeDtypeStruct((b, cn, LANES), F32), jax.ShapeDtypeStruct((b, LANES, cn), F32)],
        compiler_params=_cparams(("parallel",)), name="nsa_compress",
    )(cmp_in, cmp_in, *weights_k, *weights_v)


def _cmp_select_kernel(qt_ref, kc_ref, vct_ref, o_ref, sel_ref, *, tq, n_blk, n_cmp):
    qi = pl.program_id(1)
    cn = kc_ref.shape[0]
    kc_hi, kc_lo = _split_bf16(kc_ref[...])
    vct = vct_ref[...].astype(BF16)
    row = lax.broadcasted_iota(jnp.int32, (LANES, 1), 0)
    slot_rows = ((row % HEAD_DIM) < HALF, (row % HEAD_DIM) >= HALF)
    t_row = qi * tq + lax.broadcasted_iota(jnp.int32, (1, tq), 1)
    m_col = lax.broadcasted_iota(jnp.int32, (cn, 1), 0)
    valid = (m_col * CMP_STRIDE + L_CMP - 1 <= t_row) & (m_col < n_cmp)
    any_valid = (t_row >= L_CMP - 1).astype(F32)
    heads = [(p_blk, s) for p_blk in range(C_HPG) for s in range(C_GROUPS)]
    scores = []
    for p_blk, s in heads:
        q_hi, q_lo = _split_bf16(jnp.where(slot_rows[s], qt_ref[p_blk * LANES:(p_blk + 1) * LANES, :], 0.0))
        scores.append(_dot(kc_hi, q_hi) + _dot(kc_lo, q_hi) + _dot(kc_hi, q_lo))
    probs = []
    for sc in scores:
        sc = jnp.where(valid, sc, NEG)
        e = jnp.exp2(sc - jnp.max(sc, axis=0, keepdims=True))
        probs.append(e / jnp.sum(e, axis=0, keepdims=True) * any_valid)
    outs = [_dot(vct[s * HEAD_DIM:(s + 1) * HEAD_DIM, :], p.astype(BF16)) for (_, s), p in zip(heads, probs)]
    psum = [sum(p for (_, s), p in zip(heads, probs) if s == g) for g in range(C_GROUPS)]
    for p_blk in range(C_HPG):
        o_ref[:, p_blk * LANES:(p_blk + 1) * LANES] = jnp.concatenate(
            outs[C_GROUPS * p_blk:C_GROUPS * (p_blk + 1)], axis=0).T.astype(o_ref.dtype)
    jb = lax.broadcasted_iota(jnp.int32, (n_blk, cn), 0)
    mm = lax.broadcasted_iota(jnp.int32, (n_blk, cn), 1)
    overlap = ((mm * CMP_STRIDE < jb * L_SEL + L_SEL) & (mm * CMP_STRIDE + L_CMP > jb * L_SEL)
               & (mm < n_cmp)).astype(BF16)
    j = lax.broadcasted_iota(jnp.int32, (n_blk, tq), 0)
    cur = (qi * tq + lax.broadcasted_iota(jnp.int32, (n_blk, tq), 1)) // L_SEL
    forced = (j == 0) | (j == cur) | (j == cur - 1)
    n_top = min(N_SEL, n_blk)
    pad_rows = LANES // 4 - n_blk
    parts = []
    for s in range(C_GROUPS):
        p_hi, p_lo = _split_bf16(psum[s])
        imp = _dot(overlap, p_hi) + _dot(overlap, p_lo)
        score = jnp.where(j > cur, NEG, imp + jnp.where(forced, FORCE_BONUS, 0.0))
        rank = jnp.zeros((n_blk, tq), jnp.int32)
        for jp in range(n_blk):
            r = score[jp:jp + 1, :]
            rank = rank + ((r > score) | ((r == score) & (jp < j))).astype(jnp.int32)
        parts.append(jnp.where(rank < n_top, 0.0, -1.0))
        if pad_rows:
            parts.append(jnp.zeros((pad_rows, tq), F32))
    parts.append(jnp.zeros((LANES // 2, tq), F32))
    sel_ref[...] = jnp.concatenate(parts, axis=0).astype(sel_ref.dtype)


def _cmp_select(q_raw_t, kcmp, vcmp_t, n_cmp):
    b, width, s = q_raw_t.shape
    cn = kcmp.shape[1]
    tq = min(TOKEN_TILE, s)
    n_blk = s // L_SEL
    return pl.pallas_call(
        functools.partial(_cmp_select_kernel, tq=tq, n_blk=n_blk, n_cmp=n_cmp), grid=(b, s // tq),
        in_specs=[pl.BlockSpec((None, width, tq), lambda i, j: (i, 0, j)),
                  pl.BlockSpec((None, cn, LANES), lambda i, j: (i, 0, 0)),
                  pl.BlockSpec((None, LANES, cn), lambda i, j: (i, 0, 0))],
        out_specs=[pl.BlockSpec((None, tq, width), lambda i, j: (i, j, 0)),
                   pl.BlockSpec((None, LANES, tq), lambda i, j: (i, 0, j))],
        out_shape=[jax.ShapeDtypeStruct((b, s, width), BF16), jax.ShapeDtypeStruct((b, LANES, s), BF16)],
        compiler_params=_cparams(("parallel", "parallel")), name="nsa_cmp_select",
    )(q_raw_t, kcmp, vcmp_t)


GDN_HEADS = 2
GDN_BATCH = 2
GDN_UNROLL = 8


def _gdn_kernel(q_ref, k_ref, v_ref, gate_ref, ng_ref, o_ref, mc_s, n_s, gt_s):
    s_len = q_ref.shape[0]
    c = DN_CHUNK
    n_chunks = s_len // c
    lanes = [slice(j * LANES, (j + 1) * LANES) for j in range(GDN_HEADS)]

    sc = GDN_BATCH * c
    ii = lax.broadcasted_iota(jnp.int32, (sc, sc), 0)
    jj = lax.broadcasted_iota(jnp.int32, (sc, sc), 1)
    same = (ii // c) == (jj // c)
    causal, strict, upper, eye = same & (ii >= jj), same & (ii > jj), same & (ii <= jj), ii == jj
    chunk_end = same & (jj % c == c - 1)
    ng = ng_ref[...]

    def load(n, j):
        r0 = pl.multiple_of(n * sc, sc)
        return (q_ref[pl.ds(r0, sc), lanes[j]], k_ref[pl.ds(r0, sc), lanes[j]], v_ref[pl.ds(r0, sc), lanes[j]],
                gate_ref[j, pl.ds(r0, sc), 0:1], gate_ref[j, pl.ds(r0, sc), 1:2])

    def prepare(q, k, v, g, beta):
        g_row = jnp.sum(jnp.where(eye, g, 0.0), axis=0, keepdims=True)
        gc_col = jnp.sum(jnp.where(causal, g_row, 0.0), axis=1, keepdims=True)
        gc_row = jnp.sum(jnp.where(upper, g, 0.0), axis=0, keepdims=True)
        g_last = jnp.sum(jnp.where(chunk_end, gc_row, 0.0), axis=1, keepdims=True)
        decay = jnp.where(causal, jnp.exp(jnp.where(causal, gc_col - gc_row, 0.0)), 0.0)
        eg = jnp.exp(gc_col)
        kb = k * beta
        kbf = k.astype(BF16)
        raw = _dot_nt(kb.astype(BF16), kbf)
        raw_qk = _dot_nt(q.astype(BF16), kbf)
        rhs = jnp.concatenate([v * beta, kb * eg], axis=1).astype(BF16)
        kd = k * jnp.exp(g_last - gc_col)
        kd_t = [kd[t * c:(t + 1) * c].T.astype(BF16) for t in range(GDN_BATCH)]
        g_tot = [jnp.exp(g_last[t * c:t * c + 1]) for t in range(GDN_BATCH)]
        return raw, raw_qk, decay, rhs, kd_t, g_tot, q * eg

    def solve(prepared):
        n = len(prepared)
        decay = [p[2] for p in prepared]
        x = [-jnp.where(strict, prepared[i][0] * decay[i], 0.0) for i in range(n)]
        inv = [jnp.where(eye, 1.0, 0.0) + x[i] for i in range(n)]
        xb = [x[i].astype(BF16) for i in range(n)]
        x = [_dot(xb[i], xb[i]) for i in range(n)]
        for _ in range(int(math.log2(c)) - 2):
            xb = [x[i].astype(BF16) for i in range(n)]
            both = [_dot(jnp.concatenate([xb[i], inv[i].astype(BF16)], axis=0), xb[i]) for i in range(n)]
            x = [both[i][:sc] for i in range(n)]
            inv = [inv[i] + both[i][sc:] for i in range(n)]
        last = [_dot(inv[i].astype(BF16), x[i].astype(BF16)) for i in range(n)]
        inv = [inv[i] + last[i] for i in range(n)]
        sol = [_dot(inv[i].astype(BF16), prepared[i][3]).astype(BF16) for i in range(n)]
        qk = [(prepared[i][1] * decay[i]).astype(BF16) for i in range(n)]
        qo = [_dot(qk[i], sol[i]) for i in range(n)]
        mn = [[_dot(prepared[i][4][t], sol[i][t * c:(t + 1) * c]) for t in range(GDN_BATCH)] for i in range(n)]
        return [(qo[i], prepared[i][6], mn[i], prepared[i][5]) for i in range(n)]

    def store(n, j, qo, q_dec, mn, g_tot):
        r0 = pl.multiple_of(n * sc, sc)
        o_ref[pl.ds(r0, sc), lanes[j]] = qo[:, :D_DV]
        q_eff = (q_dec - qo[:, D_DV:]).astype(BF16)
        for t in range(GDN_BATCH):
            n_s[j, n * GDN_BATCH + t] = mn[t][:, :D_DV]
            mc_s[j, n * GDN_BATCH + t, :D_DK, :] = mn[t][:, D_DV:].astype(BF16)
            mc_s[j, n * GDN_BATCH + t, D_DK:, :] = q_eff[t * c:(t + 1) * c]
            gt_s[j, n * GDN_BATCH + t] = jnp.broadcast_to(g_tot[t], (8, LANES))

    def prepare_some(i, carry):
        items = [(i * GDN_UNROLL + u, j) for u in range(GDN_UNROLL) for j in range(GDN_HEADS)]
        results = solve([prepare(*operands) for operands in [load(n, j) for n, j in items]])
        for (n, j), res in zip(items, results):
            store(n, j, *res)
        return carry

    lax.fori_loop(0, n_chunks // (GDN_BATCH * GDN_UNROLL), prepare_some, 0)

    def advance(n, states):
        r0 = pl.multiple_of(n * c, c)
        operands = [(o_ref[pl.ds(r0, c), lanes[j]], gt_s[j, n], mc_s[j, n], n_s[j, n]) for j in range(GDN_HEADS)]
        prods = [_dot(operands[j][2], states[j].astype(BF16)) for j in range(GDN_HEADS)]
        out = [states[j] * operands[j][1][0:1, :] - prods[j][:D_DK] + operands[j][3] for j in range(GDN_HEADS)]
        for j in range(GDN_HEADS):
            o_ref[pl.ds(r0, c), lanes[j]] = operands[j][0] + prods[j][D_DK:]
        return tuple(out)

    lax.fori_loop(0, n_chunks, advance, tuple(jnp.zeros((D_DK, D_DV), F32) for _ in range(GDN_HEADS)))
    for j in range(GDN_HEADS):
        o = o_ref[:, lanes[j]]
        o_ref[:, lanes[j]] = o * lax.rsqrt(jnp.mean(o * o, axis=-1, keepdims=True) + NORM_EPS) * ng


def _gated_deltanet(q, k, v, gates, norm_g):
    b, s, _ = q.shape
    hp, width = GDN_HEADS, GDN_HEADS * LANES
    groups = D_HEADS // hp
    n_chunks = s // DN_CHUNK
    assert n_chunks % (GDN_BATCH * GDN_UNROLL) == 0
    col = pl.BlockSpec((None, s, width), lambda i, h: (i, 0, h))
    return pl.pallas_call(
        _gdn_kernel, grid=(b, groups),
        in_specs=[col, col, col, pl.BlockSpec((None, hp, s, 2), lambda i, h: (i, h, 0, 0)),
                  pl.BlockSpec((1, D_DV), lambda i, h: (0, 0))],
        out_specs=col,
        out_shape=jax.ShapeDtypeStruct((b, s, D_HEADS * D_DV), F32),
        scratch_shapes=[pltpu.VMEM((hp, n_chunks, D_DK + DN_CHUNK, D_DV), BF16),
                        pltpu.VMEM((hp, n_chunks, D_DK, D_DV), F32), pltpu.VMEM((hp, n_chunks, 8, LANES), F32)],
        compiler_params=_cparams(("parallel", "parallel")), name="gated_deltanet",
    )(q, k, v, gates, norm_g.astype(F32).reshape(1, D_DV))


def _out_kernel(*refs, odd, final):
    x_ref, mg_ref, gate_ref, w_ref = refs[:4]
    o_ref = refs[-1]
    rest = list(refs[4:-1])
    fin_ref = rest.pop() if final else None
    half = w_ref.shape[0] // 2
    sg = _silu(gate_ref[...].astype(F32))
    if odd:
        cmp_ref, slc_ref, win_ref, od_ref, small_ref = rest
        lane = lax.broadcasted_iota(jnp.int32, (1, LANES), 1)
        bg = jax.nn.sigmoid(small_ref[...])
        blocks = []
        for p_blk in range(C_HPG):
            sl = slice(p_blk * LANES, (p_blk + 1) * LANES)
            acc = 0.0
            for br, ref in enumerate((cmp_ref, slc_ref, win_ref)):
                ca = 2 * D_HEADS + p_blk * N_BRANCH + br
                cb = 2 * D_HEADS + (p_blk + C_HPG) * N_BRANCH + br
                acc = acc + jnp.where(lane < HEAD_DIM, bg[:, ca:ca + 1], bg[:, cb:cb + 1]) * ref[:, sl]
            blocks.append(acc)
        first = jnp.concatenate(blocks, axis=1)
        second = od_ref[...]
    else:
        first, second = rest[0][...], rest[1][...]
    y = (_dot((first * sg[:, :half]).astype(BF16), w_ref[:half, :])
         + _dot((second * sg[:, half:]).astype(BF16), w_ref[half:, :]))
    out = x_ref[...] + mg_ref[...] * y
    if final:
        out = out * lax.rsqrt(jnp.mean(out * out, axis=-1, keepdims=True) + NORM_EPS) * fin_ref[...]
    o_ref[...] = out


def _out_projection(x, mod_l, gate, w, branches, final_g=None):
    b, s, d = x.shape
    ts = min(TOKEN_TILE, s)
    odd = len(branches) > 2
    row = lambda width: pl.BlockSpec((None, ts, width), lambda i, j: (i, j, 0))
    in_specs = [row(d), pl.BlockSpec((None, None, 1, d), lambda i, j: (i, 2, 0, 0)), row(gate.shape[-1]),
                pl.BlockSpec(w.shape, lambda i, j: (0, 0), pipeline_mode=pl.Buffered(1))]
    in_specs += [row(a.shape[-1]) for a in branches]
    args = [x, mod_l, gate, w, *branches]
    if final_g is not None:
        in_specs.append(pl.BlockSpec((1, d), lambda i, j: (0, 0)))
        args.append(final_g.reshape(1, d))
    return pl.pallas_call(
        functools.partial(_out_kernel, odd=odd, final=final_g is not None), grid=(b, s // ts),
        in_specs=in_specs, out_specs=row(d), out_shape=jax.ShapeDtypeStruct((b, s, d), F32),
        compiler_params=_cparams(("parallel", "parallel")), name="gated_out_proj",
    )(*args)


def _pair_cols(a0, b0):
    a, bb = np.arange(a0, a0 + HEAD_DIM), np.arange(b0, b0 + HEAD_DIM)
    return np.concatenate([a[:HALF], bb[:HALF], a[HALF:], bb[HALF:]])


def _paired_head_order(width):
    pairs = [np.concatenate([np.arange(p * HEAD_DIM, (p + 1) * HEAD_DIM),
                             np.arange((p + C_HPG) * HEAD_DIM, (p + C_HPG + 1) * HEAD_DIM)]) for p in range(C_HPG)]
    return np.concatenate(pairs + [np.arange(C_HEADS * HEAD_DIM, width)])


def _even_layout():
    aq, ak, av = 0, 512, 1024
    bq, bk, bv, bf, gate = 1536, 2048, 2560, 3072, 3080
    zero = gate + 1024
    cols = [_pair_cols(ak + 2 * h * HEAD_DIM, ak + (2 * h + 1) * HEAD_DIM) for h in range(A_HEADS)]
    cols.append(np.arange(bk, bk + 512))
    cols.append(np.concatenate([np.repeat(np.arange(bf, bf + B_HEADS), BIAS_PIECES),
                                np.full(LANES - B_HEADS * BIAS_PIECES, zero)]))
    cols.append(np.arange(gate, gate + 1024))
    segs = (("rope", 0, 512), ("bf16", 512, 1024), ("f32", 1024, 1152), ("bf16", 1152, 2176))
    rows = [_pair_cols(aq + 2 * h * HEAD_DIM, aq + (2 * h + 1) * HEAD_DIM) for h in range(A_HEADS)]
    rows += [np.arange(bq, bq + 512), np.arange(av, av + 512), np.arange(bv, bv + 512)]
    tsegs = (("rope", 0, 512), ("bf16", 512, 2048))
    return np.concatenate(cols), segs, np.concatenate(rows), tsegs


def _odd_layout():
    cq, kc, vc, ks, vs, kw, vw, cg = 0, 512, 640, 768, 896, 1024, 1152, 1280
    dq, da, db, gate = 1304, 2840, 2844, 2848
    zero = gate + 1024
    small = np.concatenate([np.arange(da, da + 2 * D_HEADS), np.arange(cg, cg + C_HEADS * N_BRANCH)])
    cols = [np.arange(dq, dq + 1536), np.concatenate([small, np.full(LANES - small.size, zero)])]
    cols += [_pair_cols(ks, ks + HEAD_DIM), _pair_cols(kw, kw + HEAD_DIM), np.arange(kc, kc + 256)]
    cols.append(gate + _paired_head_order(1024))
    segs = (("dn_q", 0, 512), ("dn_k", 512, 1024), ("dn_v", 1024, 1536), ("dn_gates", 1536, 1664),
            ("rope", 1664, 1920), ("f32", 1920, 2176), ("bf16", 2176, 3200))
    rows = [_pair_cols(cq + p * HEAD_DIM, cq + (p + C_HPG) * HEAD_DIM) for p in range(C_HPG)]
    rows += [np.arange(vs, vs + LANES), np.arange(vw, vw + LANES)]
    tsegs = (("rope+raw", 0, 512), ("bf16", 512, 768))
    return np.concatenate(cols), segs, np.concatenate(rows), tsegs


def _layout_weights(w, cols, rows, n_query_rows):
    w = jnp.concatenate([w, jnp.zeros((w.shape[0], 1), w.dtype)], axis=1)
    scale = jnp.where(jnp.arange(rows.size) < n_query_rows, QK_SCALE, 1.0).astype(w.dtype)
    return w[:, cols].astype(BF16), (w[:, rows] * scale).T.astype(BF16)


def _compress_weights(pe, w1, w2, for_keys):
    half = L_CMP // 2 * HEAD_DIM
    w1ab = jnp.concatenate([w1[:half], w1[half:]], axis=1)
    w1ab = w1ab.reshape(CMP_STRIDE, HEAD_DIM, 2 * CMP_HIDDEN)
    zeros = jnp.zeros_like(w1ab)
    w1_by_tok = jnp.concatenate([jnp.concatenate([w1ab, zeros], axis=2),
                                 jnp.concatenate([zeros, w1ab], axis=2)], axis=1).astype(BF16)
    w2p = jnp.zeros((C_GROUPS, CMP_HIDDEN, LANES), F32)
    for g in range(C_GROUPS):
        if for_keys:
            w2p = w2p.at[g, :, g * HALF:(g + 1) * HALF].set(w2[:, :HALF])
            w2p = w2p.at[g, :, HEAD_DIM + g * HALF:HEAD_DIM + (g + 1) * HALF].set(w2[:, HALF:])
        else:
            w2p = w2p.at[g, :, g * HEAD_DIM:(g + 1) * HEAD_DIM].set(w2)
    if not for_keys:
        w2p = w2p.transpose(0, 2, 1)
    return [w1_by_tok, w1.astype(F32), pe.astype(F32).reshape(1, L_CMP * HEAD_DIM), w2p.astype(BF16)]


def _even_layer(x, mod_l, rope, layer_idx, g, w_in, b_forget, lq1, lk1, lq2, lk2, subln_g, w_out, final_g):
    cols, segs, rows, tsegs = _even_layout()
    w, wt = _layout_weights(w_in, cols, rows, 2 * A_HEADS * HEAD_DIM + B_HEADS * HEAD_DIM)
    k_a, k_b, forget_logits, gate, qt_a, rest_t = _projection(x, mod_l, g, rope, w, wt, segs, tsegs)
    lam_init = 0.8 - 0.6 * math.exp(-0.3 * layer_idx)
    vec = lambda a: a.astype(F32).reshape(1, -1)
    const = lambda shape: (lambda tq, tk, tile: pl.BlockSpec(shape, lambda i, hb, *t: (0, 0)))
    nb = B_HEADS // 2
    per_step = 4
    oa = _flash("diff", qt_a, k_a, rest_t, A_HEADS, per_step, 0, 0, nb, False,
                extra=[vec(lq1), vec(lk1), vec(lq2), vec(lk2), subln_g.astype(F32).reshape(LANES, 1)],
                extra_specs=[const((1, HEAD_DIM))] * 4 + [const((LANES, 1))], lam_init=lam_init)
    bias = _forget_cumsum(forget_logits, b_forget)
    bias_spec = lambda tq, tk, tile: pl.BlockSpec((None, tk, LANES), lambda i, hb, *t: (i, tile(*t)[1], 0))
    ob = _flash("fox", rest_t, k_b, rest_t, nb, per_step, 0, 0, nb + A_HEADS, False,
                extra=[bias], extra_specs=[bias_spec])
    return _out_projection(x, mod_l, gate, w_out.astype(BF16), [oa, ob], final_g)


def _odd_layer(x, mod_l, rope, g, w_in, pe_k, pe_v, w1_k, w2_k, w1_v, w2_v, conv_w, a_log, dt_bias,
               dn_norm_g, w_out, final_g):
    b, s, _ = x.shape
    cols, segs, rows, tsegs = _odd_layout()
    w, wt = _layout_weights(w_in, cols, rows, C_HEADS * HEAD_DIM)
    gate_params = jnp.zeros((2, LANES), F32).at[:, :D_HEADS].set(jnp.stack([a_log, dt_bias]).astype(F32))
    dn_q, dn_k, dn_v, small, k_rot, cmp_in, gate, q_raw_t, q_rot_t, v_t = _projection(
        x, mod_l, g, rope, w, wt, segs, tsegs, deltanet=(conv_w.astype(F32), gate_params))
    n_cmp = (s - L_CMP) // CMP_STRIDE + 1
    kcmp, vcmp_t = _compress(cmp_in, _compress_weights(pe_k, w1_k, w2_k, True),
                             _compress_weights(pe_v, w1_v, w2_v, False))
    o_cmp, sel = _cmp_select(q_raw_t, kcmp, vcmp_t, n_cmp)
    sel_spec = lambda tq, tk, tile: pl.BlockSpec((None, LANES, tq), lambda i, hb, *t: (i, 0, tile(*t)[0]))
    o_slc = _flash("sel", q_rot_t, k_rot, v_t, C_HPG, C_HPG, 0, 0, 0, True, extra=[sel], extra_specs=[sel_spec])
    o_win = _flash("win", q_rot_t, k_rot, v_t, C_HPG, C_HPG, 0, 1, 1, True)
    gates = small[:, :, :2 * D_HEADS].reshape(b, s, 2, D_HEADS).transpose(0, 3, 1, 2)
    od = _gated_deltanet(dn_q, dn_k, dn_v, gates, dn_norm_g)
    rows = _paired_head_order(w_out.shape[0])
    return _out_projection(x, mod_l, gate, w_out[rows].astype(BF16), [o_cmp, o_slc, o_win, od, small], final_g)


def kernel(x, c, positions, norm_g, w_mod, b_mod, w_out, final_norm_g, w_in_even, b_forget, lambda_q1, lambda_k1,
           lambda_q2, lambda_k2, subln_g, w_in_odd, cmp_pe_k, cmp_pe_v, cmp_w1_k, cmp_w2_k, cmp_w1_v, cmp_w2_v,
           conv_w, a_log, dt_bias, dn_norm_g):
    depth = norm_g.shape[0]
    rope = _rope_tables(positions)
    mod = _modulation(c, w_mod, b_mod)
    for l in range(depth):
        final_g = final_norm_g if l == depth - 1 else None
        i = l // 2
        if l % 2 == 0:
            x = _even_layer(x, mod[l], rope, l, norm_g[l], w_in_even[i], b_forget[i], lambda_q1[i],
                            lambda_k1[i], lambda_q2[i], lambda_k2[i], subln_g[i], w_out[l], final_g)
        else:
            x = _odd_layer(x, mod[l], rope, norm_g[l], w_in_odd[i], cmp_pe_k[i], cmp_pe_v[i], cmp_w1_k[i],
                           cmp_w2_k[i], cmp_w1_v[i], cmp_w2_v[i], conv_w[i], a_log[i], dt_bias[i], dn_norm_g[i],
                           w_out[l], final_g)
    return x
```

```python
import functools
import math

import jax
import jax.numpy as jnp
import numpy as np
from jax import lax
from jax.experimental import pallas as pl
from jax.experimental.pallas import tpu as pltpu

F32 = jnp.float32
BF16 = jnp.bfloat16
HI = lax.Precision.HIGHEST

LANES = 128
HEAD_DIM = 64
HALF = HEAD_DIM // 2
ROPE_THETA = 10000.0
NORM_EPS = 1e-6
NEG = -1e30
MASK_BIG = 1e30
LOG2E = math.log2(math.e)
QK_SCALE = HEAD_DIM ** -0.5 * LOG2E
BIAS_PIECES = 3
NORM_ROWS = 16
A_HEADS = 4
B_HEADS = 8
C_HEADS = 8
C_GROUPS = 2
C_HPG = C_HEADS // C_GROUPS
L_CMP = 32
CMP_STRIDE = 16
CMP_HIDDEN = 256
L_SEL = 64
N_SEL = 8
WINDOW = 512
N_BRANCH = 3
FORCE_BONUS = 1e4
D_HEADS = 4
D_DK = 128
D_DV = 128
CONV_WIDTH = 4
DN_CHUNK = 64
TOKEN_TILE = 512
VMEM_LIMIT = 56 * 1024 * 1024


def _cparams(sem):
    return pltpu.CompilerParams(dimension_semantics=sem, vmem_limit_bytes=VMEM_LIMIT)


def _dot(a, b, precision=None):
    return jnp.dot(a, b, precision=precision, preferred_element_type=F32)


def _dot_nt(a, b, precision=None):
    return lax.dot_general(a, b, (((1,), (1,)), ((), ())), precision=precision, preferred_element_type=F32)


def _dot_tn(a, b, precision=None):
    return lax.dot_general(a, b, (((0,), (0,)), ((), ())), precision=precision, preferred_element_type=F32)


def _split_bf16(a):
    hi = a.astype(BF16)
    return hi, (a - hi.astype(F32)).astype(BF16)


def _softplus(z):
    return jnp.maximum(z, 0.0) + jnp.log1p(jnp.exp(-jnp.abs(z)))


def _silu(z):
    return z * jax.nn.sigmoid(z)


def _rope_table_kernel(pos_ref, inv_ref, cos_ref, sin_ref, cos_t_ref, sin_t_ref):
    ang = pos_ref[...].astype(F32) * inv_ref[...]
    lane = lax.broadcasted_iota(jnp.int32, (1, LANES), 1)
    cos = jnp.cos(ang)
    sin = jnp.where(lane < 2 * HALF, -1.0, 1.0) * jnp.sin(ang)
    cos_ref[...] = cos
    sin_ref[...] = sin
    cos_t_ref[...] = cos.T
    sin_t_ref[...] = sin.T


def _rope_tables(positions):
    b, s = positions.shape
    inv = ROPE_THETA ** (-jnp.arange(0, HEAD_DIM, 2, dtype=F32) / HEAD_DIM)
    inv = jnp.tile(inv, 4).reshape(1, LANES)
    tok = jax.ShapeDtypeStruct((b, s, LANES), F32)
    feat = jax.ShapeDtypeStruct((b, LANES, s), F32)
    return pl.pallas_call(
        _rope_table_kernel, grid=(b,),
        in_specs=[pl.BlockSpec((None, s, 1), lambda i: (i, 0, 0)),
                  pl.BlockSpec((1, LANES), lambda i: (0, 0))],
        out_specs=[pl.BlockSpec((None, s, LANES), lambda i: (i, 0, 0))] * 2
        + [pl.BlockSpec((None, LANES, s), lambda i: (i, 0, 0))] * 2,
        out_shape=[tok, tok, feat, feat], compiler_params=_cparams(("parallel",)), name="rope_tables",
    )(positions.reshape(b, s, 1), inv)


def _mod_kernel(c_ref, w_ref, b_ref, o_ref):
    o_ref[...] = _dot(_silu(c_ref[...]), w_ref[...], HI) + b_ref[...]


def _modulation(c, w_mod, b_mod):
    depth, d, n = w_mod.shape
    b = c.shape[0]
    tn = 1024
    mod = pl.pallas_call(
        _mod_kernel, grid=(depth, n // tn),
        in_specs=[pl.BlockSpec((b, d), lambda l, j: (0, 0)),
                  pl.BlockSpec((None, d, tn), lambda l, j: (l, 0, j)),
                  pl.BlockSpec((None, 1, tn), lambda l, j: (l, 0, j))],
        out_specs=pl.BlockSpec((None, b, tn), lambda l, j: (l, 0, j)),
        out_shape=jax.ShapeDtypeStruct((depth, b, n), F32),
        compiler_params=_cparams(("parallel", "parallel")), name="modulation",
    )(c, w_mod, b_mod.reshape(depth, 1, n))
    return mod.reshape(depth, b, 3, 1, d)


def _proj_kernel(x_ref, *refs, segs, tsegs, deltanet):
    _proj_body(x_ref[...], *refs, segs=segs, tsegs=tsegs, deltanet=deltanet)


def _proj_body(x, shift_ref, scale_ref, g_ref, cos_ref, sin_ref, cos_t_ref, sin_t_ref, w_ref, wt_ref,
               *rest, segs, tsegs, deltanet):
    h = x * lax.rsqrt(jnp.mean(x * x, axis=-1, keepdims=True) + NORM_EPS) * g_ref[...]
    h = (h * (1.0 + scale_ref[...]) + shift_ref[...]).astype(BF16)
    if deltanet:
        conv_ref, gparam_ref, halo_ref = rest[0], rest[1], rest[-1]
        outs = list(rest[2:-1])

        @pl.when(pl.program_id(1) == 0)
        def _():
            halo_ref[...] = jnp.zeros(halo_ref.shape, F32)
    else:
        outs = list(rest)
    ts = x.shape[0]
    step = 4 * LANES
    conv_off = 0
    work = []

    def tok_item(kind, c0, a, e, o_ref, cols):
        def epilogue(acc):
            if kind == "rope":
                cos, sin = cos_ref[...], sin_ref[...]
                for j in range(0, e - a, LANES):
                    blk = acc[:, j:j + LANES]
                    rot = blk * cos + pltpu.roll(blk, 2 * HALF, axis=1) * sin
                    o_ref[:, a - c0 + j:a - c0 + j + LANES] = rot.astype(o_ref.dtype)
            elif kind == "dn_gates":
                lane = lax.broadcasted_iota(jnp.int32, (1, LANES), 1)
                log_decay = -jnp.exp(gparam_ref[0:1, :]) * _softplus(acc + gparam_ref[1:2, :])
                o_ref[...] = jnp.where(lane < D_HEADS, log_decay,
                                       jnp.where(lane < 2 * D_HEADS, jax.nn.sigmoid(acc), acc))
            elif kind in ("dn_q", "dn_k", "dn_v"):
                wc = conv_ref[:, cols]
                ext = jnp.concatenate([halo_ref[:, cols], acc], axis=0)
                halo_ref[:, cols] = acc[ts - 8:, :]
                y = acc * wc[CONV_WIDTH - 1:CONV_WIDTH, :]
                for back in range(1, CONV_WIDTH):
                    y = y + ext[8 - back:8 - back + ts, :] * wc[CONV_WIDTH - 1 - back:CONV_WIDTH - back, :]
                y = _silu(y)
                for j in range(0, e - a, LANES):
                    blk = y[:, j:j + LANES]
                    if kind != "dn_v":
                        blk = blk * lax.rsqrt(jnp.sum(blk * blk, axis=-1, keepdims=True) + NORM_EPS)
                    if kind == "dn_q":
                        blk = blk * (D_DK ** -0.5)
                    o_ref[:, a - c0 + j:a - c0 + j + LANES] = blk
            else:
                o_ref[:, a - c0:e - c0] = acc.astype(o_ref.dtype)
        return (lambda: _dot(h, w_ref[:, a:e])), epilogue

    def feat_item(kind, r0, a, e, o_ref, raw_ref):
        def epilogue(acc):
            if raw_ref is not None:
                raw_ref[a - r0:e - r0, :] = acc
            if kind in ("rope", "rope+raw"):
                cos, sin = cos_t_ref[...], sin_t_ref[...]
                for j in range(0, e - a, LANES):
                    blk = acc[j:j + LANES, :]
                    rot = blk * cos + pltpu.roll(blk, 2 * HALF, axis=0) * sin
                    o_ref[a - r0 + j:a - r0 + j + LANES, :] = rot.astype(o_ref.dtype)
            else:
                o_ref[a - r0:e - r0, :] = acc.astype(o_ref.dtype)
        return (lambda: _dot_nt(wt_ref[a:e, :], h)), epilogue

    for kind, c0, c1 in segs:
        o_ref = outs.pop(0)
        for a in range(c0, c1, step):
            e = min(a + step, c1)
            cols = None
            if kind in ("dn_q", "dn_k", "dn_v"):
                cols = slice(conv_off, conv_off + (e - a))
                conv_off += e - a
            work.append(tok_item(kind, c0, a, e, o_ref, cols))
    for kind, r0, r1 in tsegs:
        raw_ref = outs.pop(0) if kind == "rope+raw" else None
        o_ref = outs.pop(0)
        for a in range(r0, r1, step):
            work.append(feat_item(kind, r0, a, min(a + step, r1), o_ref, raw_ref))
    acc = work[0][0]()
    for i, (_, epilogue) in enumerate(work):
        nxt = work[i + 1][0]() if i + 1 < len(work) else None
        epilogue(acc)
        acc = nxt


def _projection(x, mod_l, g, rope, w, wt, segs, tsegs, deltanet=None):
    b, s, d = x.shape
    ts = min(TOKEN_TILE, s)
    in_specs, args, out_specs, out_shapes, scratch, sem = _projection_operands(
        b, s, d, ts, mod_l, g, rope, w, wt, segs, tsegs, deltanet)
    return pl.pallas_call(
        functools.partial(_proj_kernel, segs=segs, tsegs=tsegs, deltanet=deltanet is not None), grid=(b, s // ts),
        in_specs=[pl.BlockSpec((None, ts, d), lambda i, j: (i, j, 0))] + in_specs,
        out_specs=out_specs, out_shape=out_shapes, scratch_shapes=scratch,
        compiler_params=_cparams(sem), name="adaln_in_proj",
    )(x, *args)


def _projection_operands(b, s, d, ts, mod_l, g, rope, w, wt, segs, tsegs, deltanet):
    row = lambda width: pl.BlockSpec((None, ts, width), lambda i, j: (i, j, 0))
    col = lambda height: pl.BlockSpec((None, height, ts), lambda i, j: (i, 0, j))
    out_specs, out_shapes = [], []
    for kind, c0, c1 in segs:
        out_specs.append(row(c1 - c0))
        out_shapes.append(jax.ShapeDtypeStruct((b, s, c1 - c0), BF16 if kind in ("rope", "bf16") else F32))
    for kind, r0, r1 in tsegs:
        if kind == "rope+raw":
            out_specs.append(col(r1 - r0))
            out_shapes.append(jax.ShapeDtypeStruct((b, r1 - r0, s), F32))
        out_specs.append(col(r1 - r0))
        out_shapes.append(jax.ShapeDtypeStruct((b, r1 - r0, s), BF16))
    modspec = lambda k: pl.BlockSpec((None, None, 1, d), lambda i, j: (i, k, 0, 0))
    resident = lambda a: pl.BlockSpec(a.shape, lambda i, j: (0, 0), pipeline_mode=pl.Buffered(1))
    in_specs = [modspec(0), modspec(1), pl.BlockSpec((1, d), lambda i, j: (0, 0)),
                row(LANES), row(LANES), col(LANES), col(LANES), resident(w), resident(wt)]
    args = [mod_l, mod_l, g.reshape(1, d), *rope, w, wt]
    scratch = []
    if deltanet is not None:
        in_specs += [pl.BlockSpec(a.shape, lambda i, j: (0, 0)) for a in deltanet]
        args += list(deltanet)
        scratch = [pltpu.VMEM((8, deltanet[0].shape[1]), F32)]
    sem = ("parallel", "arbitrary") if deltanet is not None else ("parallel", "parallel")
    return in_specs, args, out_specs, out_shapes, scratch, sem


def _cum_kernel(x_ref, bias_ref, o_ref):
    s = x_ref.shape[0]
    ii = lax.broadcasted_iota(jnp.int32, (LANES, LANES), 0)
    jj = lax.broadcasted_iota(jnp.int32, (LANES, LANES), 1)
    lower = (ii >= jj).astype(F32)
    lane = lax.broadcasted_iota(jnp.int32, (1, LANES), 1)
    carry = jnp.zeros((1, LANES), F32)
    for r0 in range(0, s, LANES):
        z = x_ref[r0:r0 + LANES, :] + bias_ref[...]
        logf = jnp.minimum(z, 0.0) - jnp.log1p(jnp.exp(-jnp.abs(z)))
        loc = _dot(lower, logf, HI) + carry
        carry = loc[LANES - 1:LANES, :]
        val = loc * (-LOG2E)
        hi = val.astype(BF16)
        rest = val - hi.astype(F32)
        mid = rest.astype(BF16)
        lo = (rest - mid.astype(F32)).astype(BF16)
        piece = jnp.where(lane % BIAS_PIECES == 0, hi, jnp.where(lane % BIAS_PIECES == 1, mid, lo))
        o_ref[r0:r0 + LANES, :] = jnp.where(lane < B_HEADS * BIAS_PIECES, piece, jnp.zeros_like(piece))


def _forget_cumsum(logits, b_forget):
    b, s, _ = logits.shape
    bias = jnp.zeros((1, LANES), F32).at[0, :B_HEADS * BIAS_PIECES].set(jnp.repeat(b_forget.astype(F32), BIAS_PIECES))
    return pl.pallas_call(
        _cum_kernel, grid=(b,),
        in_specs=[pl.BlockSpec((None, s, LANES), lambda i: (i, 0, 0)), pl.BlockSpec((1, LANES), lambda i: (0, 0))],
        out_specs=pl.BlockSpec((None, s, LANES), lambda i: (i, 0, 0)),
        out_shape=jax.ShapeDtypeStruct((b, s, LANES), BF16),
        compiler_params=_cparams(("parallel",)), name="forget_cumsum",
    )(logits, bias)


def _triangle_step(t, nq):
    qi = sum((t >= j * (j + 1) // 2).astype(jnp.int32) for j in range(1, nq))
    return qi, t - qi * (qi + 1) // 2


def _flash_kernel(*refs, mode, tq, tk, nq, nblk, shared_kv, lam_init):
    qt_ref, k_ref, vt_ref = refs[:3]
    m_ref, acc_ref = refs[-2:]
    o_ref = refs[-3]
    extra = refs[3:-3]
    if mode == "win":
        qi, ki = pl.program_id(2), pl.program_id(3)
        kv, last = qi - 1 + ki, ki == pl.num_programs(3) - 1
    else:
        qi, ki = _triangle_step(pl.program_id(2), nq)
        kv, last = ki, ki == qi
    row = lax.broadcasted_iota(jnp.int32, (LANES, 1), 0)
    if mode == "fox":
        slot_rows = (row < HEAD_DIM, row >= HEAD_DIM)
    else:
        slot_rows = ((row % HEAD_DIM) < HALF, (row % HEAD_DIM) >= HALF)
    vrows = acc_ref.shape[1] - NORM_ROWS
    block = lambda j: slice(j * LANES, (j + 1) * LANES)
    half = tq // 2

    @pl.when(ki == 0)
    def _():
        m_ref[...] = jnp.full(m_ref.shape, NEG, F32)
        acc_ref[...] = jnp.zeros(acc_ref.shape, F32)

    def step(kind):
        rr = lax.broadcasted_iota(jnp.int32, (LANES, tq), 0)
        k_extra = [None, None]
        for s in range(2):
            if mode == "sel":
                cb = (kv * tk + lax.broadcasted_iota(jnp.int32, (tk, LANES), 0)) // L_SEL
                ll = lax.broadcasted_iota(jnp.int32, (tk, LANES), 1)
                k_extra[s] = jnp.where(ll == cb + s * (LANES // 4), MASK_BIG, 0.0).astype(BF16)

        def q_extra(j, s):
            if mode == "sel":
                return extra[0][...]
            if mode == "fox":
                first = ((pl.program_id(1) * nblk + j) * 2 + s) * BIAS_PIECES
                return jnp.where((rr >= first) & (rr < first + BIAS_PIECES), 1.0, 0.0).astype(BF16)
            return None
        lo, hi, everything = slice(0, half), slice(half, tq), slice(0, tq)
        parts = {"full": [(everything, everything, None)],
                 "lower": [(lo, lo, lo), (everything, hi, hi)],
                 "upper": [(everything, lo, lo), (hi, hi, hi)]}[kind]
        tri_k = lax.broadcasted_iota(jnp.int32, (half, half), 0)
        tri_q = lax.broadcasted_iota(jnp.int32, (half, half), 1)
        tri = tri_k <= tri_q if kind == "lower" else tri_k > tri_q
        ones = jnp.ones((NORM_ROWS, tk), BF16)
        sts, vts = [], []
        for j in range(nblk):
            qt = qt_ref[block(j), :]
            k = k_ref[...] if shared_kv else k_ref[:, block(j)]
            base = 0 if shared_kv else j * LANES
            for s in range(2):
                qs = jnp.where(slot_rows[s], qt, jnp.zeros_like(qt))
                q_more = q_extra(j, s)
                if q_more is not None:
                    k_more = extra[0][...] if mode == "fox" else k_extra[s]
                    k_all, q_all = jnp.concatenate([k, k_more], axis=1), jnp.concatenate([qs, q_more], axis=0)
                else:
                    k_all, q_all = k, qs
                sts.append([_dot(k_all[krows], q_all[:, lanes]) for krows, lanes, _ in parts])
                v0 = base if vrows == LANES else base + s * vrows
                vts.append(jnp.concatenate([vt_ref[v0:v0 + vrows, :], ones], axis=0))
        for idx in range(2 * nblk):
            for (krows, lanes, tri_rows), st in zip(parts, sts[idx]):
                if tri_rows is not None:
                    if krows == tri_rows:
                        st = jnp.where(tri, st, NEG)
                    elif tri_rows == lo:
                        st = jnp.concatenate([jnp.where(tri, st[:half], NEG), st[half:]], axis=0)
                    else:
                        st = jnp.concatenate([st[:half], jnp.where(tri, st[half:], NEG)], axis=0)
                m_prev = m_ref[idx, :, lanes]
                m_new = jnp.maximum(m_prev, jnp.max(st, axis=0, keepdims=True))
                alpha = jnp.exp2(m_prev - m_new)
                p = jnp.exp2(st - m_new).astype(BF16)
                acc_ref[idx, :, lanes] = alpha * acc_ref[idx, :, lanes] + _dot(vts[idx][:, krows], p)
                m_ref[idx, :, lanes] = m_new

    if mode == "win":
        pl.when((ki == 0) & (kv >= 0))(functools.partial(step, "upper"))
        pl.when(ki == 1)(functools.partial(step, "lower"))
    else:
        pl.when(ki < qi)(functools.partial(step, "full"))
        pl.when(ki == qi)(functools.partial(step, "lower"))

    @pl.when(last)
    def _():
        for j in range(nblk):
            o0 = acc_ref[2 * j, :vrows, :] / acc_ref[2 * j, vrows:vrows + 1, :]
            o1 = acc_ref[2 * j + 1, :vrows, :] / acc_ref[2 * j + 1, vrows:vrows + 1, :]
            if mode == "diff":
                lq1, lk1, lq2, lk2, subg = (r[...] for r in extra)
                lam = (jnp.exp(jnp.sum(lq1 * lk1, axis=1, keepdims=True))
                       - jnp.exp(jnp.sum(lq2 * lk2, axis=1, keepdims=True)) + lam_init)
                o = o0 - lam * o1
                o = o * lax.rsqrt(jnp.mean(o * o, axis=0, keepdims=True) + NORM_EPS) * subg * (1.0 - lam_init)
            else:
                o = jnp.concatenate([o0, o1], axis=0)
            o_ref[:, block(j)] = o.T.astype(o_ref.dtype)


def _flash(mode, qt, k, vt, n_qblocks, nblk, q0, k0, v0, shared_kv, extra=(), extra_specs=(), lam_init=0.0):
    b, s, _ = k.shape
    tq = tk = min(TOKEN_TILE, s)
    nq = s // tq
    if mode == "win":
        assert WINDOW == tk
        steps, sem = (nq, 2), ("parallel", "arbitrary")
        tile = lambda qi, ki: (qi, jnp.maximum(qi - 1 + ki, 0))
    else:
        steps, sem = (nq * (nq + 1) // 2,), ("arbitrary",)
        tile = lambda t: _triangle_step(t, nq)
    assert n_qblocks % nblk == 0 and q0 % nblk == 0 and (shared_kv or (k0 % nblk == 0 and v0 % nblk == 0))
    wide = nblk * LANES
    if shared_kv:
        k_spec = pl.BlockSpec((None, tk, LANES), lambda i, hb, *t: (i, tile(*t)[1], k0))
        v_spec = pl.BlockSpec((None, LANES, tk), lambda i, hb, *t: (i, v0, tile(*t)[1]))
    else:
        k_spec = pl.BlockSpec((None, tk, wide), lambda i, hb, *t: (i, tile(*t)[1], k0 // nblk + hb))
        v_spec = pl.BlockSpec((None, wide, tk), lambda i, hb, *t: (i, v0 // nblk + hb, tile(*t)[1]))
    in_specs = [pl.BlockSpec((None, wide, tq), lambda i, hb, *t: (i, q0 // nblk + hb, tile(*t)[0])), k_spec, v_spec]
    for spec in extra_specs:
        in_specs.append(spec(tq, tk, tile))
    vrows = LANES if mode == "diff" else HEAD_DIM
    return pl.pallas_call(
        functools.partial(_flash_kernel, mode=mode, tq=tq, tk=tk, nq=nq, nblk=nblk, shared_kv=shared_kv,
                          lam_init=lam_init),
        grid=(b, n_qblocks // nblk) + steps, in_specs=in_specs,
        out_specs=pl.BlockSpec((None, tq, wide), lambda i, hb, *t: (i, tile(*t)[0], hb)),
        out_shape=jax.ShapeDtypeStruct((b, s, n_qblocks * LANES), BF16),
        scratch_shapes=[pltpu.VMEM((2 * nblk, 1, tq), F32), pltpu.VMEM((2 * nblk, vrows + NORM_ROWS, tq), F32)],
        compiler_params=_cparams(("parallel", "parallel") + sem),
        name="flash_" + mode,
    )(qt, k, vt, *extra)


def _compress_kernel(xk_ref, xv_ref, w1_by_tok_k, w1_k, pe_k, w2_k, w1_by_tok_v, w1_v, pe_v, w2_v, ok_ref, ov_ref):
    cn = xk_ref.shape[0] // CMP_STRIDE

    def run(x_ref, w1_by_tok, w1, pe, w2, o_ref, transposed):
        pe_term = _dot(pe[...], w1[...], HI)
        ab = jnp.zeros((cn, C_GROUPS * 2 * CMP_HIDDEN), F32)
        for tok in range(CMP_STRIDE):
            rows = x_ref[pl.ds(tok, cn, stride=CMP_STRIDE), :]
            ab = ab + _dot(rows.astype(BF16), w1_by_tok[tok])
        out = 0.0
        for g in range(C_GROUPS):
            first = ab[:, g * 2 * CMP_HIDDEN:(g * 2 + 1) * CMP_HIDDEN]
            second = ab[:, (g * 2 + 1) * CMP_HIDDEN:(g + 1) * 2 * CMP_HIDDEN]
            hid = first + pltpu.roll(second, cn - 1, axis=0) + pe_term
            act = _silu(hid).astype(BF16)
            out = out + (_dot_nt(w2[g], act) if transposed else _dot(act, w2[g]))
        o_ref[...] = out
    run(xk_ref, w1_by_tok_k, w1_k, pe_k, w2_k, ok_ref, False)
    run(xv_ref, w1_by_tok_v, w1_v, pe_v, w2_v, ov_ref, True)


def _compress(cmp_in, weights_k, weights_v):
    b, s, width = cmp_in.shape
    cn = s // CMP_STRIDE
    full = lambda a: pl.BlockSpec(a.shape, lambda i: (0,) * a.ndim)
    return pl.pallas_call(
        _compress_kernel, grid=(b,),
        in_specs=[pl.BlockSpec((None, s, LANES), lambda i: (i, 0, 0)), pl.BlockSpec((None, s, LANES), lambda i: (i, 0, 1))]
        + [full(a) for a in weights_k + weights_v],
        out_specs=[pl.BlockSpec((None, cn, LANES), lambda i: (i, 0, 0)),
                   pl.BlockSpec((None, LANES, cn), lambda i: (i, 0, 0))],
        out_shape=[jax.ShapeDtypeStruct((b, cn, LANES), F32), jax.ShapeDtypeStruct((b, LANES, cn), F32)],
        compiler_params=_cparams(("parallel",)), name="nsa_compress",
    )(cmp_in, cmp_in, *weights_k, *weights_v)


def _cmp_select_kernel(qt_ref, kc_ref, vct_ref, o_ref, sel_ref, *, tq, n_blk, n_cmp):
    qi = pl.program_id(1)
    cn = kc_ref.shape[0]
    kc_hi, kc_lo = _split_bf16(kc_ref[...])
    vct = vct_ref[...].astype(BF16)
    row = lax.broadcasted_iota(jnp.int32, (LANES, 1), 0)
    slot_rows = ((row % HEAD_DIM) < HALF, (row % HEAD_DIM) >= HALF)
    t_row = qi * tq + lax.broadcasted_iota(jnp.int32, (1, tq), 1)
    m_col = lax.broadcasted_iota(jnp.int32, (cn, 1), 0)
    valid = (m_col * CMP_STRIDE + L_CMP - 1 <= t_row) & (m_col < n_cmp)
    any_valid = (t_row >= L_CMP - 1).astype(F32)
    heads = [(p_blk, s) for p_blk in range(C_HPG) for s in range(C_GROUPS)]
    scores = []
    for p_blk, s in heads:
        q_hi, q_lo = _split_bf16(jnp.where(slot_rows[s], qt_ref[p_blk * LANES:(p_blk + 1) * LANES, :], 0.0))
        scores.append(_dot(kc_hi, q_hi) + _dot(kc_lo, q_hi) + _dot(kc_hi, q_lo))
    probs = []
    for sc in scores:
        sc = jnp.where(valid, sc, NEG)
        e = jnp.exp2(sc - jnp.max(sc, axis=0, keepdims=True))
        probs.append(e / jnp.sum(e, axis=0, keepdims=True) * any_valid)
    outs = [_dot(vct[s * HEAD_DIM:(s + 1) * HEAD_DIM, :], p.astype(BF16)) for (_, s), p in zip(heads, probs)]
    psum = [sum(p for (_, s), p in zip(heads, probs) if s == g) for g in range(C_GROUPS)]
    for p_blk in range(C_HPG):
        o_ref[:, p_blk * LANES:(p_blk + 1) * LANES] = jnp.concatenate(
            outs[C_GROUPS * p_blk:C_GROUPS * (p_blk + 1)], axis=0).T.astype(o_ref.dtype)
    jb = lax.broadcasted_iota(jnp.int32, (n_blk, cn), 0)
    mm = lax.broadcasted_iota(jnp.int32, (n_blk, cn), 1)
    overlap = ((mm * CMP_STRIDE < jb * L_SEL + L_SEL) & (mm * CMP_STRIDE + L_CMP > jb * L_SEL)
               & (mm < n_cmp)).astype(BF16)
    j = lax.broadcasted_iota(jnp.int32, (n_blk, tq), 0)
    cur = (qi * tq + lax.broadcasted_iota(jnp.int32, (n_blk, tq), 1)) // L_SEL
    forced = (j == 0) | (j == cur) | (j == cur - 1)
    n_top = min(N_SEL, n_blk)
    pad_rows = LANES // 4 - n_blk
    parts = []
    for s in range(C_GROUPS):
        p_hi, p_lo = _split_bf16(psum[s])
        imp = _dot(overlap, p_hi) + _dot(overlap, p_lo)
        score = jnp.where(j > cur, NEG, imp + jnp.where(forced, FORCE_BONUS, 0.0))
        rank = jnp.zeros((n_blk, tq), jnp.int32)
        for jp in range(n_blk):
            r = score[jp:jp + 1, :]
            rank = rank + ((r > score) | ((r == score) & (jp < j))).astype(jnp.int32)
        parts.append(jnp.where(rank < n_top, 0.0, -1.0))
        if pad_rows:
            parts.append(jnp.zeros((pad_rows, tq), F32))
    parts.append(jnp.zeros((LANES // 2, tq), F32))
    sel_ref[...] = jnp.concatenate(parts, axis=0).astype(sel_ref.dtype)


def _cmp_select(q_raw_t, kcmp, vcmp_t, n_cmp):
    b, width, s = q_raw_t.shape
    cn = kcmp.shape[1]
    tq = min(TOKEN_TILE, s)
    n_blk = s // L_SEL
    return pl.pallas_call(
        functools.partial(_cmp_select_kernel, tq=tq, n_blk=n_blk, n_cmp=n_cmp), grid=(b, s // tq),
        in_specs=[pl.BlockSpec((None, width, tq), lambda i, j: (i, 0, j)),
                  pl.BlockSpec((None, cn, LANES), lambda i, j: (i, 0, 0)),
                  pl.BlockSpec((None, LANES, cn), lambda i, j: (i, 0, 0))],
        out_specs=[pl.BlockSpec((None, tq, width), lambda i, j: (i, j, 0)),
                   pl.BlockSpec((None, LANES, tq), lambda i, j: (i, 0, j))],
        out_shape=[jax.ShapeDtypeStruct((b, s, width), BF16), jax.ShapeDtypeStruct((b, LANES, s), BF16)],
        compiler_params=_cparams(("parallel", "parallel")), name="nsa_cmp_select",
    )(q_raw_t, kcmp, vcmp_t)


GDN_HEADS = 2
GDN_BATCH = 2
GDN_UNROLL = 8


def _gdn_kernel(q_ref, k_ref, v_ref, gate_ref, ng_ref, o_ref, mc_s, n_s, gt_s):
    s_len = q_ref.shape[0]
    c = DN_CHUNK
    n_chunks = s_len // c
    lanes = [slice(j * LANES, (j + 1) * LANES) for j in range(GDN_HEADS)]

    sc = GDN_BATCH * c
    ii = lax.broadcasted_iota(jnp.int32, (sc, sc), 0)
    jj = lax.broadcasted_iota(jnp.int32, (sc, sc), 1)
    same = (ii // c) == (jj // c)
    causal, strict, upper, eye = same & (ii >= jj), same & (ii > jj), same & (ii <= jj), ii == jj
    chunk_end = same & (jj % c == c - 1)
    ng = ng_ref[...]

    def load(n, j):
        r0 = pl.multiple_of(n * sc, sc)
        return (q_ref[pl.ds(r0, sc), lanes[j]], k_ref[pl.ds(r0, sc), lanes[j]], v_ref[pl.ds(r0, sc), lanes[j]],
                gate_ref[j, pl.ds(r0, sc), 0:1], gate_ref[j, pl.ds(r0, sc), 1:2])

    def prepare(q, k, v, g, beta):
        g_row = jnp.sum(jnp.where(eye, g, 0.0), axis=0, keepdims=True)
        gc_col = jnp.sum(jnp.where(causal, g_row, 0.0), axis=1, keepdims=True)
        gc_row = jnp.sum(jnp.where(upper, g, 0.0), axis=0, keepdims=True)
        g_last = jnp.sum(jnp.where(chunk_end, gc_row, 0.0), axis=1, keepdims=True)
        decay = jnp.where(causal, jnp.exp(jnp.where(causal, gc_col - gc_row, 0.0)), 0.0)
        eg = jnp.exp(gc_col)
        kb = k * beta
        kbf = k.astype(BF16)
        raw = _dot_nt(kb.astype(BF16), kbf)
        raw_qk = _dot_nt(q.astype(BF16), kbf)
        rhs = jnp.concatenate([v * beta, kb * eg], axis=1).astype(BF16)
        kd = k * jnp.exp(g_last - gc_col)
        kd_t = [kd[t * c:(t + 1) * c].T.astype(BF16) for t in range(GDN_BATCH)]
        g_tot = [jnp.exp(g_last[t * c:t * c + 1]) for t in range(GDN_BATCH)]
        return raw, raw_qk, decay, rhs, kd_t, g_tot, q * eg

    def solve(prepared):
        n = len(prepared)
        decay = [p[2] for p in prepared]
        x = [-jnp.where(strict, prepared[i][0] * decay[i], 0.0) for i in range(n)]
        inv = [jnp.where(eye, 1.0, 0.0) + x[i] for i in range(n)]
        xb = [x[i].astype(BF16) for i in range(n)]
        x = [_dot(xb[i], xb[i]) for i in range(n)]
        for _ in range(int(math.log2(c)) - 2):
            xb = [x[i].astype(BF16) for i in range(n)]
            both = [_dot(jnp.concatenate([xb[i], inv[i].astype(BF16)], axis=0), xb[i]) for i in range(n)]
            x = [both[i][:sc] for i in range(n)]
            inv = [inv[i] + both[i][sc:] for i in range(n)]
        last = [_dot(inv[i].astype(BF16), x[i].astype(BF16)) for i in range(n)]
        inv = [inv[i] + last[i] for i in range(n)]
        sol = [_dot(inv[i].astype(BF16), prepared[i][3]).astype(BF16) for i in range(n)]
        qk = [(prepared[i][1] * decay[i]).astype(BF16) for i in range(n)]
        qo = [_dot(qk[i], sol[i]) for i in range(n)]
        mn = [[_dot(prepared[i][4][t], sol[i][t * c:(t + 1) * c]) for t in range(GDN_BATCH)] for i in range(n)]
        return [(qo[i], prepared[i][6], mn[i], prepared[i][5]) for i in range(n)]

    def store(n, j, qo, q_dec, mn, g_tot):
        r0 = pl.multiple_of(n * sc, sc)
        o_ref[pl.ds(r0, sc), lanes[j]] = qo[:, :D_DV]
        q_eff = (q_dec - qo[:, D_DV:]).astype(BF16)
        for t in range(GDN_BATCH):
            n_s[j, n * GDN_BATCH + t] = mn[t][:, :D_DV]
            mc_s[j, n * GDN_BATCH + t, :D_DK, :] = mn[t][:, D_DV:].astype(BF16)
            mc_s[j, n * GDN_BATCH + t, D_DK:, :] = q_eff[t * c:(t + 1) * c]
            gt_s[j, n * GDN_BATCH + t] = jnp.broadcast_to(g_tot[t], (8, LANES))

    def prepare_some(i, carry):
        items = [(i * GDN_UNROLL + u, j) for u in range(GDN_UNROLL) for j in range(GDN_HEADS)]
        results = solve([prepare(*operands) for operands in [load(n, j) for n, j in items]])
        for (n, j), res in zip(items, results):
            store(n, j, *res)
        return carry

    lax.fori_loop(0, n_chunks // (GDN_BATCH * GDN_UNROLL), prepare_some, 0)

    def advance(n, states):
        r0 = pl.multiple_of(n * c, c)
        operands = [(o_ref[pl.ds(r0, c), lanes[j]], gt_s[j, n], mc_s[j, n], n_s[j, n]) for j in range(GDN_HEADS)]
        prods = [_dot(operands[j][2], states[j].astype(BF16)) for j in range(GDN_HEADS)]
        out = [states[j] * operands[j][1][0:1, :] - prods[j][:D_DK] + operands[j][3] for j in range(GDN_HEADS)]
        for j in range(GDN_HEADS):
            o_ref[pl.ds(r0, c), lanes[j]] = operands[j][0] + prods[j][D_DK:]
        return tuple(out)

    lax.fori_loop(0, n_chunks, advance, tuple(jnp.zeros((D_DK, D_DV), F32) for _ in range(GDN_HEADS)))
    for j in range(GDN_HEADS):
        o = o_ref[:, lanes[j]]
        o_ref[:, lanes[j]] = o * lax.rsqrt(jnp.mean(o * o, axis=-1, keepdims=True) + NORM_EPS) * ng


def _gated_deltanet(q, k, v, gates, norm_g):
    b, s, _ = q.shape
    hp, width = GDN_HEADS, GDN_HEADS * LANES
    groups = D_HEADS // hp
    n_chunks = s // DN_CHUNK
    assert n_chunks % (GDN_BATCH * GDN_UNROLL) == 0
    col = pl.BlockSpec((None, s, width), lambda i, h: (i, 0, h))
    return pl.pallas_call(
        _gdn_kernel, grid=(b, groups),
        in_specs=[col, col, col, pl.BlockSpec((None, hp, s, 2), lambda i, h: (i, h, 0, 0)),
                  pl.BlockSpec((1, D_DV), lambda i, h: (0, 0))],
        out_specs=col,
        out_shape=jax.ShapeDtypeStruct((b, s, D_HEADS * D_DV), F32),
        scratch_shapes=[pltpu.VMEM((hp, n_chunks, D_DK + DN_CHUNK, D_DV), BF16),
                        pltpu.VMEM((hp, n_chunks, D_DK, D_DV), F32), pltpu.VMEM((hp, n_chunks, 8, LANES), F32)],
        compiler_params=_cparams(("parallel", "parallel")), name="gated_deltanet",
    )(q, k, v, gates, norm_g.astype(F32).reshape(1, D_DV))


def _out_kernel(*refs, odd, final):
    rest = list(refs[:-1])
    fin_ref = rest.pop() if final else None
    out = _out_body(*rest, odd=odd)
    if final:
        out = out * lax.rsqrt(jnp.mean(out * out, axis=-1, keepdims=True) + NORM_EPS) * fin_ref[...]
    refs[-1][...] = out


def _out_in_kernel(*refs, odd, n_out_inputs, segs, tsegs, deltanet):
    x_new = _out_body(*refs[:n_out_inputs], odd=odd)
    proj_refs = list(refs[n_out_inputs:])
    n_proj_inputs = 9 + (2 if deltanet else 0)
    x_out_ref = proj_refs.pop(n_proj_inputs)
    x_out_ref[...] = x_new
    _proj_body(x_new, *proj_refs, segs=segs, tsegs=tsegs, deltanet=deltanet)


def _out_body(x_ref, mg_ref, gate_ref, w_ref, *rest, odd):
    half = w_ref.shape[0] // 2
    sg = _silu(gate_ref[...].astype(F32))
    if odd:
        cmp_ref, slc_ref, win_ref, od_ref, small_ref = rest
        lane = lax.broadcasted_iota(jnp.int32, (1, LANES), 1)
        bg = jax.nn.sigmoid(small_ref[...])
        blocks = []
        for p_blk in range(C_HPG):
            sl = slice(p_blk * LANES, (p_blk + 1) * LANES)
            acc = 0.0
            for br, ref in enumerate((cmp_ref, slc_ref, win_ref)):
                ca = 2 * D_HEADS + p_blk * N_BRANCH + br
                cb = 2 * D_HEADS + (p_blk + C_HPG) * N_BRANCH + br
                acc = acc + jnp.where(lane < HEAD_DIM, bg[:, ca:ca + 1], bg[:, cb:cb + 1]) * ref[:, sl]
            blocks.append(acc)
        first = jnp.concatenate(blocks, axis=1)
        second = od_ref[...]
    else:
        first, second = rest[0][...], rest[1][...]
    y = (_dot((first * sg[:, :half]).astype(BF16), w_ref[:half, :])
         + _dot((second * sg[:, half:]).astype(BF16), w_ref[half:, :]))
    return x_ref[...] + mg_ref[...] * y


def _out_operands(x, mod_l, gate, w, branches, ts):
    d = x.shape[-1]
    row = lambda width: pl.BlockSpec((None, ts, width), lambda i, j: (i, j, 0))
    in_specs = [row(d), pl.BlockSpec((None, None, 1, d), lambda i, j: (i, 2, 0, 0)), row(gate.shape[-1]),
                pl.BlockSpec(w.shape, lambda i, j: (0, 0), pipeline_mode=pl.Buffered(1))]
    in_specs += [row(a.shape[-1]) for a in branches]
    return in_specs, [x, mod_l, gate, w, *branches]


def _out_projection(x, mod_l, gate, w, branches, final_g=None):
    b, s, d = x.shape
    ts = min(TOKEN_TILE, s)
    in_specs, args = _out_operands(x, mod_l, gate, w, branches, ts)
    if final_g is not None:
        in_specs.append(pl.BlockSpec((1, d), lambda i, j: (0, 0)))
        args.append(final_g.reshape(1, d))
    return pl.pallas_call(
        functools.partial(_out_kernel, odd=len(branches) > 2, final=final_g is not None), grid=(b, s // ts),
        in_specs=in_specs, out_specs=pl.BlockSpec((None, ts, d), lambda i, j: (i, j, 0)),
        out_shape=jax.ShapeDtypeStruct((b, s, d), F32),
        compiler_params=_cparams(("parallel", "parallel")), name="gated_out_proj",
    )(*args)


def _out_in_projection(x, mod_l, gate, w_out, branches, nxt):
    b, s, d = x.shape
    ts = min(TOKEN_TILE, s)
    mod_n, g_n, rope, w, wt, segs, tsegs, deltanet = nxt
    out_specs_in, out_args = _out_operands(x, mod_l, gate, w_out, branches, ts)
    in_specs, args, out_specs, out_shapes, scratch, sem = _projection_operands(
        b, s, d, ts, mod_n, g_n, rope, w, wt, segs, tsegs, deltanet)
    x_spec = pl.BlockSpec((None, ts, d), lambda i, j: (i, j, 0))
    return pl.pallas_call(
        functools.partial(_out_in_kernel, odd=len(branches) > 2, n_out_inputs=len(out_args), segs=segs, tsegs=tsegs,
                          deltanet=deltanet is not None),
        grid=(b, s // ts), in_specs=out_specs_in + in_specs,
        out_specs=[x_spec] + out_specs, out_shape=[jax.ShapeDtypeStruct((b, s, d), F32)] + out_shapes,
        scratch_shapes=scratch, compiler_params=_cparams(sem), name="out_proj_in_proj",
    )(*out_args, *args)


def _pair_cols(a0, b0):
    a, bb = np.arange(a0, a0 + HEAD_DIM), np.arange(b0, b0 + HEAD_DIM)
    return np.concatenate([a[:HALF], bb[:HALF], a[HALF:], bb[HALF:]])


def _paired_head_order(width):
    pairs = [np.concatenate([np.arange(p * HEAD_DIM, (p + 1) * HEAD_DIM),
                             np.arange((p + C_HPG) * HEAD_DIM, (p + C_HPG + 1) * HEAD_DIM)]) for p in range(C_HPG)]
    return np.concatenate(pairs + [np.arange(C_HEADS * HEAD_DIM, width)])


def _even_layout():
    aq, ak, av = 0, 512, 1024
    bq, bk, bv, bf, gate = 1536, 2048, 2560, 3072, 3080
    zero = gate + 1024
    cols = [_pair_cols(ak + 2 * h * HEAD_DIM, ak + (2 * h + 1) * HEAD_DIM) for h in range(A_HEADS)]
    cols.append(np.arange(bk, bk + 512))
    cols.append(np.concatenate([np.repeat(np.arange(bf, bf + B_HEADS), BIAS_PIECES),
                                np.full(LANES - B_HEADS * BIAS_PIECES, zero)]))
    cols.append(np.arange(gate, gate + 1024))
    segs = (("rope", 0, 512), ("bf16", 512, 1024), ("f32", 1024, 1152), ("bf16", 1152, 2176))
    rows = [_pair_cols(aq + 2 * h * HEAD_DIM, aq + (2 * h + 1) * HEAD_DIM) for h in range(A_HEADS)]
    rows += [np.arange(bq, bq + 512), np.arange(av, av + 512), np.arange(bv, bv + 512)]
    tsegs = (("rope", 0, 512), ("bf16", 512, 2048))
    return np.concatenate(cols), segs, np.concatenate(rows), tsegs


def _odd_layout():
    cq, kc, vc, ks, vs, kw, vw, cg = 0, 512, 640, 768, 896, 1024, 1152, 1280
    dq, da, db, gate = 1304, 2840, 2844, 2848
    zero = gate + 1024
    small = np.concatenate([np.arange(da, da + 2 * D_HEADS), np.arange(cg, cg + C_HEADS * N_BRANCH)])
    cols = [np.arange(dq, dq + 1536), np.concatenate([small, np.full(LANES - small.size, zero)])]
    cols += [_pair_cols(ks, ks + HEAD_DIM), _pair_cols(kw, kw + HEAD_DIM), np.arange(kc, kc + 256)]
    cols.append(gate + _paired_head_order(1024))
    segs = (("dn_q", 0, 512), ("dn_k", 512, 1024), ("dn_v", 1024, 1536), ("dn_gates", 1536, 1664),
            ("rope", 1664, 1920), ("f32", 1920, 2176), ("bf16", 2176, 3200))
    rows = [_pair_cols(cq + p * HEAD_DIM, cq + (p + C_HPG) * HEAD_DIM) for p in range(C_HPG)]
    rows += [np.arange(vs, vs + LANES), np.arange(vw, vw + LANES)]
    tsegs = (("rope+raw", 0, 512), ("bf16", 512, 768))
    return np.concatenate(cols), segs, np.concatenate(rows), tsegs


def _layout_weights(w, cols, rows, n_query_rows):
    w = jnp.concatenate([w, jnp.zeros((w.shape[0], 1), w.dtype)], axis=1)
    scale = jnp.where(jnp.arange(rows.size) < n_query_rows, QK_SCALE, 1.0).astype(w.dtype)
    return w[:, cols].astype(BF16), (w[:, rows] * scale).T.astype(BF16)


def _compress_weights(pe, w1, w2, for_keys):
    half = L_CMP // 2 * HEAD_DIM
    w1ab = jnp.concatenate([w1[:half], w1[half:]], axis=1)
    w1ab = w1ab.reshape(CMP_STRIDE, HEAD_DIM, 2 * CMP_HIDDEN)
    zeros = jnp.zeros_like(w1ab)
    w1_by_tok = jnp.concatenate([jnp.concatenate([w1ab, zeros], axis=2),
                                 jnp.concatenate([zeros, w1ab], axis=2)], axis=1).astype(BF16)
    w2p = jnp.zeros((C_GROUPS, CMP_HIDDEN, LANES), F32)
    for g in range(C_GROUPS):
        if for_keys:
            w2p = w2p.at[g, :, g * HALF:(g + 1) * HALF].set(w2[:, :HALF])
            w2p = w2p.at[g, :, HEAD_DIM + g * HALF:HEAD_DIM + (g + 1) * HALF].set(w2[:, HALF:])
        else:
            w2p = w2p.at[g, :, g * HEAD_DIM:(g + 1) * HEAD_DIM].set(w2)
    if not for_keys:
        w2p = w2p.transpose(0, 2, 1)
    return [w1_by_tok, w1.astype(F32), pe.astype(F32).reshape(1, L_CMP * HEAD_DIM), w2p.astype(BF16)]


def _even_projection_spec(w_in):
    cols, segs, rows, tsegs = _even_layout()
    w, wt = _layout_weights(w_in, cols, rows, 2 * A_HEADS * HEAD_DIM + B_HEADS * HEAD_DIM)
    return w, wt, segs, tsegs, None


def _odd_projection_spec(w_in, conv_w, a_log, dt_bias):
    cols, segs, rows, tsegs = _odd_layout()
    w, wt = _layout_weights(w_in, cols, rows, C_HEADS * HEAD_DIM)
    gate_params = jnp.zeros((2, LANES), F32).at[:, :D_HEADS].set(jnp.stack([a_log, dt_bias]).astype(F32))
    return w, wt, segs, tsegs, (conv_w.astype(F32), gate_params)


def _even_mixers(projected, layer_idx, b_forget, lq1, lk1, lq2, lk2, subln_g, w_out):
    k_a, k_b, forget_logits, gate, qt_a, rest_t = projected
    lam_init = 0.8 - 0.6 * math.exp(-0.3 * layer_idx)
    vec = lambda a: a.astype(F32).reshape(1, -1)
    const = lambda shape: (lambda tq, tk, tile: pl.BlockSpec(shape, lambda i, hb, *t: (0, 0)))
    nb = B_HEADS // 2
    per_step = 4
    oa = _flash("diff", qt_a, k_a, rest_t, A_HEADS, per_step, 0, 0, nb, False,
                extra=[vec(lq1), vec(lk1), vec(lq2), vec(lk2), subln_g.astype(F32).reshape(LANES, 1)],
                extra_specs=[const((1, HEAD_DIM))] * 4 + [const((LANES, 1))], lam_init=lam_init)
    bias = _forget_cumsum(forget_logits, b_forget)
    bias_spec = lambda tq, tk, tile: pl.BlockSpec((None, tk, LANES), lambda i, hb, *t: (i, tile(*t)[1], 0))
    ob = _flash("fox", rest_t, k_b, rest_t, nb, per_step, 0, 0, nb + A_HEADS, False,
                extra=[bias], extra_specs=[bias_spec])
    return gate, [oa, ob], w_out.astype(BF16)


def _odd_mixers(projected, pe_k, pe_v, w1_k, w2_k, w1_v, w2_v, dn_norm_g, w_out):
    dn_q, dn_k, dn_v, small, k_rot, cmp_in, gate, q_raw_t, q_rot_t, v_t = projected
    b, s, _ = small.shape
    n_cmp = (s - L_CMP) // CMP_STRIDE + 1
    kcmp, vcmp_t = _compress(cmp_in, _compress_weights(pe_k, w1_k, w2_k, True),
                             _compress_weights(pe_v, w1_v, w2_v, False))
    o_cmp, sel = _cmp_select(q_raw_t, kcmp, vcmp_t, n_cmp)
    sel_spec = lambda tq, tk, tile: pl.BlockSpec((None, LANES, tq), lambda i, hb, *t: (i, 0, tile(*t)[0]))
    o_slc = _flash("sel", q_rot_t, k_rot, v_t, C_HPG, C_HPG, 0, 0, 0, True, extra=[sel], extra_specs=[sel_spec])
    o_win = _flash("win", q_rot_t, k_rot, v_t, C_HPG, C_HPG, 0, 1, 1, True)
    gates = small[:, :, :2 * D_HEADS].reshape(b, s, 2, D_HEADS).transpose(0, 3, 1, 2)
    od = _gated_deltanet(dn_q, dn_k, dn_v, gates, dn_norm_g)
    rows = _paired_head_order(w_out.shape[0])
    return gate, [o_cmp, o_slc, o_win, od, small], w_out[rows].astype(BF16)


def kernel(x, c, positions, norm_g, w_mod, b_mod, w_out, final_norm_g, w_in_even, b_forget, lambda_q1, lambda_k1,
           lambda_q2, lambda_k2, subln_g, w_in_odd, cmp_pe_k, cmp_pe_v, cmp_w1_k, cmp_w2_k, cmp_w1_v, cmp_w2_v,
           conv_w, a_log, dt_bias, dn_norm_g):
    depth = norm_g.shape[0]
    rope = _rope_tables(positions)
    mod = _modulation(c, w_mod, b_mod)
    specs = [_even_projection_spec(w_in_even[l // 2]) if l % 2 == 0 else
             _odd_projection_spec(w_in_odd[l // 2], conv_w[l // 2], a_log[l // 2], dt_bias[l // 2])
             for l in range(depth)]
    projected = _projection(x, mod[0], norm_g[0], rope, *specs[0])
    for l in range(depth):
        i = l // 2
        if l % 2 == 0:
            gate, branches, w_o = _even_mixers(projected, l, b_forget[i], lambda_q1[i], lambda_k1[i], lambda_q2[i],
                                               lambda_k2[i], subln_g[i], w_out[l])
        else:
            gate, branches, w_o = _odd_mixers(projected, cmp_pe_k[i], cmp_pe_v[i], cmp_w1_k[i], cmp_w2_k[i],
                                              cmp_w1_v[i], cmp_w2_v[i], dn_norm_g[i], w_out[l])
        if l + 1 < depth:
            x, *projected = _out_in_projection(x, mod[l], gate, w_o, branches,
                                               (mod[l + 1], norm_g[l + 1], rope) + specs[l + 1])
        else:
            x = _out_projection(x, mod[l], gate, w_o, branches, final_norm_g)
    return x
```

```python
import functools
import math

import jax
import jax.numpy as jnp
import numpy as np
from jax import lax
from jax.experimental import pallas as pl
from jax.experimental.pallas import tpu as pltpu

F32 = jnp.float32
BF16 = jnp.bfloat16
HI = lax.Precision.HIGHEST

LANES = 128
HEAD_DIM = 64
HALF = HEAD_DIM // 2
ROPE_THETA = 10000.0
NORM_EPS = 1e-6
NEG = -1e30
MASK_BIG = 1e30
LOG2E = math.log2(math.e)
QK_SCALE = HEAD_DIM ** -0.5 * LOG2E
BIAS_PIECES = 3
NORM_ROWS = 16
A_HEADS = 4
B_HEADS = 8
C_HEADS = 8
C_GROUPS = 2
C_HPG = C_HEADS // C_GROUPS
L_CMP = 32
CMP_STRIDE = 16
CMP_HIDDEN = 256
L_SEL = 64
N_SEL = 8
WINDOW = 512
N_BRANCH = 3
FORCE_BONUS = 1e4
D_HEADS = 4
D_DK = 128
D_DV = 128
CONV_WIDTH = 4
DN_CHUNK = 64
TOKEN_TILE = 512
VMEM_LIMIT = 56 * 1024 * 1024


def _cparams(sem):
    return pltpu.CompilerParams(dimension_semantics=sem, vmem_limit_bytes=VMEM_LIMIT)


def _dot(a, b, precision=None):
    return jnp.dot(a, b, precision=precision, preferred_element_type=F32)


def _dot_nt(a, b, precision=None):
    return lax.dot_general(a, b, (((1,), (1,)), ((), ())), precision=precision, preferred_element_type=F32)


def _split_bf16(a):
    hi = a.astype(BF16)
    return hi, (a - hi.astype(F32)).astype(BF16)


def _softplus(z):
    return jnp.maximum(z, 0.0) + jnp.log1p(jnp.exp(-jnp.abs(z)))


def _silu(z):
    return z * jax.nn.sigmoid(z)


def _rope_table_kernel(pos_ref, inv_ref, cos_ref, sin_ref, cos_t_ref, sin_t_ref):
    ang = pos_ref[...].astype(F32) * inv_ref[...]
    lane = lax.broadcasted_iota(jnp.int32, (1, LANES), 1)
    cos = jnp.cos(ang)
    sin = jnp.where(lane < 2 * HALF, -1.0, 1.0) * jnp.sin(ang)
    cos_ref[...] = cos
    sin_ref[...] = sin
    cos_t_ref[...] = cos.T
    sin_t_ref[...] = sin.T


def _rope_tables(positions):
    b, s = positions.shape
    inv = ROPE_THETA ** (-jnp.arange(0, HEAD_DIM, 2, dtype=F32) / HEAD_DIM)
    inv = jnp.tile(inv, 4).reshape(1, LANES)
    tok = jax.ShapeDtypeStruct((b, s, LANES), F32)
    feat = jax.ShapeDtypeStruct((b, LANES, s), F32)
    return pl.pallas_call(
        _rope_table_kernel, grid=(b,),
        in_specs=[pl.BlockSpec((None, s, 1), lambda i: (i, 0, 0)),
                  pl.BlockSpec((1, LANES), lambda i: (0, 0))],
        out_specs=[pl.BlockSpec((None, s, LANES), lambda i: (i, 0, 0))] * 2
        + [pl.BlockSpec((None, LANES, s), lambda i: (i, 0, 0))] * 2,
        out_shape=[tok, tok, feat, feat], compiler_params=_cparams(("parallel",)), name="rope_tables",
    )(positions.reshape(b, s, 1), inv)


def _mod_kernel(c_ref, w_ref, b_ref, o_ref):
    o_ref[...] = _dot(_silu(c_ref[...]), w_ref[...], HI) + b_ref[...]


def _modulation(c, w_mod, b_mod):
    depth, d, n = w_mod.shape
    b = c.shape[0]
    tn = 1024
    mod = pl.pallas_call(
        _mod_kernel, grid=(depth, n // tn),
        in_specs=[pl.BlockSpec((b, d), lambda l, j: (0, 0)),
                  pl.BlockSpec((None, d, tn), lambda l, j: (l, 0, j)),
                  pl.BlockSpec((None, 1, tn), lambda l, j: (l, 0, j))],
        out_specs=pl.BlockSpec((None, b, tn), lambda l, j: (l, 0, j)),
        out_shape=jax.ShapeDtypeStruct((depth, b, n), F32),
        compiler_params=_cparams(("parallel", "parallel")), name="modulation",
    )(c, w_mod, b_mod.reshape(depth, 1, n))
    return mod.reshape(depth, b, 3, 1, d)


def _proj_kernel(x_ref, *refs, segs, tsegs, deltanet):
    _proj_body(x_ref[...], *refs, segs=segs, tsegs=tsegs, deltanet=deltanet)


def _proj_body(x, shift_ref, scale_ref, g_ref, cos_ref, sin_ref, cos_t_ref, sin_t_ref, w_ref, wt_ref,
               *rest, segs, tsegs, deltanet):
    h = x * lax.rsqrt(jnp.mean(x * x, axis=-1, keepdims=True) + NORM_EPS) * g_ref[...]
    h = (h * (1.0 + scale_ref[...]) + shift_ref[...]).astype(BF16)
    if deltanet:
        conv_ref, gparam_ref, halo_ref = rest[0], rest[1], rest[-1]
        outs = list(rest[2:-1])

        @pl.when(pl.program_id(1) == 0)
        def _():
            halo_ref[...] = jnp.zeros(halo_ref.shape, F32)
    else:
        outs = list(rest)
    ts = x.shape[0]
    step = 4 * LANES
    conv_off = 0
    work = []

    def tok_item(kind, c0, a, e, o_ref, cols):
        def epilogue(acc):
            if kind == "rope":
                cos, sin = cos_ref[...], sin_ref[...]
                for j in range(0, e - a, LANES):
                    blk = acc[:, j:j + LANES]
                    rot = blk * cos + pltpu.roll(blk, 2 * HALF, axis=1) * sin
                    o_ref[:, a - c0 + j:a - c0 + j + LANES] = rot.astype(o_ref.dtype)
            elif kind == "dn_gates":
                lane = lax.broadcasted_iota(jnp.int32, (1, LANES), 1)
                log_decay = -jnp.exp(gparam_ref[0:1, :]) * _softplus(acc + gparam_ref[1:2, :])
                o_ref[...] = jnp.where(lane < D_HEADS, log_decay,
                                       jnp.where(lane < 2 * D_HEADS, jax.nn.sigmoid(acc), acc))
            elif kind in ("dn_q", "dn_k", "dn_v"):
                wc = conv_ref[:, cols]
                ext = jnp.concatenate([halo_ref[:, cols], acc], axis=0)
                halo_ref[:, cols] = acc[ts - 8:, :]
                y = acc * wc[CONV_WIDTH - 1:CONV_WIDTH, :]
                for back in range(1, CONV_WIDTH):
                    y = y + ext[8 - back:8 - back + ts, :] * wc[CONV_WIDTH - 1 - back:CONV_WIDTH - back, :]
                y = _silu(y)
                for j in range(0, e - a, LANES):
                    blk = y[:, j:j + LANES]
                    if kind != "dn_v":
                        blk = blk * lax.rsqrt(jnp.sum(blk * blk, axis=-1, keepdims=True) + NORM_EPS)
                    if kind == "dn_q":
                        blk = blk * (D_DK ** -0.5)
                    o_ref[:, a - c0 + j:a - c0 + j + LANES] = blk
            else:
                o_ref[:, a - c0:e - c0] = acc.astype(o_ref.dtype)
        return (lambda: _dot(h, w_ref[:, a:e])), epilogue

    def feat_item(kind, r0, a, e, o_ref, raw_ref):
        def epilogue(acc):
            if raw_ref is not None:
                raw_ref[a - r0:e - r0, :] = acc
            if kind in ("rope", "rope+raw"):
                cos, sin = cos_t_ref[...], sin_t_ref[...]
                for j in range(0, e - a, LANES):
                    blk = acc[j:j + LANES, :]
                    rot = blk * cos + pltpu.roll(blk, 2 * HALF, axis=0) * sin
                    o_ref[a - r0 + j:a - r0 + j + LANES, :] = rot.astype(o_ref.dtype)
            else:
                o_ref[a - r0:e - r0, :] = acc.astype(o_ref.dtype)
        return (lambda: _dot_nt(wt_ref[a:e, :], h)), epilogue

    for kind, c0, c1 in segs:
        o_ref = outs.pop(0)
        for a in range(c0, c1, step):
            e = min(a + step, c1)
            cols = None
            if kind in ("dn_q", "dn_k", "dn_v"):
                cols = slice(conv_off, conv_off + (e - a))
                conv_off += e - a
            work.append(tok_item(kind, c0, a, e, o_ref, cols))
    for kind, r0, r1 in tsegs:
        raw_ref = outs.pop(0) if kind == "rope+raw" else None
        o_ref = outs.pop(0)
        for a in range(r0, r1, step):
            work.append(feat_item(kind, r0, a, min(a + step, r1), o_ref, raw_ref))
    acc = work[0][0]()
    for i, (_, epilogue) in enumerate(work):
        nxt = work[i + 1][0]() if i + 1 < len(work) else None
        epilogue(acc)
        acc = nxt


def _projection(x, mod_l, g, rope, w, wt, segs, tsegs, deltanet=None):
    b, s, d = x.shape
    ts = min(TOKEN_TILE, s)
    in_specs, args, out_specs, out_shapes, scratch, sem = _projection_operands(
        b, s, d, ts, mod_l, g, rope, w, wt, segs, tsegs, deltanet)
    return pl.pallas_call(
        functools.partial(_proj_kernel, segs=segs, tsegs=tsegs, deltanet=deltanet is not None), grid=(b, s // ts),
        in_specs=[pl.BlockSpec((None, ts, d), lambda i, j: (i, j, 0))] + in_specs,
        out_specs=out_specs, out_shape=out_shapes, scratch_shapes=scratch,
        compiler_params=_cparams(sem), name="adaln_in_proj",
    )(x, *args)


def _projection_operands(b, s, d, ts, mod_l, g, rope, w, wt, segs, tsegs, deltanet):
    row = lambda width: pl.BlockSpec((None, ts, width), lambda i, j: (i, j, 0))
    col = lambda height: pl.BlockSpec((None, height, ts), lambda i, j: (i, 0, j))
    out_specs, out_shapes = [], []
    for kind, c0, c1 in segs:
        out_specs.append(row(c1 - c0))
        out_shapes.append(jax.ShapeDtypeStruct((b, s, c1 - c0), BF16 if kind in ("rope", "bf16") else F32))
    for kind, r0, r1 in tsegs:
        if kind == "rope+raw":
            out_specs.append(col(r1 - r0))
            out_shapes.append(jax.ShapeDtypeStruct((b, r1 - r0, s), F32))
        out_specs.append(col(r1 - r0))
        out_shapes.append(jax.ShapeDtypeStruct((b, r1 - r0, s), BF16))
    modspec = lambda k: pl.BlockSpec((None, None, 1, d), lambda i, j: (i, k, 0, 0))
    resident = lambda a: pl.BlockSpec(a.shape, lambda i, j: (0, 0), pipeline_mode=pl.Buffered(1))
    in_specs = [modspec(0), modspec(1), pl.BlockSpec((1, d), lambda i, j: (0, 0)),
                row(LANES), row(LANES), col(LANES), col(LANES), resident(w), resident(wt)]
    args = [mod_l, mod_l, g.reshape(1, d), *rope, w, wt]
    scratch = []
    if deltanet is not None:
        in_specs += [pl.BlockSpec(a.shape, lambda i, j: (0, 0)) for a in deltanet]
        args += list(deltanet)
        scratch = [pltpu.VMEM((8, deltanet[0].shape[1]), F32)]
    sem = ("parallel", "arbitrary") if deltanet is not None else ("parallel", "parallel")
    return in_specs, args, out_specs, out_shapes, scratch, sem


def _cum_kernel(x_ref, bias_ref, o_ref):
    s = x_ref.shape[0]
    ii = lax.broadcasted_iota(jnp.int32, (LANES, LANES), 0)
    jj = lax.broadcasted_iota(jnp.int32, (LANES, LANES), 1)
    lower = (ii >= jj).astype(F32)
    lane = lax.broadcasted_iota(jnp.int32, (1, LANES), 1)
    carry = jnp.zeros((1, LANES), F32)
    for r0 in range(0, s, LANES):
        z = x_ref[r0:r0 + LANES, :] + bias_ref[...]
        logf = jnp.minimum(z, 0.0) - jnp.log1p(jnp.exp(-jnp.abs(z)))
        loc = _dot(lower, logf, HI) + carry
        carry = loc[LANES - 1:LANES, :]
        val = loc * (-LOG2E)
        hi = val.astype(BF16)
        rest = val - hi.astype(F32)
        mid = rest.astype(BF16)
        lo = (rest - mid.astype(F32)).astype(BF16)
        piece = jnp.where(lane % BIAS_PIECES == 0, hi, jnp.where(lane % BIAS_PIECES == 1, mid, lo))
        o_ref[r0:r0 + LANES, :] = jnp.where(lane < B_HEADS * BIAS_PIECES, piece, jnp.zeros_like(piece))


def _forget_cumsum(logits, b_forget):
    b, s, _ = logits.shape
    bias = jnp.zeros((1, LANES), F32).at[0, :B_HEADS * BIAS_PIECES].set(jnp.repeat(b_forget.astype(F32), BIAS_PIECES))
    return pl.pallas_call(
        _cum_kernel, grid=(b,),
        in_specs=[pl.BlockSpec((None, s, LANES), lambda i: (i, 0, 0)), pl.BlockSpec((1, LANES), lambda i: (0, 0))],
        out_specs=pl.BlockSpec((None, s, LANES), lambda i: (i, 0, 0)),
        out_shape=jax.ShapeDtypeStruct((b, s, LANES), BF16),
        compiler_params=_cparams(("parallel",)), name="forget_cumsum",
    )(logits, bias)


def _triangle_step(t, nq):
    qi = sum((t >= j * (j + 1) // 2).astype(jnp.int32) for j in range(1, nq))
    return qi, t - qi * (qi + 1) // 2


def _flash_kernel(*refs, mode, tq, tk, nq, nblk, shared_kv, lam_init):
    qt_ref, k_ref, vt_ref = refs[:3]
    m_ref, acc_ref = refs[-2:]
    o_ref = refs[-3]
    extra = refs[3:-3]
    if mode == "win":
        qi, ki = pl.program_id(2), pl.program_id(3)
        kv, last = qi - 1 + ki, ki == pl.num_programs(3) - 1
    else:
        qi, ki = _triangle_step(pl.program_id(2), nq)
        kv, last = ki, ki == qi
    row = lax.broadcasted_iota(jnp.int32, (LANES, 1), 0)
    if mode == "fox":
        slot_rows = (row < HEAD_DIM, row >= HEAD_DIM)
    else:
        slot_rows = ((row % HEAD_DIM) < HALF, (row % HEAD_DIM) >= HALF)
    vrows = acc_ref.shape[1] - NORM_ROWS
    block = lambda j: slice(j * LANES, (j + 1) * LANES)
    half = tq // 2

    @pl.when(ki == 0)
    def _():
        m_ref[...] = jnp.full(m_ref.shape, NEG, F32)
        acc_ref[...] = jnp.zeros(acc_ref.shape, F32)

    def step(kind):
        rr = lax.broadcasted_iota(jnp.int32, (LANES, tq), 0)
        k_extra = [None, None]
        for s in range(2):
            if mode == "sel":
                cb = (kv * tk + lax.broadcasted_iota(jnp.int32, (tk, LANES), 0)) // L_SEL
                ll = lax.broadcasted_iota(jnp.int32, (tk, LANES), 1)
                k_extra[s] = jnp.where(ll == cb + s * (LANES // 4), MASK_BIG, 0.0).astype(BF16)

        def q_extra(j, s):
            if mode == "sel":
                return extra[0][...]
            if mode == "fox":
                first = ((pl.program_id(1) * nblk + j) * 2 + s) * BIAS_PIECES
                return jnp.where((rr >= first) & (rr < first + BIAS_PIECES), 1.0, 0.0).astype(BF16)
            return None
        lo, hi, everything = slice(0, half), slice(half, tq), slice(0, tq)
        parts = {"full": [(everything, everything, None)],
                 "lower": [(lo, lo, lo), (everything, hi, hi)],
                 "upper": [(everything, lo, lo), (hi, hi, hi)]}[kind]
        tri_k = lax.broadcasted_iota(jnp.int32, (half, half), 0)
        tri_q = lax.broadcasted_iota(jnp.int32, (half, half), 1)
        tri = tri_k <= tri_q if kind == "lower" else tri_k > tri_q
        ones = jnp.ones((NORM_ROWS, tk), BF16)
        sts, vts = [], []
        for j in range(nblk):
            qt = qt_ref[block(j), :]
            k = k_ref[...] if shared_kv else k_ref[:, block(j)]
            base = 0 if shared_kv else j * LANES
            for s in range(2):
                qs = jnp.where(slot_rows[s], qt, jnp.zeros_like(qt))
                q_more = q_extra(j, s)
                if q_more is not None:
                    k_more = extra[0][...] if mode == "fox" else k_extra[s]
                    k_all, q_all = jnp.concatenate([k, k_more], axis=1), jnp.concatenate([qs, q_more], axis=0)
                else:
                    k_all, q_all = k, qs
                sts.append([_dot(k_all[krows], q_all[:, lanes]) for krows, lanes, _ in parts])
                v0 = base if vrows == LANES else base + s * vrows
                vts.append(jnp.concatenate([vt_ref[v0:v0 + vrows, :], ones], axis=0))
        for idx in range(2 * nblk):
            for (krows, lanes, tri_rows), st in zip(parts, sts[idx]):
                if tri_rows is not None:
                    if krows == tri_rows:
                        st = jnp.where(tri, st, NEG)
                    elif tri_rows == lo:
                        st = jnp.concatenate([jnp.where(tri, st[:half], NEG), st[half:]], axis=0)
                    else:
                        st = jnp.concatenate([st[:half], jnp.where(tri, st[half:], NEG)], axis=0)
                m_prev = m_ref[idx, :, lanes]
                m_new = jnp.maximum(m_prev, jnp.max(st, axis=0, keepdims=True))
                alpha = jnp.exp2(m_prev - m_new)
                p = jnp.exp2(st - m_new).astype(BF16)
                acc_ref[idx, :, lanes] = alpha * acc_ref[idx, :, lanes] + _dot(vts[idx][:, krows], p)
                m_ref[idx, :, lanes] = m_new

    if mode == "win":
        pl.when((ki == 0) & (kv >= 0))(functools.partial(step, "upper"))
        pl.when(ki == 1)(functools.partial(step, "lower"))
    else:
        pl.when(ki < qi)(functools.partial(step, "full"))
        pl.when(ki == qi)(functools.partial(step, "lower"))

    @pl.when(last)
    def _():
        for j in range(nblk):
            o0 = acc_ref[2 * j, :vrows, :] / acc_ref[2 * j, vrows:vrows + 1, :]
            o1 = acc_ref[2 * j + 1, :vrows, :] / acc_ref[2 * j + 1, vrows:vrows + 1, :]
            if mode == "diff":
                lq1, lk1, lq2, lk2, subg = (r[...] for r in extra)
                lam = (jnp.exp(jnp.sum(lq1 * lk1, axis=1, keepdims=True))
                       - jnp.exp(jnp.sum(lq2 * lk2, axis=1, keepdims=True)) + lam_init)
                o = o0 - lam * o1
                o = o * lax.rsqrt(jnp.mean(o * o, axis=0, keepdims=True) + NORM_EPS) * subg * (1.0 - lam_init)
            else:
                o = jnp.concatenate([o0, o1], axis=0)
            o_ref[:, block(j)] = o.T.astype(o_ref.dtype)


def _flash(mode, qt, k, vt, n_qblocks, nblk, q0, k0, v0, shared_kv, extra=(), extra_specs=(), lam_init=0.0):
    b, s, _ = k.shape
    tq = tk = min(TOKEN_TILE, s)
    nq = s // tq
    if mode == "win":
        assert WINDOW == tk
        steps, sem = (nq, 2), ("parallel", "arbitrary")
        tile = lambda qi, ki: (qi, jnp.maximum(qi - 1 + ki, 0))
    else:
        steps, sem = (nq * (nq + 1) // 2,), ("arbitrary",)
        tile = lambda t: _triangle_step(t, nq)
    assert n_qblocks % nblk == 0 and q0 % nblk == 0 and (shared_kv or (k0 % nblk == 0 and v0 % nblk == 0))
    wide = nblk * LANES
    if shared_kv:
        k_spec = pl.BlockSpec((None, tk, LANES), lambda i, hb, *t: (i, tile(*t)[1], k0))
        v_spec = pl.BlockSpec((None, LANES, tk), lambda i, hb, *t: (i, v0, tile(*t)[1]))
    else:
        k_spec = pl.BlockSpec((None, tk, wide), lambda i, hb, *t: (i, tile(*t)[1], k0 // nblk + hb))
        v_spec = pl.BlockSpec((None, wide, tk), lambda i, hb, *t: (i, v0 // nblk + hb, tile(*t)[1]))
    in_specs = [pl.BlockSpec((None, wide, tq), lambda i, hb, *t: (i, q0 // nblk + hb, tile(*t)[0])), k_spec, v_spec]
    for spec in extra_specs:
        in_specs.append(spec(tq, tk, tile))
    vrows = LANES if mode == "diff" else HEAD_DIM
    return pl.pallas_call(
        functools.partial(_flash_kernel, mode=mode, tq=tq, tk=tk, nq=nq, nblk=nblk, shared_kv=shared_kv,
                          lam_init=lam_init),
        grid=(b, n_qblocks // nblk) + steps, in_specs=in_specs,
        out_specs=pl.BlockSpec((None, tq, wide), lambda i, hb, *t: (i, tile(*t)[0], hb)),
        out_shape=jax.ShapeDtypeStruct((b, s, n_qblocks * LANES), BF16),
        scratch_shapes=[pltpu.VMEM((2 * nblk, 1, tq), F32), pltpu.VMEM((2 * nblk, vrows + NORM_ROWS, tq), F32)],
        compiler_params=_cparams(("parallel", "parallel") + sem),
        name="flash_" + mode,
    )(qt, k, vt, *extra)


def _compress_kernel(xk_ref, xv_ref, w1_by_tok_k, w1_k, pe_k, w2_k, w1_by_tok_v, w1_v, pe_v, w2_v, ok_ref, ov_ref,
                     pe_term_s):
    cn = xk_ref.shape[0] // CMP_STRIDE

    @pl.when(pl.program_id(0) == 0)
    def _():
        for idx, (pe, w1) in enumerate(((pe_k, w1_k), (pe_v, w1_v))):
            pe_term_s[idx] = jnp.broadcast_to(_dot(pe[...], w1[...], HI), (8, CMP_HIDDEN))

    def run(idx, x_ref, w1_by_tok, w2, o_ref, transposed):
        pe_term = pe_term_s[idx][0:1, :]
        ab = jnp.zeros((cn, C_GROUPS * 2 * CMP_HIDDEN), F32)
        for tok in range(CMP_STRIDE):
            rows = x_ref[pl.ds(tok, cn, stride=CMP_STRIDE), :]
            ab = ab + _dot(rows.astype(BF16), w1_by_tok[tok])
        out = 0.0
        for g in range(C_GROUPS):
            first = ab[:, g * 2 * CMP_HIDDEN:(g * 2 + 1) * CMP_HIDDEN]
            second = ab[:, (g * 2 + 1) * CMP_HIDDEN:(g + 1) * 2 * CMP_HIDDEN]
            hid = first + pltpu.roll(second, cn - 1, axis=0) + pe_term
            act = _silu(hid).astype(BF16)
            out = out + (_dot_nt(w2[g], act) if transposed else _dot(act, w2[g]))
        o_ref[...] = out
    run(0, xk_ref, w1_by_tok_k, w2_k, ok_ref, False)
    run(1, xv_ref, w1_by_tok_v, w2_v, ov_ref, True)


def _compress(cmp_in, weights_k, weights_v):
    b, s, width = cmp_in.shape
    cn = s // CMP_STRIDE
    full = lambda a: pl.BlockSpec(a.shape, lambda i: (0,) * a.ndim)
    return pl.pallas_call(
        _compress_kernel, grid=(b,),
        in_specs=[pl.BlockSpec((None, s, LANES), lambda i: (i, 0, 0)), pl.BlockSpec((None, s, LANES), lambda i: (i, 0, 1))]
        + [full(a) for a in weights_k + weights_v],
        out_specs=[pl.BlockSpec((None, cn, LANES), lambda i: (i, 0, 0)),
                   pl.BlockSpec((None, LANES, cn), lambda i: (i, 0, 0))],
        out_shape=[jax.ShapeDtypeStruct((b, cn, LANES), F32), jax.ShapeDtypeStruct((b, LANES, cn), F32)],
        scratch_shapes=[pltpu.VMEM((2, 8, CMP_HIDDEN), F32)],
        compiler_params=_cparams(("arbitrary",)), name="nsa_compress",
    )(cmp_in, cmp_in, *weights_k, *weights_v)


def _cmp_select_kernel(qt_ref, kc_ref, vct_ref, o_ref, sel_ref, *, tq, n_blk, n_cmp):
    qi = pl.program_id(1)
    cn = kc_ref.shape[0]
    kc_hi, kc_lo = _split_bf16(kc_ref[...])
    vct = vct_ref[...].astype(BF16)
    row = lax.broadcasted_iota(jnp.int32, (LANES, 1), 0)
    slot_rows = ((row % HEAD_DIM) < HALF, (row % HEAD_DIM) >= HALF)
    t_row = qi * tq + lax.broadcasted_iota(jnp.int32, (1, tq), 1)
    m_col = lax.broadcasted_iota(jnp.int32, (cn, 1), 0)
    valid = (m_col * CMP_STRIDE + L_CMP - 1 <= t_row) & (m_col < n_cmp)
    any_valid = (t_row >= L_CMP - 1).astype(F32)
    heads = [(p_blk, s) for p_blk in range(C_HPG) for s in range(C_GROUPS)]
    scores = []
    for p_blk, s in heads:
        q_hi, q_lo = _split_bf16(jnp.where(slot_rows[s], qt_ref[p_blk * LANES:(p_blk + 1) * LANES, :], 0.0))
        scores.append(_dot(kc_hi, q_hi) + _dot(kc_lo, q_hi) + _dot(kc_hi, q_lo))
    probs = []
    for sc in scores:
        sc = jnp.where(valid, sc, NEG)
        e = jnp.exp2(sc - jnp.max(sc, axis=0, keepdims=True))
        probs.append(e / jnp.sum(e, axis=0, keepdims=True) * any_valid)
    outs = [_dot(vct[s * HEAD_DIM:(s + 1) * HEAD_DIM, :], p.astype(BF16)) for (_, s), p in zip(heads, probs)]
    psum = [sum(p for (_, s), p in zip(heads, probs) if s == g) for g in range(C_GROUPS)]
    for p_blk in range(C_HPG):
        o_ref[:, p_blk * LANES:(p_blk + 1) * LANES] = jnp.concatenate(
            outs[C_GROUPS * p_blk:C_GROUPS * (p_blk + 1)], axis=0).T.astype(o_ref.dtype)
    jb = lax.broadcasted_iota(jnp.int32, (n_blk, cn), 0)
    mm = lax.broadcasted_iota(jnp.int32, (n_blk, cn), 1)
    overlap = ((mm * CMP_STRIDE < jb * L_SEL + L_SEL) & (mm * CMP_STRIDE + L_CMP > jb * L_SEL)
               & (mm < n_cmp)).astype(BF16)
    j = lax.broadcasted_iota(jnp.int32, (n_blk, tq), 0)
    cur = (qi * tq + lax.broadcasted_iota(jnp.int32, (n_blk, tq), 1)) // L_SEL
    forced = (j == 0) | (j == cur) | (j == cur - 1)
    n_top = min(N_SEL, n_blk)
    pad_rows = LANES // 4 - n_blk
    parts = []
    for s in range(C_GROUPS):
        p_hi, p_lo = _split_bf16(psum[s])
        imp = _dot(overlap, p_hi) + _dot(overlap, p_lo)
        score = jnp.where(j > cur, NEG, imp + jnp.where(forced, FORCE_BONUS, 0.0))
        rank = jnp.zeros((n_blk, tq), jnp.int32)
        for jp in range(n_blk):
            r = score[jp:jp + 1, :]
            rank = rank + ((r > score) | ((r == score) & (jp < j))).astype(jnp.int32)
        parts.append(jnp.where(rank < n_top, 0.0, -1.0))
        if pad_rows:
            parts.append(jnp.zeros((pad_rows, tq), F32))
    parts.append(jnp.zeros((LANES // 2, tq), F32))
    sel_ref[...] = jnp.concatenate(parts, axis=0).astype(sel_ref.dtype)


def _cmp_select(q_raw_t, kcmp, vcmp_t, n_cmp):
    b, width, s = q_raw_t.shape
    cn = kcmp.shape[1]
    tq = min(TOKEN_TILE, s)
    n_blk = s // L_SEL
    return pl.pallas_call(
        functools.partial(_cmp_select_kernel, tq=tq, n_blk=n_blk, n_cmp=n_cmp), grid=(b, s // tq),
        in_specs=[pl.BlockSpec((None, width, tq), lambda i, j: (i, 0, j)),
                  pl.BlockSpec((None, cn, LANES), lambda i, j: (i, 0, 0)),
                  pl.BlockSpec((None, LANES, cn), lambda i, j: (i, 0, 0))],
        out_specs=[pl.BlockSpec((None, tq, width), lambda i, j: (i, j, 0)),
                   pl.BlockSpec((None, LANES, tq), lambda i, j: (i, 0, j))],
        out_shape=[jax.ShapeDtypeStruct((b, s, width), BF16), jax.ShapeDtypeStruct((b, LANES, s), BF16)],
        compiler_params=_cparams(("parallel", "parallel")), name="nsa_cmp_select",
    )(q_raw_t, kcmp, vcmp_t)


GDN_HEADS = 2
GDN_BATCH = 2
GDN_UNROLL = 8


def _gdn_kernel(q_ref, k_ref, v_ref, gate_ref, ng_ref, o_ref, mc_s, n_s, gt_s):
    s_len = q_ref.shape[0]
    c = DN_CHUNK
    n_chunks = s_len // c
    lanes = [slice(j * LANES, (j + 1) * LANES) for j in range(GDN_HEADS)]

    sc = GDN_BATCH * c
    ii = lax.broadcasted_iota(jnp.int32, (sc, sc), 0)
    jj = lax.broadcasted_iota(jnp.int32, (sc, sc), 1)
    same = (ii // c) == (jj // c)
    causal, strict, upper, eye = same & (ii >= jj), same & (ii > jj), same & (ii <= jj), ii == jj
    chunk_end = same & (jj % c == c - 1)
    ng = ng_ref[...]

    def load(n, j):
        r0 = pl.multiple_of(n * sc, sc)
        return (q_ref[pl.ds(r0, sc), lanes[j]], k_ref[pl.ds(r0, sc), lanes[j]], v_ref[pl.ds(r0, sc), lanes[j]],
                gate_ref[j, pl.ds(r0, sc), 0:1], gate_ref[j, pl.ds(r0, sc), 1:2])

    def prepare(q, k, v, g, beta):
        g_row = jnp.sum(jnp.where(eye, g, 0.0), axis=0, keepdims=True)
        gc_col = jnp.sum(jnp.where(causal, g_row, 0.0), axis=1, keepdims=True)
        gc_row = jnp.sum(jnp.where(upper, g, 0.0), axis=0, keepdims=True)
        g_last = jnp.sum(jnp.where(chunk_end, gc_row, 0.0), axis=1, keepdims=True)
        decay = jnp.where(causal, jnp.exp(jnp.where(causal, gc_col - gc_row, 0.0)), 0.0)
        eg = jnp.exp(gc_col)
        kb = k * beta
        kbf = k.astype(BF16)
        raw = _dot_nt(kb.astype(BF16), kbf)
        raw_qk = _dot_nt(q.astype(BF16), kbf)
        rhs = jnp.concatenate([v * beta, kb * eg], axis=1).astype(BF16)
        kd = k * jnp.exp(g_last - gc_col)
        kd_t = [kd[t * c:(t + 1) * c].T.astype(BF16) for t in range(GDN_BATCH)]
        g_tot = [jnp.exp(g_last[t * c:t * c + 1]) for t in range(GDN_BATCH)]
        return raw, raw_qk, decay, rhs, kd_t, g_tot, q * eg

    def solve(prepared):
        n = len(prepared)
        decay = [p[2] for p in prepared]
        x = [-jnp.where(strict, prepared[i][0] * decay[i], 0.0) for i in range(n)]
        inv = [jnp.where(eye, 1.0, 0.0) + x[i] for i in range(n)]
        xb = [x[i].astype(BF16) for i in range(n)]
        x = [_dot(xb[i], xb[i]) for i in range(n)]
        for _ in range(int(math.log2(c)) - 2):
            xb = [x[i].astype(BF16) for i in range(n)]
            both = [_dot(jnp.concatenate([xb[i], inv[i].astype(BF16)], axis=0), xb[i]) for i in range(n)]
            x = [both[i][:sc] for i in range(n)]
            inv = [inv[i] + both[i][sc:] for i in range(n)]
        last = [_dot(inv[i].astype(BF16), x[i].astype(BF16)) for i in range(n)]
        inv = [inv[i] + last[i] for i in range(n)]
        sol = [_dot(inv[i].astype(BF16), prepared[i][3]).astype(BF16) for i in range(n)]
        qk = [(prepared[i][1] * decay[i]).astype(BF16) for i in range(n)]
        qo = [_dot(qk[i], sol[i]) for i in range(n)]
        mn = [[_dot(prepared[i][4][t], sol[i][t * c:(t + 1) * c]) for t in range(GDN_BATCH)] for i in range(n)]
        return [(qo[i], prepared[i][6], mn[i], prepared[i][5]) for i in range(n)]

    def store(n, j, qo, q_dec, mn, g_tot):
        r0 = pl.multiple_of(n * sc, sc)
        o_ref[pl.ds(r0, sc), lanes[j]] = qo[:, :D_DV]
        q_eff = (q_dec - qo[:, D_DV:]).astype(BF16)
        for t in range(GDN_BATCH):
            n_s[j, n * GDN_BATCH + t] = mn[t][:, :D_DV]
            mc_s[j, n * GDN_BATCH + t, :D_DK, :] = mn[t][:, D_DV:].astype(BF16)
            mc_s[j, n * GDN_BATCH + t, D_DK:, :] = q_eff[t * c:(t + 1) * c]
            gt_s[j, n * GDN_BATCH + t] = jnp.broadcast_to(g_tot[t], (8, LANES))

    def prepare_some(i, carry):
        items = [(i * GDN_UNROLL + u, j) for u in range(GDN_UNROLL) for j in range(GDN_HEADS)]
        results = solve([prepare(*operands) for operands in [load(n, j) for n, j in items]])
        for (n, j), res in zip(items, results):
            store(n, j, *res)
        return carry

    lax.fori_loop(0, n_chunks // (GDN_BATCH * GDN_UNROLL), prepare_some, 0)

    def advance(n, states):
        r0 = pl.multiple_of(n * c, c)
        operands = [(o_ref[pl.ds(r0, c), lanes[j]], gt_s[j, n], mc_s[j, n], n_s[j, n]) for j in range(GDN_HEADS)]
        prods = [_dot(operands[j][2], states[j].astype(BF16)) for j in range(GDN_HEADS)]
        out = [states[j] * operands[j][1][0:1, :] - prods[j][:D_DK] + operands[j][3] for j in range(GDN_HEADS)]
        for j in range(GDN_HEADS):
            o_ref[pl.ds(r0, c), lanes[j]] = operands[j][0] + prods[j][D_DK:]
        return tuple(out)

    lax.fori_loop(0, n_chunks, advance, tuple(jnp.zeros((D_DK, D_DV), F32) for _ in range(GDN_HEADS)))
    for j in range(GDN_HEADS):
        o = o_ref[:, lanes[j]]
        o_ref[:, lanes[j]] = o * lax.rsqrt(jnp.mean(o * o, axis=-1, keepdims=True) + NORM_EPS) * ng


def _gated_deltanet(q, k, v, gates, norm_g):
    b, s, _ = q.shape
    hp, width = GDN_HEADS, GDN_HEADS * LANES
    groups = D_HEADS // hp
    n_chunks = s // DN_CHUNK
    assert n_chunks % (GDN_BATCH * GDN_UNROLL) == 0
    col = pl.BlockSpec((None, s, width), lambda i, h: (i, 0, h))
    return pl.pallas_call(
        _gdn_kernel, grid=(b, groups),
        in_specs=[col, col, col, pl.BlockSpec((None, hp, s, 2), lambda i, h: (i, h, 0, 0)),
                  pl.BlockSpec((1, D_DV), lambda i, h: (0, 0))],
        out_specs=col,
        out_shape=jax.ShapeDtypeStruct((b, s, D_HEADS * D_DV), F32),
        scratch_shapes=[pltpu.VMEM((hp, n_chunks, D_DK + DN_CHUNK, D_DV), BF16),
                        pltpu.VMEM((hp, n_chunks, D_DK, D_DV), F32), pltpu.VMEM((hp, n_chunks, 8, LANES), F32)],
        compiler_params=_cparams(("parallel", "parallel")), name="gated_deltanet",
    )(q, k, v, gates, norm_g.astype(F32).reshape(1, D_DV))


def _out_kernel(*refs, odd, final):
    rest = list(refs[:-1])
    fin_ref = rest.pop() if final else None
    out = _out_body(*rest, odd=odd)
    if final:
        out = out * lax.rsqrt(jnp.mean(out * out, axis=-1, keepdims=True) + NORM_EPS) * fin_ref[...]
    refs[-1][...] = out


def _out_in_kernel(*refs, odd, n_out_inputs, segs, tsegs, deltanet):
    x_new = _out_body(*refs[:n_out_inputs], odd=odd)
    proj_refs = list(refs[n_out_inputs:])
    n_proj_inputs = 9 + (2 if deltanet else 0)
    x_out_ref = proj_refs.pop(n_proj_inputs)
    x_out_ref[...] = x_new
    _proj_body(x_new, *proj_refs, segs=segs, tsegs=tsegs, deltanet=deltanet)


def _out_body(x_ref, mg_ref, gate_ref, w_ref, *rest, odd):
    half = w_ref.shape[0] // 2
    sg = _silu(gate_ref[...].astype(F32))
    if odd:
        cmp_ref, slc_ref, win_ref, od_ref, small_ref = rest
        lane = lax.broadcasted_iota(jnp.int32, (1, LANES), 1)
        bg = jax.nn.sigmoid(small_ref[...])
        blocks = []
        for p_blk in range(C_HPG):
            sl = slice(p_blk * LANES, (p_blk + 1) * LANES)
            acc = 0.0
            for br, ref in enumerate((cmp_ref, slc_ref, win_ref)):
                ca = 2 * D_HEADS + p_blk * N_BRANCH + br
                cb = 2 * D_HEADS + (p_blk + C_HPG) * N_BRANCH + br
                acc = acc + jnp.where(lane < HEAD_DIM, bg[:, ca:ca + 1], bg[:, cb:cb + 1]) * ref[:, sl]
            blocks.append(acc)
        first = jnp.concatenate(blocks, axis=1)
        second = od_ref[...]
    else:
        first, second = rest[0][...], rest[1][...]
    y = (_dot((first * sg[:, :half]).astype(BF16), w_ref[:half, :])
         + _dot((second * sg[:, half:]).astype(BF16), w_ref[half:, :]))
    return x_ref[...] + mg_ref[...] * y


def _out_operands(x, mod_l, gate, w, branches, ts):
    d = x.shape[-1]
    row = lambda width: pl.BlockSpec((None, ts, width), lambda i, j: (i, j, 0))
    in_specs = [row(d), pl.BlockSpec((None, None, 1, d), lambda i, j: (i, 2, 0, 0)), row(gate.shape[-1]),
                pl.BlockSpec(w.shape, lambda i, j: (0, 0), pipeline_mode=pl.Buffered(1))]
    in_specs += [row(a.shape[-1]) for a in branches]
    return in_specs, [x, mod_l, gate, w, *branches]


def _out_projection(x, mod_l, gate, w, branches, final_g=None):
    b, s, d = x.shape
    ts = min(TOKEN_TILE, s)
    in_specs, args = _out_operands(x, mod_l, gate, w, branches, ts)
    if final_g is not None:
        in_specs.append(pl.BlockSpec((1, d), lambda i, j: (0, 0)))
        args.append(final_g.reshape(1, d))
    return pl.pallas_call(
        functools.partial(_out_kernel, odd=len(branches) > 2, final=final_g is not None), grid=(b, s // ts),
        in_specs=in_specs, out_specs=pl.BlockSpec((None, ts, d), lambda i, j: (i, j, 0)),
        out_shape=jax.ShapeDtypeStruct((b, s, d), F32),
        compiler_params=_cparams(("parallel", "parallel")), name="gated_out_proj",
    )(*args)


def _out_in_projection(x, mod_l, gate, w_out, branches, nxt):
    b, s, d = x.shape
    ts = min(TOKEN_TILE, s)
    mod_n, g_n, rope, w, wt, segs, tsegs, deltanet = nxt
    out_specs_in, out_args = _out_operands(x, mod_l, gate, w_out, branches, ts)
    in_specs, args, out_specs, out_shapes, scratch, sem = _projection_operands(
        b, s, d, ts, mod_n, g_n, rope, w, wt, segs, tsegs, deltanet)
    x_spec = pl.BlockSpec((None, ts, d), lambda i, j: (i, j, 0))
    return pl.pallas_call(
        functools.partial(_out_in_kernel, odd=len(branches) > 2, n_out_inputs=len(out_args), segs=segs, tsegs=tsegs,
                          deltanet=deltanet is not None),
        grid=(b, s // ts), in_specs=out_specs_in + in_specs,
        out_specs=[x_spec] + out_specs, out_shape=[jax.ShapeDtypeStruct((b, s, d), F32)] + out_shapes,
        scratch_shapes=scratch, compiler_params=_cparams(sem), name="out_proj_in_proj",
    )(*out_args, *args)


def _pair_cols(a0, b0):
    a, bb = np.arange(a0, a0 + HEAD_DIM), np.arange(b0, b0 + HEAD_DIM)
    return np.concatenate([a[:HALF], bb[:HALF], a[HALF:], bb[HALF:]])


def _paired_head_order(width):
    pairs = [np.concatenate([np.arange(p * HEAD_DIM, (p + 1) * HEAD_DIM),
                             np.arange((p + C_HPG) * HEAD_DIM, (p + C_HPG + 1) * HEAD_DIM)]) for p in range(C_HPG)]
    return np.concatenate(pairs + [np.arange(C_HEADS * HEAD_DIM, width)])


def _even_layout():
    aq, ak, av = 0, 512, 1024
    bq, bk, bv, bf, gate = 1536, 2048, 2560, 3072, 3080
    zero = gate + 1024
    cols = [_pair_cols(ak + 2 * h * HEAD_DIM, ak + (2 * h + 1) * HEAD_DIM) for h in range(A_HEADS)]
    cols.append(np.arange(bk, bk + 512))
    cols.append(np.concatenate([np.repeat(np.arange(bf, bf + B_HEADS), BIAS_PIECES),
                                np.full(LANES - B_HEADS * BIAS_PIECES, zero)]))
    cols.append(np.arange(gate, gate + 1024))
    segs = (("rope", 0, 512), ("bf16", 512, 1024), ("f32", 1024, 1152), ("bf16", 1152, 2176))
    rows = [_pair_cols(aq + 2 * h * HEAD_DIM, aq + (2 * h + 1) * HEAD_DIM) for h in range(A_HEADS)]
    rows += [np.arange(bq, bq + 512), np.arange(av, av + 512), np.arange(bv, bv + 512)]
    tsegs = (("rope", 0, 512), ("bf16", 512, 2048))
    return np.concatenate(cols), segs, np.concatenate(rows), tsegs


def _odd_layout():
    cq, kc, vc, ks, vs, kw, vw, cg = 0, 512, 640, 768, 896, 1024, 1152, 1280
    dq, da, db, gate = 1304, 2840, 2844, 2848
    zero = gate + 1024
    small = np.concatenate([np.arange(da, da + 2 * D_HEADS), np.arange(cg, cg + C_HEADS * N_BRANCH)])
    cols = [np.arange(dq, dq + 1536), np.concatenate([small, np.full(LANES - small.size, zero)])]
    cols += [_pair_cols(ks, ks + HEAD_DIM), _pair_cols(kw, kw + HEAD_DIM), np.arange(kc, kc + 256)]
    cols.append(gate + _paired_head_order(1024))
    segs = (("dn_q", 0, 512), ("dn_k", 512, 1024), ("dn_v", 1024, 1536), ("dn_gates", 1536, 1664),
            ("rope", 1664, 1920), ("f32", 1920, 2176), ("bf16", 2176, 3200))
    rows = [_pair_cols(cq + p * HEAD_DIM, cq + (p + C_HPG) * HEAD_DIM) for p in range(C_HPG)]
    rows += [np.arange(vs, vs + LANES), np.arange(vw, vw + LANES)]
    tsegs = (("rope+raw", 0, 512), ("bf16", 512, 768))
    return np.concatenate(cols), segs, np.concatenate(rows), tsegs


def _layout_weights(w, cols, rows, n_query_rows):
    w = jnp.concatenate([w, jnp.zeros((w.shape[0], 1), w.dtype)], axis=1)
    scale = jnp.where(jnp.arange(rows.size) < n_query_rows, QK_SCALE, 1.0).astype(w.dtype)
    return w[:, cols].astype(BF16), (w[:, rows] * scale).T.astype(BF16)


def _compress_weights(pe, w1, w2, for_keys):
    half = L_CMP // 2 * HEAD_DIM
    w1ab = jnp.concatenate([w1[:half], w1[half:]], axis=1)
    w1ab = w1ab.reshape(CMP_STRIDE, HEAD_DIM, 2 * CMP_HIDDEN)
    zeros = jnp.zeros_like(w1ab)
    w1_by_tok = jnp.concatenate([jnp.concatenate([w1ab, zeros], axis=2),
                                 jnp.concatenate([zeros, w1ab], axis=2)], axis=1).astype(BF16)
    w2p = jnp.zeros((C_GROUPS, CMP_HIDDEN, LANES), F32)
    for g in range(C_GROUPS):
        if for_keys:
            w2p = w2p.at[g, :, g * HALF:(g + 1) * HALF].set(w2[:, :HALF])
            w2p = w2p.at[g, :, HEAD_DIM + g * HALF:HEAD_DIM + (g + 1) * HALF].set(w2[:, HALF:])
        else:
            w2p = w2p.at[g, :, g * HEAD_DIM:(g + 1) * HEAD_DIM].set(w2)
    if not for_keys:
        w2p = w2p.transpose(0, 2, 1)
    return [w1_by_tok, w1.astype(F32), pe.astype(F32).reshape(1, L_CMP * HEAD_DIM), w2p.astype(BF16)]


def _even_projection_spec(w_in):
    cols, segs, rows, tsegs = _even_layout()
    w, wt = _layout_weights(w_in, cols, rows, 2 * A_HEADS * HEAD_DIM + B_HEADS * HEAD_DIM)
    return w, wt, segs, tsegs, None


def _odd_projection_spec(w_in, conv_w, a_log, dt_bias):
    cols, segs, rows, tsegs = _odd_layout()
    w, wt = _layout_weights(w_in, cols, rows, C_HEADS * HEAD_DIM)
    gate_params = jnp.zeros((2, LANES), F32).at[:, :D_HEADS].set(jnp.stack([a_log, dt_bias]).astype(F32))
    return w, wt, segs, tsegs, (conv_w.astype(F32), gate_params)


def _even_mixers(projected, layer_idx, b_forget, lq1, lk1, lq2, lk2, subln_g, w_out):
    k_a, k_b, forget_logits, gate, qt_a, rest_t = projected
    lam_init = 0.8 - 0.6 * math.exp(-0.3 * layer_idx)
    vec = lambda a: a.astype(F32).reshape(1, -1)
    const = lambda shape: (lambda tq, tk, tile: pl.BlockSpec(shape, lambda i, hb, *t: (0, 0)))
    nb = B_HEADS // 2
    per_step = 4
    oa = _flash("diff", qt_a, k_a, rest_t, A_HEADS, per_step, 0, 0, nb, False,
                extra=[vec(lq1), vec(lk1), vec(lq2), vec(lk2), subln_g.astype(F32).reshape(LANES, 1)],
                extra_specs=[const((1, HEAD_DIM))] * 4 + [const((LANES, 1))], lam_init=lam_init)
    bias = _forget_cumsum(forget_logits, b_forget)
    bias_spec = lambda tq, tk, tile: pl.BlockSpec((None, tk, LANES), lambda i, hb, *t: (i, tile(*t)[1], 0))
    ob = _flash("fox", rest_t, k_b, rest_t, nb, per_step, 0, 0, nb + A_HEADS, False,
                extra=[bias], extra_specs=[bias_spec])
    return gate, [oa, ob], w_out.astype(BF16)


def _odd_mixers(projected, pe_k, pe_v, w1_k, w2_k, w1_v, w2_v, dn_norm_g, w_out):
    dn_q, dn_k, dn_v, small, k_rot, cmp_in, gate, q_raw_t, q_rot_t, v_t = projected
    b, s, _ = small.shape
    n_cmp = (s - L_CMP) // CMP_STRIDE + 1
    kcmp, vcmp_t = _compress(cmp_in, _compress_weights(pe_k, w1_k, w2_k, True),
                             _compress_weights(pe_v, w1_v, w2_v, False))
    o_cmp, sel = _cmp_select(q_raw_t, kcmp, vcmp_t, n_cmp)
    sel_spec = lambda tq, tk, tile: pl.BlockSpec((None, LANES, tq), lambda i, hb, *t: (i, 0, tile(*t)[0]))
    o_slc = _flash("sel", q_rot_t, k_rot, v_t, C_HPG, C_HPG, 0, 0, 0, True, extra=[sel], extra_specs=[sel_spec])
    o_win = _flash("win", q_rot_t, k_rot, v_t, C_HPG, C_HPG, 0, 1, 1, True)
    gates = small[:, :, :2 * D_HEADS].reshape(b, s, 2, D_HEADS).transpose(0, 3, 1, 2)
    od = _gated_deltanet(dn_q, dn_k, dn_v, gates, dn_norm_g)
    rows = _paired_head_order(w_out.shape[0])
    return gate, [o_cmp, o_slc, o_win, od, small], w_out[rows].astype(BF16)


def kernel(x, c, positions, norm_g, w_mod, b_mod, w_out, final_norm_g, w_in_even, b_forget, lambda_q1, lambda_k1,
           lambda_q2, lambda_k2, subln_g, w_in_odd, cmp_pe_k, cmp_pe_v, cmp_w1_k, cmp_w2_k, cmp_w1_v, cmp_w2_v,
           conv_w, a_log, dt_bias, dn_norm_g):
    depth = norm_g.shape[0]
    rope = _rope_tables(positions)
    mod = _modulation(c, w_mod, b_mod)
    specs = [_even_projection_spec(w_in_even[l // 2]) if l % 2 == 0 else
             _odd_projection_spec(w_in_odd[l // 2], conv_w[l // 2], a_log[l // 2], dt_bias[l // 2])
             for l in range(depth)]
    projected = _projection(x, mod[0], norm_g[0], rope, *specs[0])
    for l in range(depth):
        i = l // 2
        if l % 2 == 0:
            gate, branches, w_o = _even_mixers(projected, l, b_forget[i], lambda_q1[i], lambda_k1[i], lambda_q2[i],
                                               lambda_k2[i], subln_g[i], w_out[l])
        else:
            gate, branches, w_o = _odd_mixers(projected, cmp_pe_k[i], cmp_pe_v[i], cmp_w1_k[i], cmp_w2_k[i],
                                              cmp_w1_v[i], cmp_w2_v[i], dn_norm_g[i], w_out[l])
        if l + 1 < depth:
            x, *projected = _out_in_projection(x, mod[l], gate, w_o, branches,
                                               (mod[l + 1], norm_g[l + 1], rope) + specs[l + 1])
        else:
            x = _out_projection(x, mod[l], gate, w_o, branches, final_norm_g)
    return x
```

```python
import functools
import math

import jax
import jax.numpy as jnp
import numpy as np
from jax import lax
from jax.experimental import pallas as pl
from jax.experimental.pallas import tpu as pltpu

F32 = jnp.float32
BF16 = jnp.bfloat16
HI = lax.Precision.HIGHEST

LANES = 128
HEAD_DIM = 64
HALF = HEAD_DIM // 2
ROPE_THETA = 10000.0
NORM_EPS = 1e-6
NEG = -1e30
MASK_BIG = 1e30
LOG2E = math.log2(math.e)
QK_SCALE = HEAD_DIM ** -0.5 * LOG2E
BIAS_PIECES = 3
NORM_ROWS = 16
A_HEADS = 4
B_HEADS = 8
C_HEADS = 8
C_GROUPS = 2
C_HPG = C_HEADS // C_GROUPS
L_CMP = 32
CMP_STRIDE = 16
CMP_HIDDEN = 256
L_SEL = 64
N_SEL = 8
WINDOW = 512
N_BRANCH = 3
FORCE_BONUS = 1e4
D_HEADS = 4
D_DK = 128
D_DV = 128
CONV_WIDTH = 4
DN_CHUNK = 64
TOKEN_TILE = 512
VMEM_LIMIT = 56 * 1024 * 1024


def _cparams(sem):
    return pltpu.CompilerParams(dimension_semantics=sem, vmem_limit_bytes=VMEM_LIMIT)


def _dot(a, b, precision=None):
    return jnp.dot(a, b, precision=precision, preferred_element_type=F32)


def _dot_nt(a, b, precision=None):
    return lax.dot_general(a, b, (((1,), (1,)), ((), ())), precision=precision, preferred_element_type=F32)


def _split_bf16(a):
    hi = a.astype(BF16)
    return hi, (a - hi.astype(F32)).astype(BF16)


def _softplus(z):
    return jnp.maximum(z, 0.0) + jnp.log1p(jnp.exp(-jnp.abs(z)))


def _silu(z):
    return z * jax.nn.sigmoid(z)


def _rope_table_kernel(pos_ref, inv_ref, cos_ref, sin_ref, cos_t_ref, sin_t_ref):
    ang = inv_ref[...] * pos_ref[...].astype(F32)
    cos, sin = jnp.cos(ang), jnp.sin(ang)
    cos_t = jnp.concatenate([cos] * 4, axis=0)
    sin_t = jnp.concatenate([-sin, -sin, sin, sin], axis=0)
    cos_t_ref[...] = cos_t
    sin_t_ref[...] = sin_t
    cos_ref[...] = cos_t.T
    sin_ref[...] = sin_t.T


def _rope_tables(positions):
    b, s = positions.shape
    inv = ROPE_THETA ** (-jnp.arange(0, HEAD_DIM, 2, dtype=F32) / HEAD_DIM)
    tok = jax.ShapeDtypeStruct((b, s, LANES), F32)
    feat = jax.ShapeDtypeStruct((b, LANES, s), F32)
    return pl.pallas_call(
        _rope_table_kernel, grid=(b,),
        in_specs=[pl.BlockSpec((None, 1, s), lambda i: (i, 0, 0)),
                  pl.BlockSpec((HALF, 1), lambda i: (0, 0))],
        out_specs=[pl.BlockSpec((None, s, LANES), lambda i: (i, 0, 0))] * 2
        + [pl.BlockSpec((None, LANES, s), lambda i: (i, 0, 0))] * 2,
        out_shape=[tok, tok, feat, feat], compiler_params=_cparams(("parallel",)), name="rope_tables",
    )(positions.reshape(b, 1, s), inv.reshape(HALF, 1))


def _mod_kernel(c_ref, w_ref, b_ref, o_ref):
    o_ref[...] = _dot(_silu(c_ref[...]), w_ref[...], HI) + b_ref[...]


def _modulation(c, w_mod, b_mod):
    depth, d, n = w_mod.shape
    b = c.shape[0]
    tn = 1024
    mod = pl.pallas_call(
        _mod_kernel, grid=(depth, n // tn),
        in_specs=[pl.BlockSpec((b, d), lambda l, j: (0, 0)),
                  pl.BlockSpec((None, d, tn), lambda l, j: (l, 0, j)),
                  pl.BlockSpec((None, 1, tn), lambda l, j: (l, 0, j))],
        out_specs=pl.BlockSpec((None, b, tn), lambda l, j: (l, 0, j)),
        out_shape=jax.ShapeDtypeStruct((depth, b, n), F32),
        compiler_params=_cparams(("parallel", "parallel")), name="modulation",
    )(c, w_mod, b_mod.reshape(depth, 1, n))
    return mod.reshape(depth, b, 3, 1, d)


def _proj_kernel(x_ref, *refs, segs, tsegs, deltanet):
    _proj_body(x_ref[...], *refs, segs=segs, tsegs=tsegs, deltanet=deltanet)


def _proj_body(x, shift_ref, scale_ref, g_ref, cos_ref, sin_ref, cos_t_ref, sin_t_ref, w_ref, wt_ref,
               *rest, segs, tsegs, deltanet):
    h = x * lax.rsqrt(jnp.mean(x * x, axis=-1, keepdims=True) + NORM_EPS) * g_ref[...]
    h = (h * (1.0 + scale_ref[...]) + shift_ref[...]).astype(BF16)
    if deltanet:
        conv_ref, gparam_ref, halo_ref = rest[0], rest[1], rest[-1]
        outs = list(rest[2:-1])

        @pl.when(pl.program_id(1) == 0)
        def _():
            halo_ref[...] = jnp.zeros(halo_ref.shape, F32)
    else:
        outs = list(rest)
    ts = x.shape[0]
    step = 4 * LANES
    conv_off = 0
    work = []

    def tok_item(kind, c0, a, e, o_ref, cols):
        def epilogue(acc):
            if kind == "rope":
                cos, sin = cos_ref[...], sin_ref[...]
                for j in range(0, e - a, LANES):
                    blk = acc[:, j:j + LANES]
                    rot = blk * cos + pltpu.roll(blk, 2 * HALF, axis=1) * sin
                    o_ref[:, a - c0 + j:a - c0 + j + LANES] = rot.astype(o_ref.dtype)
            elif kind == "dn_gates":
                lane = lax.broadcasted_iota(jnp.int32, (1, LANES), 1)
                log_decay = -jnp.exp(gparam_ref[0:1, :]) * _softplus(acc + gparam_ref[1:2, :])
                o_ref[...] = jnp.where(lane < D_HEADS, log_decay,
                                       jnp.where(lane < 2 * D_HEADS, jax.nn.sigmoid(acc), acc))
            elif kind in ("dn_q", "dn_k", "dn_v"):
                wc = conv_ref[:, cols]
                ext = jnp.concatenate([halo_ref[:, cols], acc], axis=0)
                halo_ref[:, cols] = acc[ts - 8:, :]
                y = acc * wc[CONV_WIDTH - 1:CONV_WIDTH, :]
                for back in range(1, CONV_WIDTH):
                    y = y + ext[8 - back:8 - back + ts, :] * wc[CONV_WIDTH - 1 - back:CONV_WIDTH - back, :]
                y = _silu(y)
                for j in range(0, e - a, LANES):
                    blk = y[:, j:j + LANES]
                    if kind != "dn_v":
                        blk = blk * lax.rsqrt(jnp.sum(blk * blk, axis=-1, keepdims=True) + NORM_EPS)
                    if kind == "dn_q":
                        blk = blk * (D_DK ** -0.5)
                    o_ref[:, a - c0 + j:a - c0 + j + LANES] = blk
            else:
                o_ref[:, a - c0:e - c0] = acc.astype(o_ref.dtype)
        return (lambda: _dot(h, w_ref[:, a:e])), epilogue

    def feat_item(kind, r0, a, e, o_ref, raw_ref):
        def epilogue(acc):
            if raw_ref is not None:
                raw_ref[a - r0:e - r0, :] = acc
            if kind in ("rope", "rope+raw"):
                cos, sin = cos_t_ref[...], sin_t_ref[...]
                for j in range(0, e - a, LANES):
                    blk = acc[j:j + LANES, :]
                    rot = blk * cos + pltpu.roll(blk, 2 * HALF, axis=0) * sin
                    o_ref[a - r0 + j:a - r0 + j + LANES, :] = rot.astype(o_ref.dtype)
            else:
                o_ref[a - r0:e - r0, :] = acc.astype(o_ref.dtype)
        return (lambda: _dot_nt(wt_ref[a:e, :], h)), epilogue

    for kind, c0, c1 in segs:
        o_ref = outs.pop(0)
        for a in range(c0, c1, step):
            e = min(a + step, c1)
            cols = None
            if kind in ("dn_q", "dn_k", "dn_v"):
                cols = slice(conv_off, conv_off + (e - a))
                conv_off += e - a
            work.append(tok_item(kind, c0, a, e, o_ref, cols))
    for kind, r0, r1 in tsegs:
        raw_ref = outs.pop(0) if kind == "rope+raw" else None
        o_ref = outs.pop(0)
        for a in range(r0, r1, step):
            work.append(feat_item(kind, r0, a, min(a + step, r1), o_ref, raw_ref))
    acc = work[0][0]()
    for i, (_, epilogue) in enumerate(work):
        nxt = work[i + 1][0]() if i + 1 < len(work) else None
        epilogue(acc)
        acc = nxt


def _projection(x, mod_l, g, rope, w, wt, segs, tsegs, deltanet=None):
    b, s, d = x.shape
    ts = min(TOKEN_TILE, s)
    in_specs, args, out_specs, out_shapes, scratch, sem = _projection_operands(
        b, s, d, ts, mod_l, g, rope, w, wt, segs, tsegs, deltanet)
    return pl.pallas_call(
        functools.partial(_proj_kernel, segs=segs, tsegs=tsegs, deltanet=deltanet is not None), grid=(b, s // ts),
        in_specs=[pl.BlockSpec((None, ts, d), lambda i, j: (i, j, 0))] + in_specs,
        out_specs=out_specs, out_shape=out_shapes, scratch_shapes=scratch,
        compiler_params=_cparams(sem), name="adaln_in_proj",
    )(x, *args)


def _projection_operands(b, s, d, ts, mod_l, g, rope, w, wt, segs, tsegs, deltanet):
    row = lambda width: pl.BlockSpec((None, ts, width), lambda i, j: (i, j, 0))
    col = lambda height: pl.BlockSpec((None, height, ts), lambda i, j: (i, 0, j))
    out_specs, out_shapes = [], []
    for kind, c0, c1 in segs:
        out_specs.append(row(c1 - c0))
        out_shapes.append(jax.ShapeDtypeStruct((b, s, c1 - c0), BF16 if kind in ("rope", "bf16") else F32))
    for kind, r0, r1 in tsegs:
        if kind == "rope+raw":
            out_specs.append(col(r1 - r0))
            out_shapes.append(jax.ShapeDtypeStruct((b, r1 - r0, s), F32))
        out_specs.append(col(r1 - r0))
        out_shapes.append(jax.ShapeDtypeStruct((b, r1 - r0, s), BF16))
    modspec = lambda k: pl.BlockSpec((None, None, 1, d), lambda i, j: (i, k, 0, 0))
    resident = lambda a: pl.BlockSpec(a.shape, lambda i, j: (0, 0), pipeline_mode=pl.Buffered(1))
    in_specs = [modspec(0), modspec(1), pl.BlockSpec((1, d), lambda i, j: (0, 0)),
                row(LANES), row(LANES), col(LANES), col(LANES), resident(w), resident(wt)]
    args = [mod_l, mod_l, g.reshape(1, d), *rope, w, wt]
    scratch = []
    if deltanet is not None:
        in_specs += [pl.BlockSpec(a.shape, lambda i, j: (0, 0)) for a in deltanet]
        args += list(deltanet)
        scratch = [pltpu.VMEM((8, deltanet[0].shape[1]), F32)]
    sem = ("parallel", "arbitrary") if deltanet is not None else ("parallel", "parallel")
    return in_specs, args, out_specs, out_shapes, scratch, sem


def _cum_kernel(x_ref, bias_ref, o_ref):
    s = x_ref.shape[0]
    ii = lax.broadcasted_iota(jnp.int32, (LANES, LANES), 0)
    jj = lax.broadcasted_iota(jnp.int32, (LANES, LANES), 1)
    lower = (ii >= jj).astype(F32)
    lane = lax.broadcasted_iota(jnp.int32, (1, LANES), 1)
    carry = jnp.zeros((1, LANES), F32)
    for r0 in range(0, s, LANES):
        z = x_ref[r0:r0 + LANES, :] + bias_ref[...]
        logf = jnp.minimum(z, 0.0) - jnp.log1p(jnp.exp(-jnp.abs(z)))
        loc = _dot(lower, logf, HI) + carry
        carry = loc[LANES - 1:LANES, :]
        val = loc * (-LOG2E)
        hi = val.astype(BF16)
        rest = val - hi.astype(F32)
        mid = rest.astype(BF16)
        lo = (rest - mid.astype(F32)).astype(BF16)
        piece = jnp.where(lane % BIAS_PIECES == 0, hi, jnp.where(lane % BIAS_PIECES == 1, mid, lo))
        o_ref[r0:r0 + LANES, :] = jnp.where(lane < B_HEADS * BIAS_PIECES, piece, jnp.zeros_like(piece))


def _forget_cumsum(logits, b_forget):
    b, s, _ = logits.shape
    bias = jnp.zeros((1, LANES), F32).at[0, :B_HEADS * BIAS_PIECES].set(jnp.repeat(b_forget.astype(F32), BIAS_PIECES))
    return pl.pallas_call(
        _cum_kernel, grid=(b,),
        in_specs=[pl.BlockSpec((None, s, LANES), lambda i: (i, 0, 0)), pl.BlockSpec((1, LANES), lambda i: (0, 0))],
        out_specs=pl.BlockSpec((None, s, LANES), lambda i: (i, 0, 0)),
        out_shape=jax.ShapeDtypeStruct((b, s, LANES), BF16),
        compiler_params=_cparams(("parallel",)), name="forget_cumsum",
    )(logits, bias)


def _triangle_step(t, nq):
    qi = sum((t >= j * (j + 1) // 2).astype(jnp.int32) for j in range(1, nq))
    return qi, t - qi * (qi + 1) // 2


def _flash_kernel(*refs, mode, tq, tk, nq, nblk, shared_kv, lam_init):
    qt_ref, k_ref, vt_ref = refs[:3]
    m_ref, acc_ref = refs[-2:]
    o_ref = refs[-3]
    extra = refs[3:-3]
    if mode == "win":
        qi, ki = pl.program_id(2), pl.program_id(3)
        kv, last = qi - 1 + ki, ki == pl.num_programs(3) - 1
    else:
        qi, ki = _triangle_step(pl.program_id(2), nq)
        kv, last = ki, ki == qi
    row = lax.broadcasted_iota(jnp.int32, (LANES, 1), 0)
    if mode == "fox":
        slot_rows = (row < HEAD_DIM, row >= HEAD_DIM)
    else:
        slot_rows = ((row % HEAD_DIM) < HALF, (row % HEAD_DIM) >= HALF)
    vrows = acc_ref.shape[1] - NORM_ROWS
    block = lambda j: slice(j * LANES, (j + 1) * LANES)
    half = tq // 2

    @pl.when(ki == 0)
    def _():
        m_ref[...] = jnp.full(m_ref.shape, NEG, F32)
        acc_ref[...] = jnp.zeros(acc_ref.shape, F32)

    def step(kind):
        rr = lax.broadcasted_iota(jnp.int32, (LANES, tq), 0)
        k_extra = [None, None]
        for s in range(2):
            if mode == "sel":
                cb = (kv * tk + lax.broadcasted_iota(jnp.int32, (tk, LANES), 0)) // L_SEL
                ll = lax.broadcasted_iota(jnp.int32, (tk, LANES), 1)
                k_extra[s] = jnp.where(ll == cb + s * (LANES // 4), MASK_BIG, 0.0).astype(BF16)

        def q_extra(j, s):
            if mode == "sel":
                return extra[0][...]
            if mode == "fox":
                first = ((pl.program_id(1) * nblk + j) * 2 + s) * BIAS_PIECES
                return jnp.where((rr >= first) & (rr < first + BIAS_PIECES), 1.0, 0.0).astype(BF16)
            return None
        lo, hi, everything = slice(0, half), slice(half, tq), slice(0, tq)
        parts = {"full": [(everything, everything, None)],
                 "lower": [(lo, lo, lo), (everything, hi, hi)],
                 "upper": [(everything, lo, lo), (hi, hi, hi)]}[kind]
        tri_k = lax.broadcasted_iota(jnp.int32, (half, half), 0)
        tri_q = lax.broadcasted_iota(jnp.int32, (half, half), 1)
        tri = tri_k <= tri_q if kind == "lower" else tri_k > tri_q
        ones = jnp.ones((NORM_ROWS, tk), BF16)
        sts, vts = [], []
        for j in range(nblk):
            qt = qt_ref[block(j), :]
            k = k_ref[...] if shared_kv else k_ref[:, block(j)]
            base = 0 if shared_kv else j * LANES
            for s in range(2):
                qs = jnp.where(slot_rows[s], qt, jnp.zeros_like(qt))
                q_more = q_extra(j, s)
                if q_more is not None:
                    k_more = extra[0][...] if mode == "fox" else k_extra[s]
                    k_all, q_all = jnp.concatenate([k, k_more], axis=1), jnp.concatenate([qs, q_more], axis=0)
                else:
                    k_all, q_all = k, qs
                sts.append([_dot(k_all[krows], q_all[:, lanes]) for krows, lanes, _ in parts])
                v0 = base if vrows == LANES else base + s * vrows
                vts.append(jnp.concatenate([vt_ref[v0:v0 + vrows, :], ones], axis=0))
        for idx in range(2 * nblk):
            for (krows, lanes, tri_rows), st in zip(parts, sts[idx]):
                if tri_rows is not None:
                    if krows == tri_rows:
                        st = jnp.where(tri, st, NEG)
                    elif tri_rows == lo:
                        st = jnp.concatenate([jnp.where(tri, st[:half], NEG), st[half:]], axis=0)
                    else:
                        st = jnp.concatenate([st[:half], jnp.where(tri, st[half:], NEG)], axis=0)
                m_prev = m_ref[idx, :, lanes]
                m_new = jnp.maximum(m_prev, jnp.max(st, axis=0, keepdims=True))
                alpha = jnp.exp2(m_prev - m_new)
                p = jnp.exp2(st - m_new).astype(BF16)
                acc_ref[idx, :, lanes] = alpha * acc_ref[idx, :, lanes] + _dot(vts[idx][:, krows], p)
                m_ref[idx, :, lanes] = m_new

    if mode == "win":
        pl.when((ki == 0) & (kv >= 0))(functools.partial(step, "upper"))
        pl.when(ki == 1)(functools.partial(step, "lower"))
    else:
        pl.when(ki < qi)(functools.partial(step, "full"))
        pl.when(ki == qi)(functools.partial(step, "lower"))

    @pl.when(last)
    def _():
        for j in range(nblk):
            o0 = acc_ref[2 * j, :vrows, :] / acc_ref[2 * j, vrows:vrows + 1, :]
            o1 = acc_ref[2 * j + 1, :vrows, :] / acc_ref[2 * j + 1, vrows:vrows + 1, :]
            if mode == "diff":
                lq1, lk1, lq2, lk2, subg = (r[...] for r in extra)
                lam = (jnp.exp(jnp.sum(lq1 * lk1, axis=1, keepdims=True))
                       - jnp.exp(jnp.sum(lq2 * lk2, axis=1, keepdims=True)) + lam_init)
                o = o0 - lam * o1
                o = o * lax.rsqrt(jnp.mean(o * o, axis=0, keepdims=True) + NORM_EPS) * subg * (1.0 - lam_init)
            else:
                o = jnp.concatenate([o0, o1], axis=0)
            o_ref[:, block(j)] = o.T.astype(o_ref.dtype)


def _flash(mode, qt, k, vt, n_qblocks, nblk, q0, k0, v0, shared_kv, extra=(), extra_specs=(), lam_init=0.0):
    b, s, _ = k.shape
    tq = tk = min(TOKEN_TILE, s)
    nq = s // tq
    if mode == "win":
        assert WINDOW == tk
        steps, sem = (nq, 2), ("parallel", "arbitrary")
        tile = lambda qi, ki: (qi, jnp.maximum(qi - 1 + ki, 0))
    else:
        steps, sem = (nq * (nq + 1) // 2,), ("arbitrary",)
        tile = lambda t: _triangle_step(t, nq)
    assert n_qblocks % nblk == 0 and q0 % nblk == 0 and (shared_kv or (k0 % nblk == 0 and v0 % nblk == 0))
    wide = nblk * LANES
    if shared_kv:
        k_spec = pl.BlockSpec((None, tk, LANES), lambda i, hb, *t: (i, tile(*t)[1], k0))
        v_spec = pl.BlockSpec((None, LANES, tk), lambda i, hb, *t: (i, v0, tile(*t)[1]))
    else:
        k_spec = pl.BlockSpec((None, tk, wide), lambda i, hb, *t: (i, tile(*t)[1], k0 // nblk + hb))
        v_spec = pl.BlockSpec((None, wide, tk), lambda i, hb, *t: (i, v0 // nblk + hb, tile(*t)[1]))
    in_specs = [pl.BlockSpec((None, wide, tq), lambda i, hb, *t: (i, q0 // nblk + hb, tile(*t)[0])), k_spec, v_spec]
    for spec in extra_specs:
        in_specs.append(spec(tq, tk, tile))
    vrows = LANES if mode == "diff" else HEAD_DIM
    return pl.pallas_call(
        functools.partial(_flash_kernel, mode=mode, tq=tq, tk=tk, nq=nq, nblk=nblk, shared_kv=shared_kv,
                          lam_init=lam_init),
        grid=(b, n_qblocks // nblk) + steps, in_specs=in_specs,
        out_specs=pl.BlockSpec((None, tq, wide), lambda i, hb, *t: (i, tile(*t)[0], hb)),
        out_shape=jax.ShapeDtypeStruct((b, s, n_qblocks * LANES), BF16),
        scratch_shapes=[pltpu.VMEM((2 * nblk, 1, tq), F32), pltpu.VMEM((2 * nblk, vrows + NORM_ROWS, tq), F32)],
        compiler_params=_cparams(("parallel", "parallel") + sem),
        name="flash_" + mode,
    )(qt, k, vt, *extra)


def _compress_kernel(xk_ref, xv_ref, w1_by_tok_k, w1_k, pe_k, w2_k, w1_by_tok_v, w1_v, pe_v, w2_v, ok_ref, ov_ref,
                     pe_term_s):
    cn = xk_ref.shape[0] // CMP_STRIDE

    @pl.when(pl.program_id(0) == 0)
    def _():
        for idx, (pe, w1) in enumerate(((pe_k, w1_k), (pe_v, w1_v))):
            pe_term_s[idx] = jnp.broadcast_to(_dot(pe[...], w1[...], HI), (8, CMP_HIDDEN))

    def run(idx, x_ref, w1_by_tok, w2, o_ref, transposed):
        pe_term = pe_term_s[idx][0:1, :]
        ab = jnp.zeros((cn, C_GROUPS * 2 * CMP_HIDDEN), F32)
        for tok in range(CMP_STRIDE):
            rows = x_ref[pl.ds(tok, cn, stride=CMP_STRIDE), :]
            ab = ab + _dot(rows.astype(BF16), w1_by_tok[tok])
        out = 0.0
        for g in range(C_GROUPS):
            first = ab[:, g * 2 * CMP_HIDDEN:(g * 2 + 1) * CMP_HIDDEN]
            second = ab[:, (g * 2 + 1) * CMP_HIDDEN:(g + 1) * 2 * CMP_HIDDEN]
            hid = first + pltpu.roll(second, cn - 1, axis=0) + pe_term
            act = _silu(hid).astype(BF16)
            out = out + (_dot_nt(w2[g], act) if transposed else _dot(act, w2[g]))
        o_ref[...] = out
    run(0, xk_ref, w1_by_tok_k, w2_k, ok_ref, False)
    run(1, xv_ref, w1_by_tok_v, w2_v, ov_ref, True)


def _compress(cmp_in, weights_k, weights_v):
    b, s, width = cmp_in.shape
    cn = s // CMP_STRIDE
    full = lambda a: pl.BlockSpec(a.shape, lambda i: (0,) * a.ndim)
    return pl.pallas_call(
        _compress_kernel, grid=(b,),
        in_specs=[pl.BlockSpec((None, s, LANES), lambda i: (i, 0, 0)), pl.BlockSpec((None, s, LANES), lambda i: (i, 0, 1))]
        + [full(a) for a in weights_k + weights_v],
        out_specs=[pl.BlockSpec((None, cn, LANES), lambda i: (i, 0, 0)),
                   pl.BlockSpec((None, LANES, cn), lambda i: (i, 0, 0))],
        out_shape=[jax.ShapeDtypeStruct((b, cn, LANES), F32), jax.ShapeDtypeStruct((b, LANES, cn), F32)],
        scratch_shapes=[pltpu.VMEM((2, 8, CMP_HIDDEN), F32)],
        compiler_params=_cparams(("arbitrary",)), name="nsa_compress",
    )(cmp_in, cmp_in, *weights_k, *weights_v)


def _cmp_select_kernel(qt_ref, kc_ref, vct_ref, o_ref, sel_ref, *, tq, n_blk, n_cmp):
    qi = pl.program_id(1)
    cn = kc_ref.shape[0]
    kc_hi, kc_lo = _split_bf16(kc_ref[...])
    vct = vct_ref[...].astype(BF16)
    row = lax.broadcasted_iota(jnp.int32, (LANES, 1), 0)
    slot_rows = ((row % HEAD_DIM) < HALF, (row % HEAD_DIM) >= HALF)
    t_row = qi * tq + lax.broadcasted_iota(jnp.int32, (1, tq), 1)
    m_col = lax.broadcasted_iota(jnp.int32, (cn, 1), 0)
    valid = (m_col * CMP_STRIDE + L_CMP - 1 <= t_row) & (m_col < n_cmp)
    any_valid = (t_row >= L_CMP - 1).astype(F32)
    heads = [(p_blk, s) for p_blk in range(C_HPG) for s in range(C_GROUPS)]
    scores = []
    for p_blk, s in heads:
        q_hi, q_lo = _split_bf16(jnp.where(slot_rows[s], qt_ref[p_blk * LANES:(p_blk + 1) * LANES, :], 0.0))
        scores.append(_dot(kc_hi, q_hi) + _dot(kc_lo, q_hi) + _dot(kc_hi, q_lo))
    probs = []
    for sc in scores:
        sc = jnp.where(valid, sc, NEG)
        e = jnp.exp2(sc - jnp.max(sc, axis=0, keepdims=True))
        probs.append(e / jnp.sum(e, axis=0, keepdims=True) * any_valid)
    outs = [_dot(vct[s * HEAD_DIM:(s + 1) * HEAD_DIM, :], p.astype(BF16)) for (_, s), p in zip(heads, probs)]
    psum = [sum(p for (_, s), p in zip(heads, probs) if s == g) for g in range(C_GROUPS)]
    for p_blk in range(C_HPG):
        o_ref[:, p_blk * LANES:(p_blk + 1) * LANES] = jnp.concatenate(
            outs[C_GROUPS * p_blk:C_GROUPS * (p_blk + 1)], axis=0).T.astype(o_ref.dtype)
    jb = lax.broadcasted_iota(jnp.int32, (n_blk, cn), 0)
    mm = lax.broadcasted_iota(jnp.int32, (n_blk, cn), 1)
    overlap = ((mm * CMP_STRIDE < jb * L_SEL + L_SEL) & (mm * CMP_STRIDE + L_CMP > jb * L_SEL)
               & (mm < n_cmp)).astype(BF16)
    j = lax.broadcasted_iota(jnp.int32, (n_blk, tq), 0)
    cur = (qi * tq + lax.broadcasted_iota(jnp.int32, (n_blk, tq), 1)) // L_SEL
    forced = (j == 0) | (j == cur) | (j == cur - 1)
    n_top = min(N_SEL, n_blk)
    pad_rows = LANES // 4 - n_blk
    parts = []
    for s in range(C_GROUPS):
        p_hi, p_lo = _split_bf16(psum[s])
        imp = _dot(overlap, p_hi) + _dot(overlap, p_lo)
        score = jnp.where(j > cur, NEG, imp + jnp.where(forced, FORCE_BONUS, 0.0))
        rank = jnp.zeros((n_blk, tq), jnp.int32)
        for jp in range(n_blk):
            r = score[jp:jp + 1, :]
            rank = rank + ((r > score) | ((r == score) & (jp < j))).astype(jnp.int32)
        parts.append(jnp.where(rank < n_top, 0.0, -1.0))
        if pad_rows:
            parts.append(jnp.zeros((pad_rows, tq), F32))
    parts.append(jnp.zeros((LANES // 2, tq), F32))
    sel_ref[...] = jnp.concatenate(parts, axis=0).astype(sel_ref.dtype)


def _cmp_select(q_raw_t, kcmp, vcmp_t, n_cmp):
    b, width, s = q_raw_t.shape
    cn = kcmp.shape[1]
    tq = min(TOKEN_TILE, s)
    n_blk = s // L_SEL
    return pl.pallas_call(
        functools.partial(_cmp_select_kernel, tq=tq, n_blk=n_blk, n_cmp=n_cmp), grid=(b, s // tq),
        in_specs=[pl.BlockSpec((None, width, tq), lambda i, j: (i, 0, j)),
                  pl.BlockSpec((None, cn, LANES), lambda i, j: (i, 0, 0)),
                  pl.BlockSpec((None, LANES, cn), lambda i, j: (i, 0, 0))],
        out_specs=[pl.BlockSpec((None, tq, width), lambda i, j: (i, j, 0)),
                   pl.BlockSpec((None, LANES, tq), lambda i, j: (i, 0, j))],
        out_shape=[jax.ShapeDtypeStruct((b, s, width), BF16), jax.ShapeDtypeStruct((b, LANES, s), BF16)],
        compiler_params=_cparams(("parallel", "parallel")), name="nsa_cmp_select",
    )(q_raw_t, kcmp, vcmp_t)


GDN_HEADS = 2
GDN_BATCH = 2
GDN_UNROLL = 8


def _gdn_kernel(q_ref, k_ref, v_ref, gate_ref, ng_ref, o_ref, mc_s, n_s, gt_s):
    s_len = q_ref.shape[0]
    c = DN_CHUNK
    n_chunks = s_len // c
    lanes = [slice(j * LANES, (j + 1) * LANES) for j in range(GDN_HEADS)]

    sc = GDN_BATCH * c
    ii = lax.broadcasted_iota(jnp.int32, (sc, sc), 0)
    jj = lax.broadcasted_iota(jnp.int32, (sc, sc), 1)
    same = (ii // c) == (jj // c)
    causal, strict, upper, eye = same & (ii >= jj), same & (ii > jj), same & (ii <= jj), ii == jj
    chunk_end = same & (jj % c == c - 1)
    ng = ng_ref[...]

    def load(n, j):
        r0 = pl.multiple_of(n * sc, sc)
        return (q_ref[pl.ds(r0, sc), lanes[j]], k_ref[pl.ds(r0, sc), lanes[j]], v_ref[pl.ds(r0, sc), lanes[j]],
                gate_ref[j, pl.ds(r0, sc), 0:1], gate_ref[j, pl.ds(r0, sc), 1:2])

    def prepare(q, k, v, g, beta):
        g_row = jnp.sum(jnp.where(eye, g, 0.0), axis=0, keepdims=True)
        gc_col = jnp.sum(jnp.where(causal, g_row, 0.0), axis=1, keepdims=True)
        gc_row = jnp.sum(jnp.where(upper, g, 0.0), axis=0, keepdims=True)
        g_last = jnp.sum(jnp.where(chunk_end, gc_row, 0.0), axis=1, keepdims=True)
        decay = jnp.where(causal, jnp.exp(jnp.where(causal, gc_col - gc_row, 0.0)), 0.0)
        eg = jnp.exp(gc_col)
        kb = k * beta
        kbf = k.astype(BF16)
        raw = _dot_nt(kb.astype(BF16), kbf)
        raw_qk = _dot_nt(q.astype(BF16), kbf)
        rhs = jnp.concatenate([v * beta, kb * eg], axis=1).astype(BF16)
        kd = k * jnp.exp(g_last - gc_col)
        kd_t = [kd[t * c:(t + 1) * c].T.astype(BF16) for t in range(GDN_BATCH)]
        g_tot = [jnp.exp(g_last[t * c:t * c + 1]) for t in range(GDN_BATCH)]
        return raw, raw_qk, decay, rhs, kd_t, g_tot, q * eg

    def solve(prepared):
        n = len(prepared)
        decay = [p[2] for p in prepared]
        x = [-jnp.where(strict, prepared[i][0] * decay[i], 0.0) for i in range(n)]
        inv = [jnp.where(eye, 1.0, 0.0) + x[i] for i in range(n)]
        xb = [x[i].astype(BF16) for i in range(n)]
        x = [_dot(xb[i], xb[i]) for i in range(n)]
        for _ in range(int(math.log2(c)) - 2):
            xb = [x[i].astype(BF16) for i in range(n)]
            both = [_dot(jnp.concatenate([xb[i], inv[i].astype(BF16)], axis=0), xb[i]) for i in range(n)]
            x = [both[i][:sc] for i in range(n)]
            inv = [inv[i] + both[i][sc:] for i in range(n)]
        last = [_dot(inv[i].astype(BF16), x[i].astype(BF16)) for i in range(n)]
        inv = [inv[i] + last[i] for i in range(n)]
        sol = [_dot(inv[i].astype(BF16), prepared[i][3]).astype(BF16) for i in range(n)]
        qk = [(prepared[i][1] * decay[i]).astype(BF16) for i in range(n)]
        qo = [_dot(qk[i], sol[i]) for i in range(n)]
        mn = [[_dot(prepared[i][4][t], sol[i][t * c:(t + 1) * c]) for t in range(GDN_BATCH)] for i in range(n)]
        return [(qo[i], prepared[i][6], mn[i], prepared[i][5]) for i in range(n)]

    def store(n, j, qo, q_dec, mn, g_tot):
        r0 = pl.multiple_of(n * sc, sc)
        o_ref[pl.ds(r0, sc), lanes[j]] = qo[:, :D_DV]
        q_eff = (q_dec - qo[:, D_DV:]).astype(BF16)
        for t in range(GDN_BATCH):
            n_s[j, n * GDN_BATCH + t] = mn[t][:, :D_DV]
            mc_s[j, n * GDN_BATCH + t, :D_DK, :] = mn[t][:, D_DV:].astype(BF16)
            mc_s[j, n * GDN_BATCH + t, D_DK:, :] = q_eff[t * c:(t + 1) * c]
            gt_s[j, n * GDN_BATCH + t] = jnp.broadcast_to(g_tot[t], (8, LANES))

    def prepare_some(i, carry):
        items = [(i * GDN_UNROLL + u, j) for u in range(GDN_UNROLL) for j in range(GDN_HEADS)]
        results = solve([prepare(*operands) for operands in [load(n, j) for n, j in items]])
        for (n, j), res in zip(items, results):
            store(n, j, *res)
        return carry

    lax.fori_loop(0, n_chunks // (GDN_BATCH * GDN_UNROLL), prepare_some, 0)

    def advance(n, states):
        r0 = pl.multiple_of(n * c, c)
        operands = [(o_ref[pl.ds(r0, c), lanes[j]], gt_s[j, n], mc_s[j, n], n_s[j, n]) for j in range(GDN_HEADS)]
        prods = [_dot(operands[j][2], states[j].astype(BF16)) for j in range(GDN_HEADS)]
        out = [states[j] * operands[j][1][0:1, :] - prods[j][:D_DK] + operands[j][3] for j in range(GDN_HEADS)]
        for j in range(GDN_HEADS):
            o_ref[pl.ds(r0, c), lanes[j]] = operands[j][0] + prods[j][D_DK:]
        return tuple(out)

    lax.fori_loop(0, n_chunks, advance, tuple(jnp.zeros((D_DK, D_DV), F32) for _ in range(GDN_HEADS)))
    for j in range(GDN_HEADS):
        o = o_ref[:, lanes[j]]
        o_ref[:, lanes[j]] = o * lax.rsqrt(jnp.mean(o * o, axis=-1, keepdims=True) + NORM_EPS) * ng


def _gated_deltanet(q, k, v, gates, norm_g):
    b, s, _ = q.shape
    hp, width = GDN_HEADS, GDN_HEADS * LANES
    groups = D_HEADS // hp
    n_chunks = s // DN_CHUNK
    assert n_chunks % (GDN_BATCH * GDN_UNROLL) == 0
    col = pl.BlockSpec((None, s, width), lambda i, h: (i, 0, h))
    return pl.pallas_call(
        _gdn_kernel, grid=(b, groups),
        in_specs=[col, col, col, pl.BlockSpec((None, hp, s, 2), lambda i, h: (i, h, 0, 0)),
                  pl.BlockSpec((1, D_DV), lambda i, h: (0, 0))],
        out_specs=col,
        out_shape=jax.ShapeDtypeStruct((b, s, D_HEADS * D_DV), F32),
        scratch_shapes=[pltpu.VMEM((hp, n_chunks, D_DK + DN_CHUNK, D_DV), BF16),
                        pltpu.VMEM((hp, n_chunks, D_DK, D_DV), F32), pltpu.VMEM((hp, n_chunks, 8, LANES), F32)],
        compiler_params=_cparams(("parallel", "parallel")), name="gated_deltanet",
    )(q, k, v, gates, norm_g.astype(F32).reshape(1, D_DV))


def _out_kernel(*refs, odd, final):
    rest = list(refs[:-1])
    fin_ref = rest.pop() if final else None
    out = _out_body(*rest, odd=odd)
    if final:
        out = out * lax.rsqrt(jnp.mean(out * out, axis=-1, keepdims=True) + NORM_EPS) * fin_ref[...]
    refs[-1][...] = out


def _out_in_kernel(*refs, odd, n_out_inputs, segs, tsegs, deltanet):
    x_new = _out_body(*refs[:n_out_inputs], odd=odd)
    proj_refs = list(refs[n_out_inputs:])
    n_proj_inputs = 9 + (2 if deltanet else 0)
    x_out_ref = proj_refs.pop(n_proj_inputs)
    x_out_ref[...] = x_new
    _proj_body(x_new, *proj_refs, segs=segs, tsegs=tsegs, deltanet=deltanet)


def _out_body(x_ref, mg_ref, gate_ref, w_ref, *rest, odd):
    half = w_ref.shape[0] // 2
    sg = _silu(gate_ref[...].astype(F32))
    if odd:
        cmp_ref, slc_ref, win_ref, od_ref, small_ref = rest
        lane = lax.broadcasted_iota(jnp.int32, (1, LANES), 1)
        bg = jax.nn.sigmoid(small_ref[...])
        blocks = []
        for p_blk in range(C_HPG):
            sl = slice(p_blk * LANES, (p_blk + 1) * LANES)
            acc = 0.0
            for br, ref in enumerate((cmp_ref, slc_ref, win_ref)):
                ca = 2 * D_HEADS + p_blk * N_BRANCH + br
                cb = 2 * D_HEADS + (p_blk + C_HPG) * N_BRANCH + br
                acc = acc + jnp.where(lane < HEAD_DIM, bg[:, ca:ca + 1], bg[:, cb:cb + 1]) * ref[:, sl]
            blocks.append(acc)
        first = jnp.concatenate(blocks, axis=1)
        second = od_ref[...]
    else:
        first, second = rest[0][...], rest[1][...]
    y = (_dot((first * sg[:, :half]).astype(BF16), w_ref[:half, :])
         + _dot((second * sg[:, half:]).astype(BF16), w_ref[half:, :]))
    return x_ref[...] + mg_ref[...] * y


def _out_operands(x, mod_l, gate, w, branches, ts):
    d = x.shape[-1]
    row = lambda width: pl.BlockSpec((None, ts, width), lambda i, j: (i, j, 0))
    in_specs = [row(d), pl.BlockSpec((None, None, 1, d), lambda i, j: (i, 2, 0, 0)), row(gate.shape[-1]),
                pl.BlockSpec(w.shape, lambda i, j: (0, 0), pipeline_mode=pl.Buffered(1))]
    in_specs += [row(a.shape[-1]) for a in branches]
    return in_specs, [x, mod_l, gate, w, *branches]


def _out_projection(x, mod_l, gate, w, branches, final_g=None):
    b, s, d = x.shape
    ts = min(TOKEN_TILE, s)
    in_specs, args = _out_operands(x, mod_l, gate, w, branches, ts)
    if final_g is not None:
        in_specs.append(pl.BlockSpec((1, d), lambda i, j: (0, 0)))
        args.append(final_g.reshape(1, d))
    return pl.pallas_call(
        functools.partial(_out_kernel, odd=len(branches) > 2, final=final_g is not None), grid=(b, s // ts),
        in_specs=in_specs, out_specs=pl.BlockSpec((None, ts, d), lambda i, j: (i, j, 0)),
        out_shape=jax.ShapeDtypeStruct((b, s, d), F32),
        compiler_params=_cparams(("parallel", "parallel")), name="gated_out_proj",
    )(*args)


def _out_in_projection(x, mod_l, gate, w_out, branches, nxt):
    b, s, d = x.shape
    ts = min(TOKEN_TILE, s)
    mod_n, g_n, rope, w, wt, segs, tsegs, deltanet = nxt
    out_specs_in, out_args = _out_operands(x, mod_l, gate, w_out, branches, ts)
    in_specs, args, out_specs, out_shapes, scratch, sem = _projection_operands(
        b, s, d, ts, mod_n, g_n, rope, w, wt, segs, tsegs, deltanet)
    x_spec = pl.BlockSpec((None, ts, d), lambda i, j: (i, j, 0))
    return pl.pallas_call(
        functools.partial(_out_in_kernel, odd=len(branches) > 2, n_out_inputs=len(out_args), segs=segs, tsegs=tsegs,
                          deltanet=deltanet is not None),
        grid=(b, s // ts), in_specs=out_specs_in + in_specs,
        out_specs=[x_spec] + out_specs, out_shape=[jax.ShapeDtypeStruct((b, s, d), F32)] + out_shapes,
        scratch_shapes=scratch, compiler_params=_cparams(sem), name="out_proj_in_proj",
    )(*out_args, *args)


def _pair_cols(a0, b0):
    a, bb = np.arange(a0, a0 + HEAD_DIM), np.arange(b0, b0 + HEAD_DIM)
    return np.concatenate([a[:HALF], bb[:HALF], a[HALF:], bb[HALF:]])


def _paired_head_order(width):
    pairs = [np.concatenate([np.arange(p * HEAD_DIM, (p + 1) * HEAD_DIM),
                             np.arange((p + C_HPG) * HEAD_DIM, (p + C_HPG + 1) * HEAD_DIM)]) for p in range(C_HPG)]
    return np.concatenate(pairs + [np.arange(C_HEADS * HEAD_DIM, width)])


def _even_layout():
    aq, ak, av = 0, 512, 1024
    bq, bk, bv, bf, gate = 1536, 2048, 2560, 3072, 3080
    zero = gate + 1024
    cols = [_pair_cols(ak + 2 * h * HEAD_DIM, ak + (2 * h + 1) * HEAD_DIM) for h in range(A_HEADS)]
    cols.append(np.arange(bk, bk + 512))
    cols.append(np.concatenate([np.repeat(np.arange(bf, bf + B_HEADS), BIAS_PIECES),
                                np.full(LANES - B_HEADS * BIAS_PIECES, zero)]))
    cols.append(np.arange(gate, gate + 1024))
    segs = (("rope", 0, 512), ("bf16", 512, 1024), ("f32", 1024, 1152), ("bf16", 1152, 2176))
    rows = [_pair_cols(aq + 2 * h * HEAD_DIM, aq + (2 * h + 1) * HEAD_DIM) for h in range(A_HEADS)]
    rows += [np.arange(bq, bq + 512), np.arange(av, av + 512), np.arange(bv, bv + 512)]
    tsegs = (("rope", 0, 512), ("bf16", 512, 2048))
    return np.concatenate(cols), segs, np.concatenate(rows), tsegs


def _odd_layout():
    cq, kc, vc, ks, vs, kw, vw, cg = 0, 512, 640, 768, 896, 1024, 1152, 1280
    dq, da, db, gate = 1304, 2840, 2844, 2848
    zero = gate + 1024
    small = np.concatenate([np.arange(da, da + 2 * D_HEADS), np.arange(cg, cg + C_HEADS * N_BRANCH)])
    cols = [np.arange(dq, dq + 1536), np.concatenate([small, np.full(LANES - small.size, zero)])]
    cols += [_pair_cols(ks, ks + HEAD_DIM), _pair_cols(kw, kw + HEAD_DIM), np.arange(kc, kc + 256)]
    cols.append(gate + _paired_head_order(1024))
    segs = (("dn_q", 0, 512), ("dn_k", 512, 1024), ("dn_v", 1024, 1536), ("dn_gates", 1536, 1664),
            ("rope", 1664, 1920), ("f32", 1920, 2176), ("bf16", 2176, 3200))
    rows = [_pair_cols(cq + p * HEAD_DIM, cq + (p + C_HPG) * HEAD_DIM) for p in range(C_HPG)]
    rows += [np.arange(vs, vs + LANES), np.arange(vw, vw + LANES)]
    tsegs = (("rope+raw", 0, 512), ("bf16", 512, 768))
    return np.concatenate(cols), segs, np.concatenate(rows), tsegs


def _layout_weights(w, cols, rows, n_query_rows):
    w = jnp.concatenate([w, jnp.zeros((w.shape[0], 1), w.dtype)], axis=1)
    scale = jnp.where(jnp.arange(rows.size) < n_query_rows, QK_SCALE, 1.0).astype(w.dtype)
    return w[:, cols].astype(BF16), (w[:, rows] * scale).T.astype(BF16)


def _compress_weights(pe, w1, w2, for_keys):
    half = L_CMP // 2 * HEAD_DIM
    w1ab = jnp.concatenate([w1[:half], w1[half:]], axis=1)
    w1ab = w1ab.reshape(CMP_STRIDE, HEAD_DIM, 2 * CMP_HIDDEN)
    zeros = jnp.zeros_like(w1ab)
    w1_by_tok = jnp.concatenate([jnp.concatenate([w1ab, zeros], axis=2),
                                 jnp.concatenate([zeros, w1ab], axis=2)], axis=1).astype(BF16)
    w2p = jnp.zeros((C_GROUPS, CMP_HIDDEN, LANES), F32)
    for g in range(C_GROUPS):
        if for_keys:
            w2p = w2p.at[g, :, g * HALF:(g + 1) * HALF].set(w2[:, :HALF])
            w2p = w2p.at[g, :, HEAD_DIM + g * HALF:HEAD_DIM + (g + 1) * HALF].set(w2[:, HALF:])
        else:
            w2p = w2p.at[g, :, g * HEAD_DIM:(g + 1) * HEAD_DIM].set(w2)
    if not for_keys:
        w2p = w2p.transpose(0, 2, 1)
    return [w1_by_tok, w1.astype(F32), pe.astype(F32).reshape(1, L_CMP * HEAD_DIM), w2p.astype(BF16)]


def _even_projection_spec(w_in):
    cols, segs, rows, tsegs = _even_layout()
    w, wt = _layout_weights(w_in, cols, rows, 2 * A_HEADS * HEAD_DIM + B_HEADS * HEAD_DIM)
    return w, wt, segs, tsegs, None


def _odd_projection_spec(w_in, conv_w, a_log, dt_bias):
    cols, segs, rows, tsegs = _odd_layout()
    w, wt = _layout_weights(w_in, cols, rows, C_HEADS * HEAD_DIM)
    gate_params = jnp.zeros((2, LANES), F32).at[:, :D_HEADS].set(jnp.stack([a_log, dt_bias]).astype(F32))
    return w, wt, segs, tsegs, (conv_w.astype(F32), gate_params)


def _even_mixers(projected, layer_idx, b_forget, lq1, lk1, lq2, lk2, subln_g, w_out):
    k_a, k_b, forget_logits, gate, qt_a, rest_t = projected
    lam_init = 0.8 - 0.6 * math.exp(-0.3 * layer_idx)
    vec = lambda a: a.astype(F32).reshape(1, -1)
    const = lambda shape: (lambda tq, tk, tile: pl.BlockSpec(shape, lambda i, hb, *t: (0, 0)))
    nb = B_HEADS // 2
    per_step = 4
    oa = _flash("diff", qt_a, k_a, rest_t, A_HEADS, per_step, 0, 0, nb, False,
                extra=[vec(lq1), vec(lk1), vec(lq2), vec(lk2), subln_g.astype(F32).reshape(LANES, 1)],
                extra_specs=[const((1, HEAD_DIM))] * 4 + [const((LANES, 1))], lam_init=lam_init)
    bias = _forget_cumsum(forget_logits, b_forget)
    bias_spec = lambda tq, tk, tile: pl.BlockSpec((None, tk, LANES), lambda i, hb, *t: (i, tile(*t)[1], 0))
    ob = _flash("fox", rest_t, k_b, rest_t, nb, per_step, 0, 0, nb + A_HEADS, False,
                extra=[bias], extra_specs=[bias_spec])
    return gate, [oa, ob], w_out.astype(BF16)


def _odd_mixers(projected, pe_k, pe_v, w1_k, w2_k, w1_v, w2_v, dn_norm_g, w_out):
    dn_q, dn_k, dn_v, small, k_rot, cmp_in, gate, q_raw_t, q_rot_t, v_t = projected
    b, s, _ = small.shape
    n_cmp = (s - L_CMP) // CMP_STRIDE + 1
    kcmp, vcmp_t = _compress(cmp_in, _compress_weights(pe_k, w1_k, w2_k, True),
                             _compress_weights(pe_v, w1_v, w2_v, False))
    o_cmp, sel = _cmp_select(q_raw_t, kcmp, vcmp_t, n_cmp)
    sel_spec = lambda tq, tk, tile: pl.BlockSpec((None, LANES, tq), lambda i, hb, *t: (i, 0, tile(*t)[0]))
    o_slc = _flash("sel", q_rot_t, k_rot, v_t, C_HPG, C_HPG, 0, 0, 0, True, extra=[sel], extra_specs=[sel_spec])
    o_win = _flash("win", q_rot_t, k_rot, v_t, C_HPG, C_HPG, 0, 1, 1, True)
    gates = small[:, :, :2 * D_HEADS].reshape(b, s, 2, D_HEADS).transpose(0, 3, 1, 2)
    od = _gated_deltanet(dn_q, dn_k, dn_v, gates, dn_norm_g)
    rows = _paired_head_order(w_out.shape[0])
    return gate, [o_cmp, o_slc, o_win, od, small], w_out[rows].astype(BF16)


def kernel(x, c, positions, norm_g, w_mod, b_mod, w_out, final_norm_g, w_in_even, b_forget, lambda_q1, lambda_k1,
           lambda_q2, lambda_k2, subln_g, w_in_odd, cmp_pe_k, cmp_pe_v, cmp_w1_k, cmp_w2_k, cmp_w1_v, cmp_w2_v,
           conv_w, a_log, dt_bias, dn_norm_g):
    depth = norm_g.shape[0]
    rope = _rope_tables(positions)
    mod = _modulation(c, w_mod, b_mod)
    specs = [_even_projection_spec(w_in_even[l // 2]) if l % 2 == 0 else
             _odd_projection_spec(w_in_odd[l // 2], conv_w[l // 2], a_log[l // 2], dt_bias[l // 2])
             for l in range(depth)]
    projected = _projection(x, mod[0], norm_g[0], rope, *specs[0])
    for l in range(depth):
        i = l // 2
        if l % 2 == 0:
            gate, branches, w_o = _even_mixers(projected, l, b_forget[i], lambda_q1[i], lambda_k1[i], lambda_q2[i],
                                               lambda_k2[i], subln_g[i], w_out[l])
        else:
            gate, branches, w_o = _odd_mixers(projected, cmp_pe_k[i], cmp_pe_v[i], cmp_w1_k[i], cmp_w2_k[i],
                                              cmp_w1_v[i], cmp_w2_v[i], dn_norm_g[i], w_out[l])
        if l + 1 < depth:
            x, *projected = _out_in_projection(x, mod[l], gate, w_o, branches,
                                               (mod[l + 1], norm_g[l + 1], rope) + specs[l + 1])
        else:
            x = _out_projection(x, mod[l], gate, w_o, branches, final_norm_g)
    return x
```
